```python
import jax
import jax.numpy as jnp
from jax import lax
import numpy as np

D_MODEL = 1024
BATCH = 1
SEQ = 16384
DEPTH = 4
DEC_BATCH = 32
DEC_SEQ = 64
PAST_LEN = 1024

CHUNK = 64
HEAD_DIM = 64
A_HEADS = 8
A_KV_HEADS = 2
A_GROUP = A_HEADS // A_KV_HEADS
WINDOW = 128
A_PREV_CHUNKS = WINDOW // CHUNK
A_CACHE_ROWS = WINDOW
B_HEADS = 8
B_PREV_CHUNKS = 8
B_CACHE_ROWS = B_PREV_CHUNKS * CHUNK
REL_CLIP = 128
REL_SIZE = REL_CLIP + CHUNK
A_QW = A_HEADS * HEAD_DIM
A_KVW = A_KV_HEADS * HEAD_DIM
B_W = B_HEADS * HEAD_DIM
ATT_PROJ = A_QW + 2 * A_KVW + 3 * B_W
ATT_W = A_QW + B_W
M_HEADS = 8
M_DK = 64
M_DV = 128
M_QKW = M_HEADS * M_DK
M_VW = M_HEADS * M_DV
MLSTM_PROJ = 2 * M_QKW + 2 * M_VW + 2 * M_HEADS
MEM_TOKENS = 256
MEM_HEADS = 4
MEM_HD = D_MODEL // MEM_HEADS
D_FF = 2816
CONV_W = 3
N_EVEN = (DEPTH + 1) // 2
N_ODD = DEPTH // 2
EPS = 1e-6

kernel_name = 'hybrid_streaming_encoder_step'


def _rms(x, g):
    xf = x.astype(jnp.float32)
    y = xf * lax.rsqrt(jnp.mean(xf * xf, axis=-1, keepdims=True) + EPS)
    return (y * g.astype(jnp.float32)).astype(x.dtype)


def _chunk_band(x, n_prev):
    b, t = x.shape[0], x.shape[1]
    nc = t // CHUNK
    xc = x.reshape((b, nc, CHUNK) + x.shape[2:])
    xp = jnp.pad(xc, [(0, 0), (n_prev, 0)] + [(0, 0)] * (xc.ndim - 2))
    return jnp.concatenate([xp[:, j:j + nc] for j in range(n_prev + 1)], axis=2)


def _band_valid(nc, n_prev):
    src_chunk = jnp.arange(nc)[:, None] - n_prev + (jnp.arange((n_prev + 1) * CHUNK) // CHUNK)[None, :]
    return src_chunk >= 0


def _alibi_bias(dist):
    slopes = jnp.asarray(2.0 ** (-8.0 * np.arange(1, A_HEADS + 1) / A_HEADS), dtype=jnp.float32)
    return -slopes.reshape(A_KV_HEADS, A_GROUP)[:, :, None, None] * jnp.abs(dist).astype(jnp.float32)[None, None]


def _relpos_bias(dist, table):
    idx = jnp.clip(dist, -(CHUNK - 1), REL_CLIP) + (CHUNK - 1)
    return table.astype(jnp.float32)[:, idx][:, None]


def _band_attn(q, k, v, bias, valid, sink):
    s = jnp.einsum('bnqhgd,bnshd->bnhgqs', q, k, preferred_element_type=jnp.float32) * (q.shape[-1] ** -0.5) + bias
    if valid is not None:
        s = jnp.where(valid[None, :, None, None, None, :], s, -jnp.inf)
    if sink is None:
        p = jax.nn.softmax(s, axis=-1)
    else:
        sk = sink.astype(jnp.float32)[None, None, :, :, None, None]
        mx = jnp.maximum(jnp.max(s, axis=-1, keepdims=True), sk)
        e = jnp.exp(s - mx)
        p = e / (jnp.sum(e, axis=-1, keepdims=True) + jnp.exp(sk - mx))
    return jnp.einsum('bnhgqs,bnshd->bnqhgd', p.astype(v.dtype), v)


def _att_proj(h, w_in):
    b, t, _ = h.shape
    cuts = [A_QW, A_QW + A_KVW, A_QW + 2 * A_KVW, A_QW + 2 * A_KVW + B_W, A_QW + 2 * A_KVW + 2 * B_W]
    qa, ka, va, qb, kb, vb = jnp.split(h @ w_in, cuts, axis=-1)
    return (qa.reshape(b, t, A_KV_HEADS, A_GROUP, HEAD_DIM), ka.reshape(b, t, A_KV_HEADS, HEAD_DIM),
            va.reshape(b, t, A_KV_HEADS, HEAD_DIM), qb.reshape(b, t, B_HEADS, 1, HEAD_DIM),
            kb.reshape(b, t, B_HEADS, HEAD_DIM), vb.reshape(b, t, B_HEADS, HEAD_DIM))


def _att_mixer_prompt(h, w_in, w_out, sink, rel_table):
    b, t, _ = h.shape
    nc = t // CHUNK
    qa, ka, va, qb, kb, vb = _att_proj(h, w_in)
    blk = lambda z: z.reshape((b, nc, CHUNK) + z.shape[2:])
    qpos = jnp.arange(CHUNK)[:, None]
    dist_a = qpos + A_PREV_CHUNKS * CHUNK - jnp.arange((A_PREV_CHUNKS + 1) * CHUNK)[None, :]
    oa = _band_attn(blk(qa), _chunk_band(ka, A_PREV_CHUNKS), _chunk_band(va, A_PREV_CHUNKS), _alibi_bias(dist_a),
                    _band_valid(nc, A_PREV_CHUNKS), sink.reshape(A_KV_HEADS, A_GROUP))
    dist_b = qpos + B_PREV_CHUNKS * CHUNK - jnp.arange((B_PREV_CHUNKS + 1) * CHUNK)[None, :]
    ob = _band_attn(blk(qb), _chunk_band(kb, B_PREV_CHUNKS), _chunk_band(vb, B_PREV_CHUNKS),
                    _relpos_bias(dist_b, rel_table), _band_valid(nc, B_PREV_CHUNKS), None)
    y = jnp.concatenate([oa.reshape(b, t, A_QW), ob.reshape(b, t, B_W)], axis=-1) @ w_out
    return y, (ka[:, -A_CACHE_ROWS:], va[:, -A_CACHE_ROWS:], kb[:, -B_CACHE_ROWS:], vb[:, -B_CACHE_ROWS:])


def _att_mixer_sample(h, ca_k, ca_v, cb_k, cb_v, w_in, w_out, sink, rel_table):
    b, s, _ = h.shape
    qa, ka, va, qb, kb, vb = _att_proj(h, w_in)
    ka_all = jnp.concatenate([ca_k.astype(ka.dtype), ka], axis=1)
    va_all = jnp.concatenate([ca_v.astype(va.dtype), va], axis=1)
    kb_all = jnp.concatenate([cb_k.astype(kb.dtype), kb], axis=1)
    vb_all = jnp.concatenate([cb_v.astype(vb.dtype), vb], axis=1)
    pa, pb = ca_k.shape[1], cb_k.shape[1]
    qpos = jnp.arange(s)[:, None]
    dist_a = qpos + pa - jnp.arange(pa + s)[None, :]
    oa = _band_attn(qa[:, None], ka_all[:, None], va_all[:, None], _alibi_bias(dist_a), None,
                    sink.reshape(A_KV_HEADS, A_GROUP))
    dist_b = qpos + pb - jnp.arange(pb + s)[None, :]
    ob = _band_attn(qb[:, None], kb_all[:, None], vb_all[:, None], _relpos_bias(dist_b, rel_table), None, None)
    y = jnp.concatenate([oa.reshape(b, s, A_QW), ob.reshape(b, s, B_W)], axis=-1) @ w_out
    return y, (ka, va, kb, vb)


def _mlstm_proj(h, w_in, b_i, b_f):
    b, t, _ = h.shape
    cuts = [M_QKW, 2 * M_QKW, 2 * M_QKW + M_VW, 2 * M_QKW + 2 * M_VW, 2 * M_QKW + 2 * M_VW + M_HEADS]
    q, k, v, o, ig, fg = jnp.split(h @ w_in, cuts, axis=-1)
    f32 = jnp.float32
    q = q.reshape(b, t, M_HEADS, M_DK).astype(f32)
    k = k.reshape(b, t, M_HEADS, M_DK).astype(f32) * (M_DK ** -0.5)
    v = v.reshape(b, t, M_HEADS, M_DV).astype(f32)
    ig = ig.astype(f32) + b_i.astype(f32)
    lf = jax.nn.log_sigmoid(fg.astype(f32) + b_f.astype(f32))
    return q, k, v, o, ig, lf


def _mlstm_block(carry, xs):
    c0, n0, m0 = carry
    q, k, v, ig, lf = xs
    L = q.shape[2]
    bcum = jnp.cumsum(lf, axis=-1)
    causal = jnp.tril(jnp.ones((L, L), dtype=bool))
    dmat = jnp.where(causal, bcum[..., :, None] - bcum[..., None, :] + ig[..., None, :], -jnp.inf)
    inter = bcum + m0[..., None]
    m_t = jnp.maximum(inter, jnp.max(dmat, axis=-1))
    w = jnp.exp(dmat - m_t[..., None])
    sc = jnp.einsum('bhld,bhsd->bhls', q, k) * w
    g_inter = jnp.exp(inter - m_t)
    num = jnp.einsum('bhls,bhsv->bhlv', sc, v) + g_inter[..., None] * jnp.einsum('bhld,bhdv->bhlv', q, c0)
    den = jnp.sum(sc, axis=-1) + g_inter * jnp.einsum('bhld,bhd->bhl', q, n0)
    h = num / jnp.maximum(jnp.abs(den), jnp.exp(-m_t))[..., None]
    b_last = bcum[..., -1]
    g = b_last[..., None] - bcum + ig
    m_new = jnp.maximum(b_last + m0, jnp.max(g, axis=-1))
    decay = jnp.exp(b_last + m0 - m_new)
    wk = jnp.exp(g - m_new[..., None])[..., None] * k
    c_new = decay[..., None, None] * c0 + jnp.einsum('bhsd,bhsv->bhdv', wk, v)
    n_new = decay[..., None] * n0 + jnp.sum(wk, axis=2)
    return (c_new, n_new, m_new), h


def _mlstm_prompt(h, w_in, b_i, b_f, w_out):
    b, t, _ = h.shape
    nc = t // CHUNK
    q, k, v, o, ig, lf = _mlstm_proj(h, w_in, b_i, b_f)

    def blocks(z):
        z = z.reshape((b, nc, CHUNK) + z.shape[2:])
        return jnp.swapaxes(jnp.moveaxis(z, 1, 0), 2, 3)

    init = (jnp.zeros((b, M_HEADS, M_DK, M_DV), jnp.float32), jnp.zeros((b, M_HEADS, M_DK), jnp.float32),
            jnp.zeros((b, M_HEADS), jnp.float32))
    state, hs = lax.scan(_mlstm_block, init, (blocks(q), blocks(k), blocks(v), blocks(ig), blocks(lf)))
    hs = jnp.moveaxis(jnp.swapaxes(hs, 2, 3), 0, 1).reshape(b, t, M_VW)
    y = (hs.astype(h.dtype) * jax.nn.sigmoid(o)) @ w_out
    return y, state


def _mlstm_sample(h, c0, n0, m0, w_in, b_i, b_f, w_out):
    b, s, _ = h.shape
    q, k, v, o, ig, lf = _mlstm_proj(h, w_in, b_i, b_f)
    hm = lambda z: jnp.swapaxes(z, 1, 2)
    f32 = jnp.float32
    state, hs = _mlstm_block((c0.astype(f32), n0.astype(f32), m0.astype(f32)), (hm(q), hm(k), hm(v), hm(ig), hm(lf)))
    hs = jnp.swapaxes(hs, 1, 2).reshape(b, s, M_VW)
    y = (hs.astype(h.dtype) * jax.nn.sigmoid(o)) @ w_out
    return y, state


def _mem_kv(mem, w_k, w_v):
    b, m, _ = mem.shape
    return (mem @ w_k).reshape(b, m, MEM_HEADS, MEM_HD), (mem @ w_v).reshape(b, m, MEM_HEADS, MEM_HD)


def _cross_attn(h, mk, mv, w_q, w_o):
    b, t, _ = h.shape
    q = (h @ w_q).reshape(b, t, MEM_HEADS, MEM_HD)
    s = jnp.einsum('bthd,bshd->bhts', q, mk.astype(q.dtype), preferred_element_type=jnp.float32) * (MEM_HD ** -0.5)
    p = jax.nn.softmax(s, axis=-1).astype(h.dtype)
    return jnp.einsum('bhts,bshd->bthd', p, mv.astype(h.dtype)).reshape(b, t, D_MODEL) @ w_o


def _conv_ffn(h, hist, w_in, conv_w, w_out):
    t = h.shape[1]
    u = h @ w_in
    full = jnp.concatenate([hist.astype(u.dtype), u], axis=1)
    c = full[:, 0:t] * conv_w[0]
    for j in range(1, CONV_W):
        c = c + full[:, j:j + t] * conv_w[j]
    a, g = jnp.split(c, 2, axis=-1)
    y = (jax.nn.silu(a) * g) @ w_out
    return y, full[:, -(CONV_W - 1):]


def setup_inputs(seed: int = 0) -> dict:
    key = jax.random.key(seed)
    ks = jax.random.split(key, 40)
    nrm = lambda i, shape, scale: jax.random.normal(ks[i], shape, jnp.float32) * scale
    a_rows = min(A_CACHE_ROWS, PAST_LEN)
    b_rows = min(B_CACHE_ROWS, PAST_LEN)
    return {
        'x_prompt': nrm(0, (BATCH, SEQ, D_MODEL), 1.0),
        'x_sample': nrm(1, (DEC_BATCH, DEC_SEQ, D_MODEL), 1.0),
        'mem_prompt': nrm(2, (BATCH, MEM_TOKENS, D_MODEL), 1.0),
        'cache_a_k': nrm(3, (N_EVEN, DEC_BATCH, a_rows, A_KV_HEADS, HEAD_DIM), 1.0),
        'cache_a_v': nrm(4, (N_EVEN, DEC_BATCH, a_rows, A_KV_HEADS, HEAD_DIM), 1.0),
        'cache_b_k': nrm(5, (N_EVEN, DEC_BATCH, b_rows, B_HEADS, HEAD_DIM), 1.0),
        'cache_b_v': nrm(6, (N_EVEN, DEC_BATCH, b_rows, B_HEADS, HEAD_DIM), 1.0),
        'state_mlstm_c': nrm(7, (N_ODD, DEC_BATCH, M_HEADS, M_DK, M_DV), 0.1),
        'state_mlstm_n': nrm(8, (N_ODD, DEC_BATCH, M_HEADS, M_DK), 0.1),
        'state_mlstm_m': nrm(9, (N_ODD, DEC_BATCH, M_HEADS), 0.1),
        'cache_mem_k': nrm(10, (DEPTH, DEC_BATCH, MEM_TOKENS, MEM_HEADS, MEM_HD), 1.0),
        'cache_mem_v': nrm(11, (DEPTH, DEC_BATCH, MEM_TOKENS, MEM_HEADS, MEM_HD), 1.0),
        'state_ffn_conv': nrm(12, (DEPTH, DEC_BATCH, CONV_W - 1, 2 * D_FF), 1.0),
        'norm_mix': 1.0 + nrm(13, (DEPTH, D_MODEL), 0.05),
        'norm_cross': 1.0 + nrm(14, (DEPTH, D_MODEL), 0.05),
        'norm_ffn': 1.0 + nrm(15, (DEPTH, D_MODEL), 0.05),
        'norm_final': 1.0 + nrm(16, (D_MODEL,), 0.05),
        'w_in_att': nrm(17, (N_EVEN, D_MODEL, ATT_PROJ), D_MODEL ** -0.5),
        'w_out_att': nrm(18, (N_EVEN, ATT_W, D_MODEL), 0.5 * ATT_W ** -0.5),
        'sink_a': nrm(19, (N_EVEN, A_HEADS), 0.5),
        'relpos_b': nrm(20, (N_EVEN, B_HEADS, REL_SIZE), 0.5),
        'w_in_mlstm': nrm(21, (N_ODD, D_MODEL, MLSTM_PROJ), D_MODEL ** -0.5),
        'b_igate': nrm(22, (N_ODD, M_HEADS), 0.1),
        'b_fgate': 3.0 + nrm(23, (N_ODD, M_HEADS), 0.1),
        'w_out_mlstm': nrm(24, (N_ODD, M_VW, D_MODEL), 0.5 * M_VW ** -0.5),
        'w_mem_q': nrm(25, (DEPTH, D_MODEL, D_MODEL), D_MODEL ** -0.5),
        'w_mem_k': nrm(26, (DEPTH, D_MODEL, D_MODEL), D_MODEL ** -0.5),
        'w_mem_v': nrm(27, (DEPTH, D_MODEL, D_MODEL), D_MODEL ** -0.5),
        'w_mem_o': nrm(28, (DEPTH, D_MODEL, D_MODEL), 0.5 * D_MODEL ** -0.5),
        'w_ffn_in': nrm(29, (DEPTH, D_MODEL, 2 * D_FF), D_MODEL ** -0.5),
        'conv_ffn': nrm(30, (DEPTH, CONV_W, 2 * D_FF), CONV_W ** -0.5),
        'w_ffn_out': nrm(31, (DEPTH, D_FF, D_MODEL), 0.5 * D_FF ** -0.5),
    }


def reference(x_prompt, x_sample, mem_prompt, cache_a_k, cache_a_v, cache_b_k, cache_b_v, state_mlstm_c,
              state_mlstm_n, state_mlstm_m, cache_mem_k, cache_mem_v, state_ffn_conv, norm_mix, norm_cross,
              norm_ffn, norm_final, w_in_att, w_out_att, sink_a, relpos_b, w_in_mlstm, b_igate, b_fgate,
              w_out_mlstm, w_mem_q, w_mem_k, w_mem_v, w_mem_o, w_ffn_in, conv_ffn, w_ffn_out):
    xp, xs = x_prompt, x_sample
    p_ak, p_av, p_bk, p_bv, p_c, p_n, p_m, p_mk, p_mv, p_conv = [], [], [], [], [], [], [], [], [], []
    s_ak, s_av, s_bk, s_bv, s_c, s_n, s_m, s_conv = [], [], [], [], [], [], [], []
    for l in range(DEPTH):
        hp = _rms(xp, norm_mix[l])
        hs = _rms(xs, norm_mix[l])
        if l % 2 == 0:
            e = l // 2
            yp, (ak, av, bk, bv) = _att_mixer_prompt(hp, w_in_att[e], w_out_att[e], sink_a[e], relpos_b[e])
            p_ak.append(ak); p_av.append(av); p_bk.append(bk); p_bv.append(bv)
            ys, (ak, av, bk, bv) = _att_mixer_sample(hs, cache_a_k[e], cache_a_v[e], cache_b_k[e], cache_b_v[e],
                                                     w_in_att[e], w_out_att[e], sink_a[e], relpos_b[e])
            s_ak.append(ak); s_av.append(av); s_bk.append(bk); s_bv.append(bv)
        else:
            o = l // 2
            yp, (c, n, m) = _mlstm_prompt(hp, w_in_mlstm[o], b_igate[o], b_fgate[o], w_out_mlstm[o])
            p_c.append(c); p_n.append(n); p_m.append(m)
            ys, (c, n, m) = _mlstm_sample(hs, state_mlstm_c[o], state_mlstm_n[o], state_mlstm_m[o],
                                          w_in_mlstm[o], b_igate[o], b_fgate[o], w_out_mlstm[o])
            s_c.append(c); s_n.append(n); s_m.append(m)
        xp = xp + yp
        xs = xs + ys
        mk, mv = _mem_kv(mem_prompt, w_mem_k[l], w_mem_v[l])
        p_mk.append(mk); p_mv.append(mv)
        xp = xp + _cross_attn(_rms(xp, norm_cross[l]), mk, mv, w_mem_q[l], w_mem_o[l])
        xs = xs + _cross_attn(_rms(xs, norm_cross[l]), cache_mem_k[l], cache_mem_v[l], w_mem_q[l], w_mem_o[l])
        hp = _rms(xp, norm_ffn[l])
        hist0 = jnp.zeros((hp.shape[0], CONV_W - 1, 2 * D_FF), hp.dtype)
        yp, cv = _conv_ffn(hp, hist0, w_ffn_in[l], conv_ffn[l], w_ffn_out[l])
        p_conv.append(cv)
        ys, cv = _conv_ffn(_rms(xs, norm_ffn[l]), state_ffn_conv[l], w_ffn_in[l], conv_ffn[l], w_ffn_out[l])
        s_conv.append(cv)
        xp = xp + yp
        xs = xs + ys
    y_prompt = _rms(xp, norm_final)
    y_sample = _rms(xs, norm_final)
    return (y_prompt, y_sample,
            jnp.stack(p_ak), jnp.stack(p_av), jnp.stack(p_bk), jnp.stack(p_bv),
            jnp.stack(p_c), jnp.stack(p_n), jnp.stack(p_m), jnp.stack(p_mk), jnp.stack(p_mv), jnp.stack(p_conv),
            jnp.stack(s_ak), jnp.stack(s_av), jnp.stack(s_bk), jnp.stack(s_bv),
            jnp.stack(s_c), jnp.stack(s_n), jnp.stack(s_m), jnp.stack(s_conv))
```

```python
import functools

import numpy as np
import jax
import jax.numpy as jnp
from jax import lax
from jax.experimental import pallas as pl
from jax.experimental.pallas import tpu as pltpu

F32 = jnp.float32
BF16 = jnp.bfloat16

D_MODEL = 1024
CHUNK = 64
HEAD_DIM = 64
A_HEADS = 8
A_KV_HEADS = 2
A_GROUP = A_HEADS // A_KV_HEADS
A_BAND = 3 * CHUNK
A_PREV = 2 * CHUNK
B_HEADS = 8
B_BAND = 9 * CHUNK
B_PREV = 8 * CHUNK
REL_CLIP = 128
A_QW = A_HEADS * HEAD_DIM
A_KVW = A_KV_HEADS * HEAD_DIM
B_W = B_HEADS * HEAD_DIM
ATT_PROJ = A_QW + 2 * A_KVW + 3 * B_W
KV32_W = 2 * A_KVW + 2 * B_W
M_HEADS = 8
M_DK = 64
M_DV = 128
M_QKW = M_HEADS * M_DK
M_VW = M_HEADS * M_DV
M_MAIN = 2 * M_QKW + 2 * M_VW
MEM_TOKENS = 256
MEM_HEADS = 4
MEM_HD = D_MODEL // MEM_HEADS
D_FF = 2816
FF_CHUNK = 256
EPS = 1e-6
NEG_INF = float("-inf")

VMEM_LIMIT_BYTES = 56 * 1024 * 1024
PROMPT_TILE = 512


def _params(n_axes):
    return pltpu.CompilerParams(dimension_semantics=("arbitrary",) * n_axes,
                                vmem_limit_bytes=VMEM_LIMIT_BYTES)


def _const_spec(shape):
    zeros = (0,) * len(shape)
    return pl.BlockSpec(shape, lambda *_: zeros)


def _rms(x, g):
    return x * lax.rsqrt(jnp.mean(x * x, axis=-1, keepdims=True) + EPS) * g


def _dot(a, b):
    return jnp.dot(a, b, preferred_element_type=F32)


def _dot_nt(a, b):
    return lax.dot_general(a, b, (((1,), (1,)), ((), ())), preferred_element_type=F32)


def _dot_tn(a, b):
    return lax.dot_general(a, b, (((0,), (0,)), ((), ())), preferred_element_type=F32)


def _proj_kernel(x_ref, g_ref, w_ref, *out_refs, plan):
    h = _rms(x_ref[...], g_ref[...]).astype(BF16)
    for c0, cw, dsts in plan:
        r = _dot(h, w_ref[:, c0:c0 + cw])
        for oi, oc in dsts:
            out_refs[oi][:, oc:oc + cw] = r.astype(out_refs[oi].dtype)


def _proj_call(x, g, w, plan, out_shapes, out_specs, tm, name):
    t = x.shape[0]
    return pl.pallas_call(
        functools.partial(_proj_kernel, plan=plan),
        grid=(t // tm,),
        in_specs=[pl.BlockSpec((tm, D_MODEL), lambda i: (i, 0)),
                  _const_spec((1, D_MODEL)),
                  _const_spec(w.shape)],
        out_specs=out_specs,
        out_shape=out_shapes,
        compiler_params=_params(1),
        name=name,
    )(x, g.reshape(1, D_MODEL), w)


_ATT_PLAN = (
    (0, 512, ((0, 0),)),
    (512, 512, ((0, 512),)),
    (1024, 512, ((0, 1024), (1, 256))),
    (1536, 512, ((0, 1536), (1, 768))),
    (2048, 256, ((0, 2048), (1, 0))),
)


def _att_proj(x, g, w_perm, tm, kv_all):
    t = x.shape[0]
    if kv_all:
        kv_shape, kv_spec = (t, KV32_W), pl.BlockSpec((tm, KV32_W), lambda i: (i, 0))
    else:
        kv_shape, kv_spec = (tm, KV32_W), _const_spec((tm, KV32_W))
    return _proj_call(
        x, g, w_perm, _ATT_PLAN,
        (jax.ShapeDtypeStruct((t, ATT_PROJ), BF16), jax.ShapeDtypeStruct(kv_shape, F32)),
        (pl.BlockSpec((tm, ATT_PROJ), lambda i: (i, 0)), kv_spec),
        tm, "att_proj")


_MLSTM_PLAN = tuple((c, 512, ((0, c),)) for c in range(0, M_MAIN, 512)) + ((M_MAIN, 128, ((1, 0),)),)


def _mlstm_proj(x, g, w_pad, tm):
    t = x.shape[0]
    return _proj_call(
        x, g, w_pad, _MLSTM_PLAN,
        (jax.ShapeDtypeStruct((t, M_MAIN), BF16), jax.ShapeDtypeStruct((t, 128), F32)),
        (pl.BlockSpec((tm, M_MAIN), lambda i: (i, 0)), pl.BlockSpec((tm, 128), lambda i: (i, 0))),
        tm, "mlstm_proj")


def _memkv_kernel(m_ref, w_ref, o_ref):
    o_ref[0] = _dot(m_ref[...].astype(BF16), w_ref[0])


def _memkv_call(mem, w):
    n = w.shape[0]
    return pl.pallas_call(
        _memkv_kernel,
        grid=(n,),
        in_specs=[_const_spec(mem.shape), pl.BlockSpec((1, D_MODEL, D_MODEL), lambda i: (i, 0, 0))],
        out_specs=pl.BlockSpec((1, MEM_TOKENS, D_MODEL), lambda i: (i, 0, 0)),
        out_shape=jax.ShapeDtypeStruct((n, MEM_TOKENS, D_MODEL), F32),
        compiler_params=_params(1),
        name="mem_kv",
    )(mem, w)


def _softmax_pv(s, v, sink, n_invalid):
    if n_invalid:
        col = lax.broadcasted_iota(jnp.int32, s.shape, 1)
        s = jnp.where(col < n_invalid, NEG_INF, s)
    mx = jnp.max(s, axis=-1, keepdims=True)
    if sink is not None:
        mx = jnp.maximum(mx, sink)
    e = jnp.exp(s - mx)
    den = jnp.sum(e, axis=-1, keepdims=True)
    if sink is not None:
        den = den + jnp.exp(sink - mx)
    return _dot(e.astype(BF16), v) / den


def _attn_chunk(qa, qb, ka, va, kb, vb, bias_a_ref, sink_ref, bias_b_ref, inv_a, inv_b):
    scale = HEAD_DIM ** -0.5
    outs = []
    for g in range(A_KV_HEADS):
        heads = range(g * A_GROUP, (g + 1) * A_GROUP)
        qs = jnp.concatenate([qa[:, h * HEAD_DIM:(h + 1) * HEAD_DIM] for h in heads], axis=0)
        s = _dot_nt(qs, ka[:, g * HEAD_DIM:(g + 1) * HEAD_DIM]) * scale + bias_a_ref[g]
        o = _softmax_pv(s, va[:, g * HEAD_DIM:(g + 1) * HEAD_DIM], sink_ref[g], inv_a)
        outs += [o[j * CHUNK:(j + 1) * CHUNK] for j in range(A_GROUP)]
    for h in range(B_HEADS):
        cols = slice(h * HEAD_DIM, (h + 1) * HEAD_DIM)
        s = _dot_nt(qb[:, cols], kb[:, cols]) * scale + bias_b_ref[h]
        outs.append(_softmax_pv(s, vb[:, cols], None, inv_b))
    return jnp.concatenate(outs, axis=1).astype(BF16)


def _attn_prompt_kernel(qa_ref, qb_ref, kb_ref, vb_ref, kbp_ref, vbp_ref, kva_ref, kvap_ref,
                        bias_a_ref, sink_ref, bias_b_ref, o_ref, kb_s, vb_s, kva_s, *, chunks):
    i = pl.program_id(0)
    tm = chunks * CHUNK
    kb_s[0:B_PREV] = kbp_ref[...]
    kb_s[B_PREV:B_PREV + tm] = kb_ref[...]
    vb_s[0:B_PREV] = vbp_ref[...]
    vb_s[B_PREV:B_PREV + tm] = vb_ref[...]
    kva_s[0:A_PREV] = kvap_ref[...]
    kva_s[A_PREV:A_PREV + tm] = kva_ref[...]

    def chunk(r0, inv_a, inv_b):
        return _attn_chunk(
            qa_ref[pl.ds(r0, CHUNK), :], qb_ref[pl.ds(r0, CHUNK), :],
            kva_s[pl.ds(r0, A_BAND), 0:A_KVW], kva_s[pl.ds(r0, A_BAND), A_KVW:2 * A_KVW],
            kb_s[pl.ds(r0, B_BAND), :], vb_s[pl.ds(r0, B_BAND), :],
            bias_a_ref, sink_ref, bias_b_ref, inv_a, inv_b)

    @pl.when(i == 0)
    def _():
        for jj in range(chunks):
            r0 = jj * CHUNK
            o_ref[r0:r0 + CHUNK, :] = chunk(r0, max(A_PREV - r0, 0), max(B_PREV - r0, 0))

    @pl.when(i > 0)
    def _():
        def body(jj, carry):
            r0 = pl.multiple_of(jj * CHUNK, CHUNK)
            o_ref[pl.ds(r0, CHUNK), :] = chunk(r0, 0, 0)
            return carry
        lax.fori_loop(0, chunks, body, 0)


def _attn_prompt_call(proj, bias_a, sink, bias_b, tm):
    t = proj.shape[0]
    chunks = tm // CHUNK
    assert tm == B_PREV and tm % A_PREV == 0
    ra = tm // A_PREV
    prev = lambda i: jnp.maximum(i - 1, 0)
    in_specs = [
        pl.BlockSpec((tm, 512), lambda i: (i, 0)),
        pl.BlockSpec((tm, 512), lambda i: (i, 1)),
        pl.BlockSpec((tm, 512), lambda i: (i, 2)),
        pl.BlockSpec((tm, 512), lambda i: (i, 3)),
        pl.BlockSpec((tm, 512), lambda i: (prev(i), 2)),
        pl.BlockSpec((tm, 512), lambda i: (prev(i), 3)),
        pl.BlockSpec((tm, 256), lambda i: (i, 8)),
        pl.BlockSpec((A_PREV, 256), lambda i: (jnp.maximum(i * ra - 1, 0), 8)),
        _const_spec(bias_a.shape), _const_spec(sink.shape), _const_spec(bias_b.shape),
    ]
    return pl.pallas_call(
        functools.partial(_attn_prompt_kernel, chunks=chunks),
        grid=(t // tm,),
        in_specs=in_specs,
        out_specs=pl.BlockSpec((tm, D_MODEL), lambda i: (i, 0)),
        out_shape=jax.ShapeDtypeStruct((t, D_MODEL), BF16),
        scratch_shapes=[pltpu.VMEM((B_PREV + tm, B_W), BF16), pltpu.VMEM((B_PREV + tm, B_W), BF16),
                        pltpu.VMEM((A_PREV + tm, 2 * A_KVW), BF16)],
        compiler_params=_params(1),
        name="attn_prompt",
    )(proj, proj, proj, proj, proj, proj, proj, proj, bias_a, sink, bias_b)


def _attn_sample_kernel(qa_ref, qb_ref, kb_ref, vb_ref, kva_ref, cak_ref, cav_ref, cbk_ref, cbv_ref,
                        bias_a_ref, sink_ref, bias_b_ref, o_ref, *, nb):
    for b in range(nb):
        rows = slice(b * CHUNK, (b + 1) * CHUNK)
        ka = jnp.concatenate([cak_ref[b].astype(BF16), kva_ref[rows, 0:A_KVW]], axis=0)
        va = jnp.concatenate([cav_ref[b].astype(BF16), kva_ref[rows, A_KVW:2 * A_KVW]], axis=0)
        kb = jnp.concatenate([cbk_ref[b].astype(BF16), kb_ref[rows, :]], axis=0)
        vb = jnp.concatenate([cbv_ref[b].astype(BF16), vb_ref[rows, :]], axis=0)
        o_ref[rows, :] = _attn_chunk(qa_ref[rows, :], qb_ref[rows, :], ka, va, kb, vb,
                                     bias_a_ref, sink_ref, bias_b_ref, 0, 0)


def _attn_sample_call(proj, ca_k, ca_v, cb_k, cb_v, bias_a, sink, bias_b, nb):
    t = proj.shape[0]
    nbatch = t // CHUNK
    tm = nb * CHUNK
    in_specs = [
        pl.BlockSpec((tm, 512), lambda i: (i, 0)),
        pl.BlockSpec((tm, 512), lambda i: (i, 1)),
        pl.BlockSpec((tm, 512), lambda i: (i, 2)),
        pl.BlockSpec((tm, 512), lambda i: (i, 3)),
        pl.BlockSpec((tm, 256), lambda i: (i, 8)),
        pl.BlockSpec((nb, A_PREV, A_KVW), lambda i: (i, 0, 0)),
        pl.BlockSpec((nb, A_PREV, A_KVW), lambda i: (i, 0, 0)),
        pl.BlockSpec((nb, B_PREV, B_W), lambda i: (i, 0, 0)),
        pl.BlockSpec((nb, B_PREV, B_W), lambda i: (i, 0, 0)),
        _const_spec(bias_a.shape), _const_spec(sink.shape), _const_spec(bias_b.shape),
    ]
    return pl.pallas_call(
        functools.partial(_attn_sample_kernel, nb=nb),
        grid=(nbatch // nb,),
        in_specs=in_specs,
        out_specs=pl.BlockSpec((tm, D_MODEL), lambda i: (i, 0)),
        out_shape=jax.ShapeDtypeStruct((t, D_MODEL), BF16),
        compiler_params=_params(1),
        name="attn_sample",
    )(proj, proj, proj, proj, proj, ca_k, ca_v, cb_k, cb_v, bias_a, sink, bias_b)


def _mlstm_chunk(qk, v, o, gates, c_s, n_s, m_s, tril, causal):
    lane = lax.broadcasted_iota(jnp.int32, gates.shape, 1)
    lf = jnp.minimum(gates, 0.0) - jnp.log(1.0 + jnp.exp(-jnp.abs(gates)))
    lf = jnp.where((lane >= M_HEADS) & (lane < 2 * M_HEADS), lf, 0.0)
    hi = lf.astype(BF16)
    r1 = lf - hi.astype(F32)
    mid = r1.astype(BF16)
    lo = (r1 - mid.astype(F32)).astype(BF16)
    bcum = _dot(tril, hi) + _dot(tril, mid) + _dot(tril, lo)
    comb = jnp.where(lane < M_HEADS, gates, bcum)
    comb_t = comb.T
    k_scale = M_DK ** -0.5
    outs = []
    for h in range(M_HEADS):
        q = qk[:, h * M_DK:(h + 1) * M_DK]
        k = qk[:, M_QKW + h * M_DK:M_QKW + (h + 1) * M_DK]
        vh = v[:, h * M_DV:(h + 1) * M_DV]
        ig_col = comb[:, h:h + 1]
        b_col = comb[:, M_HEADS + h:M_HEADS + h + 1]
        ig_row = comb_t[h:h + 1, :]
        b_row = comb_t[M_HEADS + h:M_HEADS + h + 1, :]
        m0 = m_s[h:h + 1, 0:1]
        c0 = c_s[h]
        n0 = n_s[h:h + 1, :]
        dmat = jnp.where(causal, b_col - b_row + ig_row, NEG_INF)
        inter = b_col + m0
        m_t = jnp.maximum(inter, jnp.max(dmat, axis=-1, keepdims=True))
        w = jnp.exp(dmat - m_t)
        sc = _dot_nt(q, k) * k_scale * w
        g_inter = jnp.exp(inter - m_t)
        num = _dot(sc.astype(BF16), vh) + g_inter * _dot(q, c0.astype(BF16))
        den = jnp.sum(sc, axis=-1, keepdims=True) + g_inter * jnp.sum(q.astype(F32) * n0, axis=-1, keepdims=True)
        outs.append(num / jnp.maximum(jnp.abs(den), jnp.exp(-m_t)))
        b_last = b_col[CHUNK - 1:CHUNK, :]
        g_col = b_last - b_col + ig_col
        m_new = jnp.maximum(b_last + m0, jnp.max(g_col, axis=0, keepdims=True))
        decay = jnp.exp(b_last + m0 - m_new)
        wk = jnp.exp(g_col - m_new) * (k.astype(F32) * k_scale)
        c_s[h] = decay * c0 + _dot_tn(wk.astype(BF16), vh)
        n_s[h:h + 1, :] = decay * n0 + jnp.sum(wk, axis=0, keepdims=True)
        m_s[h:h + 1, :] = jnp.broadcast_to(m_new, (1, 128))
    hs = jnp.concatenate(outs, axis=1)
    of = o.astype(F32)
    return (hs * (1.0 / (1.0 + jnp.exp(-of)))).astype(BF16)


def _mlstm_kernel(qk_ref, v_ref, o_ref, gates_ref, bias_ref, c_in, n_in, m_in,
                  hg_ref, c_out, n_out, m_out, c_s, n_s, m_s, *, chunks):
    j = pl.program_id(1)

    @pl.when(j == 0)
    def _():
        c_s[...] = c_in[0]
        n_s[...] = n_in[0]
        m_s[...] = m_in[0]

    row = lax.broadcasted_iota(jnp.int32, (CHUNK, CHUNK), 0)
    col = lax.broadcasted_iota(jnp.int32, (CHUNK, CHUNK), 1)
    causal = col <= row
    tril = jnp.where(causal, 1.0, 0.0).astype(BF16)
    for c in range(chunks):
        rows = slice(c * CHUNK, (c + 1) * CHUNK)
        hg_ref[rows, :] = _mlstm_chunk(qk_ref[rows, :], v_ref[rows, :], o_ref[rows, :],
                                       gates_ref[rows, :] + bias_ref[...], c_s, n_s, m_s, tril, causal)

    @pl.when(j == pl.num_programs(1) - 1)
    def _():
        c_out[0] = c_s[...]
        n_out[0] = n_s[...]
        m_out[0] = m_s[...]


def _mlstm_call(proj, gates, bias, c0, n0, m0, chunks_per_step):
    t = proj.shape[0]
    nseq = c0.shape[0]
    rows = chunks_per_step * CHUNK
    nsteps = t // (nseq * rows)
    blk = lambda s, j: s * nsteps + j
    in_specs = [
        pl.BlockSpec((rows, 1024), lambda s, j: (blk(s, j), 0)),
        pl.BlockSpec((rows, 1024), lambda s, j: (blk(s, j), 1)),
        pl.BlockSpec((rows, 1024), lambda s, j: (blk(s, j), 2)),
        pl.BlockSpec((rows, 128), lambda s, j: (blk(s, j), 0)),
        _const_spec((1, 128)),
        pl.BlockSpec((1, M_HEADS, M_DK, M_DV), lambda s, j: (s, 0, 0, 0)),
        pl.BlockSpec((1, M_HEADS, M_DK), lambda s, j: (s, 0, 0)),
        pl.BlockSpec((1, M_HEADS, 128), lambda s, j: (s, 0, 0)),
    ]
    out_specs = (
        pl.BlockSpec((rows, M_VW), lambda s, j: (blk(s, j), 0)),
        pl.BlockSpec((1, M_HEADS, M_DK, M_DV), lambda s, j: (s, 0, 0, 0)),
        pl.BlockSpec((1, M_HEADS, M_DK), lambda s, j: (s, 0, 0)),
        pl.BlockSpec((1, M_HEADS, 128), lambda s, j: (s, 0, 0)),
    )
    out_shape = (
        jax.ShapeDtypeStruct((t, M_VW), BF16),
        jax.ShapeDtypeStruct((nseq, M_HEADS, M_DK, M_DV), F32),
        jax.ShapeDtypeStruct((nseq, M_HEADS, M_DK), F32),
        jax.ShapeDtypeStruct((nseq, M_HEADS, 128), F32),
    )
    return pl.pallas_call(
        functools.partial(_mlstm_kernel, chunks=chunks_per_step),
        grid=(nseq, nsteps),
        in_specs=in_specs,
        out_specs=out_specs,
        out_shape=out_shape,
        scratch_shapes=[pltpu.VMEM((M_HEADS, M_DK, M_DV), F32), pltpu.VMEM((M_HEADS, M_DK), F32),
                        pltpu.VMEM((M_HEADS, 128), F32)],
        compiler_params=_params(2),
        name="mlstm",
    )(proj, proj, proj, gates, bias, c0, n0, m0)


def _post_kernel(x_ref, a_ref, wout_ref, g_ref, wq_ref, mk_ref, mv_ref, wo_ref, o_ref, att_s, *, nb, tb):
    x1 = x_ref[...] + _dot(a_ref[...], wout_ref[...])
    q = _dot(_rms(x1, g_ref[...]).astype(BF16), wq_ref[...]).astype(BF16)
    scale = MEM_HD ** -0.5
    for b in range(nb):
        rows = slice(b * tb, (b + 1) * tb)
        for h in range(MEM_HEADS):
            cols = slice(h * MEM_HD, (h + 1) * MEM_HD)
            s = _dot_nt(q[rows, cols], mk_ref[b, :, cols].astype(BF16)) * scale
            e = jnp.exp(s - jnp.max(s, axis=-1, keepdims=True))
            p = (e / jnp.sum(e, axis=-1, keepdims=True)).astype(BF16)
            att_s[rows, cols] = _dot(p, mv_ref[b, :, cols].astype(BF16)).astype(BF16)
    o_ref[...] = x1 + _dot(att_s[...], wo_ref[...])


def _post_call(x, a, w_out, g, w_q, mk, mv, w_o, tm, tb):
    t = x.shape[0]
    nb = tm // tb
    shared = mk.shape[0] == 1
    mem_map = (lambda i: (0, 0, 0)) if shared else (lambda i: (i, 0, 0))
    row = lambda i: (i, 0)
    in_specs = [
        pl.BlockSpec((tm, D_MODEL), row),
        pl.BlockSpec((tm, a.shape[1]), row),
        _const_spec(w_out.shape),
        _const_spec((1, D_MODEL)),
        _const_spec(w_q.shape),
        pl.BlockSpec((nb, MEM_TOKENS, D_MODEL), mem_map),
        pl.BlockSpec((nb, MEM_TOKENS, D_MODEL), mem_map),
        _const_spec(w_o.shape),
    ]
    return pl.pallas_call(
        functools.partial(_post_kernel, nb=nb, tb=tb),
        grid=(t // tm,),
        in_specs=in_specs,
        out_specs=pl.BlockSpec((tm, D_MODEL), row),
        out_shape=jax.ShapeDtypeStruct((t, D_MODEL), F32),
        scratch_shapes=[pltpu.VMEM((tm, D_MODEL), BF16)],
        compiler_params=_params(1),
        name="post_cross",
    )(x, a, w_out, g.reshape(1, D_MODEL), w_q, mk, mv, w_o)


def _conv3(u, cw, hists, seq_rows):
    w0, w1, w2 = cw[0:1], cw[1:2], cw[2:3]
    c = pltpu.roll(u, 2, 0) * w0 + pltpu.roll(u, 1, 0) * w1 + u * w2
    row = lax.broadcasted_iota(jnp.int32, (8, u.shape[1]), 0)
    pieces = []
    for s, hist in enumerate(hists):
        base = s * seq_rows
        t8 = u[base:base + 8]
        h0, h1 = hist[0:1], hist[1:2]
        u1 = jnp.where(row == 0, h1, pltpu.roll(t8, 1, 0))
        u2 = jnp.where(row == 0, h0, jnp.where(row == 1, h1, pltpu.roll(t8, 2, 0)))
        pieces += [u2 * w0 + u1 * w1 + t8 * w2, c[base + 8:base + seq_rows]]
    return jnp.concatenate(pieces, axis=0)


def _ffn_kernel(*refs, seq_rows, carry, final_norm):
    x_ref, g_ref, win_ref, cw_ref, wout_ref = refs[:5]
    pos = 5
    hist_ref = None
    if not carry:
        hist_ref = refs[pos]
        pos += 1
    gfin_ref = None
    if final_norm:
        gfin_ref = refs[pos]
        pos += 1
    o_ref, cs_ref, act_s = refs[pos:pos + 3]
    hist_s = refs[pos + 3] if carry else None

    tm = x_ref.shape[0]
    nseq = tm // seq_rows
    x = x_ref[...]
    h = _rms(x, g_ref[...]).astype(BF16)

    if carry:
        @pl.when(pl.program_id(0) == 0)
        def _():
            hist_s[...] = jnp.zeros_like(hist_s)

    for j in range(D_FF // FF_CHUNK):
        halves = []
        for half in range(2):
            c0 = half * D_FF + j * FF_CHUNK
            cols = slice(c0, c0 + FF_CHUNK)
            u = _dot(h, win_ref[:, cols])
            if carry:
                hists = [hist_s[:, cols]]
            else:
                hists = [hist_ref[s, :, cols] for s in range(nseq)]
            halves.append(_conv3(u, cw_ref[:, cols], hists, seq_rows))
            for s in range(nseq):
                last2 = u[(s + 1) * seq_rows - 2:(s + 1) * seq_rows]
                if carry:
                    hist_s[:, cols] = last2
                    cs_ref[:, cols] = last2
                else:
                    cs_ref[s, :, cols] = last2
        ca, cg = halves
        act = ca * (1.0 / (1.0 + jnp.exp(-ca))) * cg
        act_s[:, j * FF_CHUNK:(j + 1) * FF_CHUNK] = act.astype(BF16)

    y = x + _dot(act_s[...], wout_ref[...])
    if final_norm:
        y = _rms(y, gfin_ref[...])
    o_ref[...] = y


def _ffn_call(x, g, w_in, conv_w, w_out, hist, g_final, tm):
    t = x.shape[0]
    carry = hist is None
    seq_rows = tm if carry else CHUNK
    nseq = tm // seq_rows
    row = lambda i: (i, 0)
    args = [x, g.reshape(1, D_MODEL), w_in, conv_w, w_out]
    in_specs = [pl.BlockSpec((tm, D_MODEL), row), _const_spec((1, D_MODEL)),
                _const_spec(w_in.shape), _const_spec(conv_w.shape), _const_spec(w_out.shape)]
    if carry:
        cs_shape, cs_spec = (2, 2 * D_FF), _const_spec((2, 2 * D_FF))
    else:
        args.append(hist)
        in_specs.append(pl.BlockSpec((nseq, 2, 2 * D_FF), lambda i: (i, 0, 0)))
        cs_shape, cs_spec = hist.shape, pl.BlockSpec((nseq, 2, 2 * D_FF), lambda i: (i, 0, 0))
    if g_final is not None:
        args.append(g_final.reshape(1, D_MODEL))
        in_specs.append(_const_spec((1, D_MODEL)))
    scratch = [pltpu.VMEM((tm, D_FF), BF16)]
    if carry:
        scratch.append(pltpu.VMEM((2, 2 * D_FF), F32))
    return pl.pallas_call(
        functools.partial(_ffn_kernel, seq_rows=seq_rows, carry=carry, final_norm=g_final is not None),
        grid=(t // tm,),
        in_specs=in_specs,
        out_specs=(pl.BlockSpec((tm, D_MODEL), row), cs_spec),
        out_shape=(jax.ShapeDtypeStruct((t, D_MODEL), F32), jax.ShapeDtypeStruct(cs_shape, F32)),
        scratch_shapes=scratch,
        compiler_params=_params(1),
        name="conv_ffn",
    )(*args)


def _alibi_table():
    q = np.arange(CHUNK)[:, None]
    dist = np.abs(q + A_PREV - np.arange(A_BAND)[None, :]).astype(np.float32)
    slopes = (2.0 ** (-8.0 * np.arange(1, A_HEADS + 1) / A_HEADS)).astype(np.float32)
    bias = -slopes[:, None, None] * dist[None]
    return jnp.asarray(bias.reshape(A_KV_HEADS, A_GROUP * CHUNK, A_BAND))


def _relpos_index():
    q = np.arange(CHUNK)[:, None]
    dist = q + B_PREV - np.arange(B_BAND)[None, :]
    return np.clip(dist, -(CHUNK - 1), REL_CLIP) + (CHUNK - 1)


def _row_tile(t):
    return PROMPT_TILE if t % PROMPT_TILE == 0 else 256


def kernel(x_prompt, x_sample, mem_prompt, cache_a_k, cache_a_v, cache_b_k, cache_b_v, state_mlstm_c, state_mlstm_n, state_mlstm_m, cache_mem_k, cache_mem_v, state_ffn_conv, norm_mix, norm_cross, norm_ffn, norm_final, w_in_att, w_out_att, sink_a, relpos_b, w_in_mlstm, b_igate, b_fgate, w_out_mlstm, w_mem_q, w_mem_k, w_mem_v, w_mem_o, w_ffn_in, conv_ffn, w_ffn_out):
    depth = norm_mix.shape[0]
    assert x_prompt.shape[0] == 1
    tp = x_prompt.shape[1]
    nb, ds = x_sample.shape[0], x_sample.shape[1]
    assert ds == CHUNK and tp % PROMPT_TILE == 0
    ts = nb * ds
    tm_p, tm_s = PROMPT_TILE, _row_tile(ts)
    xp = x_prompt[0]
    xs = x_sample.reshape(ts, D_MODEL)

    memkv = _memkv_call(mem_prompt[0], jnp.concatenate([w_mem_k, w_mem_v], axis=0).astype(BF16))
    p_mk, p_mv = memkv[:depth], memkv[depth:]
    s_mk = cache_mem_k.reshape(depth, nb, MEM_TOKENS, D_MODEL)
    s_mv = cache_mem_v.reshape(depth, nb, MEM_TOKENS, D_MODEL)

    bias_a = _alibi_table()
    rel_idx = _relpos_index()
    post_tm_s = 256 if ts % 256 == 0 else ts
    attn_nb = 4 if nb % 4 == 0 else 1

    p_ak, p_av, p_bk, p_bv, p_c, p_n, p_m, p_conv = [], [], [], [], [], [], [], []
    s_ak, s_av, s_bk, s_bv, s_c, s_n, s_m, s_conv = [], [], [], [], [], [], [], []
    for l in range(depth):
        if l % 2 == 0:
            e = l // 2
            w = w_in_att[e]
            w_perm = jnp.concatenate([w[:, 0:512], w[:, 768:2304], w[:, 512:768]], axis=1).astype(BF16)
            proj_p, kv_p = _att_proj(xp, norm_mix[l], w_perm, tm_p, kv_all=False)
            proj_s, kv_s = _att_proj(xs, norm_mix[l], w_perm, tm_s, kv_all=True)
            sink = jnp.repeat(sink_a[e].astype(F32), CHUNK).reshape(A_KV_HEADS, A_GROUP * CHUNK, 1)
            bias_b = relpos_b[e].astype(F32)[:, rel_idx]
            mix_p = _attn_prompt_call(proj_p, bias_a, sink, bias_b, tm_p)
            mix_s = _attn_sample_call(
                proj_s,
                cache_a_k[e].reshape(nb, A_PREV, A_KVW), cache_a_v[e].reshape(nb, A_PREV, A_KVW),
                cache_b_k[e].reshape(nb, B_PREV, B_W), cache_b_v[e].reshape(nb, B_PREV, B_W),
                bias_a, sink, bias_b, attn_nb)
            w_mix_out = w_out_att[e].astype(BF16)
            p_ak.append(kv_p[tm_p - A_PREV:, 0:128].reshape(1, A_PREV, A_KV_HEADS, HEAD_DIM))
            p_av.append(kv_p[tm_p - A_PREV:, 128:256].reshape(1, A_PREV, A_KV_HEADS, HEAD_DIM))
            p_bk.append(kv_p[:, 256:768].reshape(1, B_PREV, B_HEADS, HEAD_DIM))
            p_bv.append(kv_p[:, 768:1280].reshape(1, B_PREV, B_HEADS, HEAD_DIM))
            s_ak.append(kv_s[:, 0:128].reshape(nb, ds, A_KV_HEADS, HEAD_DIM))
            s_av.append(kv_s[:, 128:256].reshape(nb, ds, A_KV_HEADS, HEAD_DIM))
            s_bk.append(kv_s[:, 256:768].reshape(nb, ds, B_HEADS, HEAD_DIM))
            s_bv.append(kv_s[:, 768:1280].reshape(nb, ds, B_HEADS, HEAD_DIM))
        else:
            o = l // 2
            w = w_in_mlstm[o]
            w_pad = jnp.concatenate([w, jnp.zeros((D_MODEL, 128 - 2 * M_HEADS), w.dtype)], axis=1).astype(BF16)
            gate_bias = jnp.concatenate([b_igate[o].astype(F32), b_fgate[o].astype(F32),
                                         jnp.zeros((128 - 2 * M_HEADS,), F32)]).reshape(1, 128)
            proj_p, gates_p = _mlstm_proj(xp, norm_mix[l], w_pad, tm_p)
            proj_s, gates_s = _mlstm_proj(xs, norm_mix[l], w_pad, tm_s)
            zc = jnp.zeros((1, M_HEADS, M_DK, M_DV), F32)
            zn = jnp.zeros((1, M_HEADS, M_DK), F32)
            zm = jnp.zeros((1, M_HEADS, 128), F32)
            mix_p, c, n, m = _mlstm_call(proj_p, gates_p, gate_bias, zc, zn, zm, 4)
            p_c.append(c); p_n.append(n); p_m.append(m[:, :, 0])
            m_in = jnp.broadcast_to(state_mlstm_m[o].astype(F32)[:, :, None], (nb, M_HEADS, 128))
            mix_s, c, n, m = _mlstm_call(proj_s, gates_s, gate_bias, state_mlstm_c[o].astype(F32),
                                         state_mlstm_n[o].astype(F32), m_in, 1)
            s_c.append(c); s_n.append(n); s_m.append(m[:, :, 0])
            w_mix_out = w_out_mlstm[o].astype(BF16)

        w_q, w_o = w_mem_q[l].astype(BF16), w_mem_o[l].astype(BF16)
        xp = _post_call(xp, mix_p, w_mix_out, norm_cross[l], w_q, p_mk[l][None], p_mv[l][None], w_o, tm_p, tm_p)
        xs = _post_call(xs, mix_s, w_mix_out, norm_cross[l], w_q, s_mk[l], s_mv[l], w_o, post_tm_s, CHUNK)

        w_fi, w_fo = w_ffn_in[l].astype(BF16), w_ffn_out[l].astype(BF16)
        g_fin = norm_final if l == depth - 1 else None
        xp, cv = _ffn_call(xp, norm_ffn[l], w_fi, conv_ffn[l], w_fo, None, g_fin, tm_p)
        p_conv.append(cv[None])
        xs, cv = _ffn_call(xs, norm_ffn[l], w_fi, conv_ffn[l], w_fo, state_ffn_conv[l], g_fin, tm_s)
        s_conv.append(cv)

    mem_shape = (depth, 1, MEM_TOKENS, MEM_HEADS, MEM_HD)
    return (xp[None], xs.reshape(nb, ds, D_MODEL),
            jnp.stack(p_ak), jnp.stack(p_av), jnp.stack(p_bk), jnp.stack(p_bv),
            jnp.stack(p_c), jnp.stack(p_n), jnp.stack(p_m),
            p_mk.reshape(mem_shape), p_mv.reshape(mem_shape), jnp.stack(p_conv),
            jnp.stack(s_ak), jnp.stack(s_av), jnp.stack(s_bk), jnp.stack(s_bv),
            jnp.stack(s_c), jnp.stack(s_n), jnp.stack(s_m), jnp.stack(s_conv))
```

```python
import functools

import numpy as np
import jax
import jax.numpy as jnp
from jax import lax
from jax.experimental import pallas as pl
from jax.experimental.pallas import tpu as pltpu

F32 = jnp.float32
BF16 = jnp.bfloat16

D_MODEL = 1024
CHUNK = 64
HEAD_DIM = 64
A_HEADS = 8
A_KV_HEADS = 2
A_GROUP = A_HEADS // A_KV_HEADS
A_BAND = 3 * CHUNK
A_PREV = 2 * CHUNK
B_HEADS = 8
B_BAND = 9 * CHUNK
B_PREV = 8 * CHUNK
REL_CLIP = 128
A_QW = A_HEADS * HEAD_DIM
A_KVW = A_KV_HEADS * HEAD_DIM
B_W = B_HEADS * HEAD_DIM
ATT_PROJ = A_QW + 2 * A_KVW + 3 * B_W
KV32_W = 2 * A_KVW + 2 * B_W
M_HEADS = 8
M_DK = 64
M_DV = 128
M_QKW = M_HEADS * M_DK
M_VW = M_HEADS * M_DV
M_MAIN = 2 * M_QKW + 2 * M_VW
MEM_TOKENS = 256
MEM_HEADS = 4
MEM_HD = D_MODEL // MEM_HEADS
D_FF = 2816
FF_CHUNK = 256
EPS = 1e-6
NEG_INF = float("-inf")

VMEM_LIMIT_BYTES = 56 * 1024 * 1024
PROMPT_TILE = 512


def _params(n_axes):
    return pltpu.CompilerParams(dimension_semantics=("arbitrary",) * n_axes,
                                vmem_limit_bytes=VMEM_LIMIT_BYTES)


def _const_spec(shape):
    zeros = (0,) * len(shape)
    return pl.BlockSpec(shape, lambda *_: zeros)


def _layer_spec(stack, layer):
    idx = (layer,) + (0,) * (stack.ndim - 1)
    return pl.BlockSpec((None,) + stack.shape[1:], lambda *_: idx)


def _rms(x, g):
    return x * lax.rsqrt(jnp.mean(x * x, axis=-1, keepdims=True) + EPS) * g


def _dot(a, b):
    return jnp.dot(a, b, preferred_element_type=F32)


def _dot_nt(a, b):
    return lax.dot_general(a, b, (((1,), (1,)), ((), ())), preferred_element_type=F32)


def _dot_tn(a, b):
    return lax.dot_general(a, b, (((0,), (0,)), ((), ())), preferred_element_type=F32)


def _proj_kernel(x_ref, g_ref, w_ref, *out_refs, plan, transposed):
    h = _rms(x_ref[...], g_ref[...]).astype(BF16)
    for c0, cw, dsts in plan:
        r = _dot(h, w_ref[:, c0:c0 + cw])
        for oi, oc in dsts:
            out_refs[oi][:, oc:oc + cw] = r.astype(out_refs[oi].dtype)
        if c0 in transposed:
            oi, width = transposed[c0]
            out_refs[oi][...] = r[:, :width].T.astype(out_refs[oi].dtype)


def _proj_call(x, g, gl, w, wl, plan, transposed, out_shapes, out_specs, tm, name):
    t = x.shape[0]
    return pl.pallas_call(
        functools.partial(_proj_kernel, plan=plan, transposed=transposed),
        grid=(t // tm,),
        in_specs=[pl.BlockSpec((tm, D_MODEL), lambda i: (i, 0)),
                  _layer_spec(g, gl),
                  _layer_spec(w, wl)],
        out_specs=out_specs,
        out_shape=out_shapes,
        compiler_params=_params(1),
        name=name,
    )(x, g, w)


_ATT_PLAN = (
    (0, 512, ((0, 0),)),
    (512, 512, ((0, 512),)),
    (1024, 512, ((0, 1024), (1, 256))),
    (1536, 512, ((0, 1536), (1, 768))),
    (2048, 256, ((0, 2048), (1, 0))),
)


def _att_proj(x, g, gl, w_perm, wl, tm, prompt):
    t = x.shape[0]
    shapes = [jax.ShapeDtypeStruct((t, ATT_PROJ), BF16)]
    specs = [pl.BlockSpec((tm, ATT_PROJ), lambda i: (i, 0))]
    transposed = {}
    if prompt:
        shapes += [jax.ShapeDtypeStruct((tm, KV32_W), F32), jax.ShapeDtypeStruct((B_W, t), BF16),
                   jax.ShapeDtypeStruct((A_KVW, t), BF16)]
        specs += [_const_spec((tm, KV32_W)), pl.BlockSpec((B_W, tm), lambda i: (0, i)),
                  pl.BlockSpec((A_KVW, tm), lambda i: (0, i))]
        transposed = {1024: (2, B_W), 2048: (3, A_KVW)}
    else:
        shapes.append(jax.ShapeDtypeStruct((t, KV32_W), F32))
        specs.append(pl.BlockSpec((tm, KV32_W), lambda i: (i, 0)))
    return _proj_call(x, g, gl, w_perm, wl, _ATT_PLAN, transposed, tuple(shapes), tuple(specs), tm, "att_proj")


_MLSTM_PLAN = tuple((c, 512, ((0, c),)) for c in range(0, M_MAIN, 512)) + ((M_MAIN, 128, ((1, 0),)),)


def _mlstm_proj(x, g, gl, w_pad, wl, tm):
    t = x.shape[0]
    return _proj_call(
        x, g, gl, w_pad, wl, _MLSTM_PLAN, {},
        (jax.ShapeDtypeStruct((t, M_MAIN), BF16), jax.ShapeDtypeStruct((t, 128), F32)),
        (pl.BlockSpec((tm, M_MAIN), lambda i: (i, 0)), pl.BlockSpec((tm, 128), lambda i: (i, 0))),
        tm, "mlstm_proj")


def _memkv_kernel(m_ref, w_ref, o_ref):
    o_ref[0] = _dot(m_ref[...].astype(BF16), w_ref[0])


def _memkv_call(mem, w):
    n = w.shape[0]
    return pl.pallas_call(
        _memkv_kernel,
        grid=(n,),
        in_specs=[_const_spec(mem.shape), pl.BlockSpec((1, D_MODEL, D_MODEL), lambda i: (i, 0, 0))],
        out_specs=pl.BlockSpec((1, MEM_TOKENS, D_MODEL), lambda i: (i, 0, 0)),
        out_shape=jax.ShapeDtypeStruct((n, MEM_TOKENS, D_MODEL), F32),
        compiler_params=_params(1),
        name="mem_kv",
    )(mem, w)


def _softmax_pv(s, v, sink):
    mx = jnp.max(s, axis=-1, keepdims=True)
    if sink is not None:
        mx = jnp.maximum(mx, sink)
    e = jnp.exp(s - mx)
    den = jnp.sum(e, axis=-1, keepdims=True)
    if sink is not None:
        den = den + jnp.exp(sink - mx)
    return _dot(e.astype(BF16), v) / den


def _attn_chunk(qa, qb, ka, va, kb, vb, bias_a_ref, sink_ref, bias_b_ref):
    outs = []
    for g in range(A_KV_HEADS):
        heads = range(g * A_GROUP, (g + 1) * A_GROUP)
        qs = jnp.concatenate([qa[:, h * HEAD_DIM:(h + 1) * HEAD_DIM] for h in heads], axis=0)
        s = _dot_nt(qs, ka[:, g * HEAD_DIM:(g + 1) * HEAD_DIM]) + bias_a_ref[g]
        o = _softmax_pv(s, va[:, g * HEAD_DIM:(g + 1) * HEAD_DIM], sink_ref[g])
        outs += [o[j * CHUNK:(j + 1) * CHUNK] for j in range(A_GROUP)]
    for h in range(B_HEADS):
        cols = slice(h * HEAD_DIM, (h + 1) * HEAD_DIM)
        s = _dot_nt(qb[:, cols], kb[:, cols]) + bias_b_ref[h]
        outs.append(_softmax_pv(s, vb[:, cols], None))
    return jnp.concatenate(outs, axis=1).astype(BF16)


P_TILE = 4 * CHUNK
P_SUB = 2 * CHUNK
PB_KEYS = B_PREV + P_TILE
PA_KEYS = A_PREV + P_SUB


def _attn_prompt_kernel(qa_ref, qb_ref, kt2_ref, kt1_ref, kt0_ref, vb2_ref, vb1_ref, vb0_ref, kat_ref, katp_ref,
                        kva_ref, kvap_ref, bias_a_ref, sink_ref, bias_b_ref, o_ref, kbt_s, vb_s, kat_s, va_s):
    for j, (k_ref, v_ref) in enumerate(((kt2_ref, vb2_ref), (kt1_ref, vb1_ref), (kt0_ref, vb0_ref))):
        kbt_s[:, j * P_TILE:(j + 1) * P_TILE] = k_ref[...]
        vb_s[j * P_TILE:(j + 1) * P_TILE] = v_ref[...]
    kat_s[:, 0:A_PREV] = katp_ref[...]
    kat_s[:, A_PREV:A_PREV + P_TILE] = kat_ref[...]
    va_s[0:A_PREV] = kvap_ref[:, A_KVW:2 * A_KVW]
    va_s[A_PREV:A_PREV + P_TILE] = kva_ref[:, A_KVW:2 * A_KVW]

    for sb in range(P_TILE // P_SUB):
        q_rows = slice(sb * P_SUB, (sb + 1) * P_SUB)
        k_rows = slice(sb * P_SUB, sb * P_SUB + PA_KEYS)
        for g in range(A_KV_HEADS):
            heads = range(g * A_GROUP, (g + 1) * A_GROUP)
            hd = slice(g * HEAD_DIM, (g + 1) * HEAD_DIM)
            qs = jnp.concatenate([qa_ref[q_rows, h * HEAD_DIM:(h + 1) * HEAD_DIM] for h in heads], axis=0)
            s = _dot(qs, kat_s[hd, k_rows]) + bias_a_ref[sb, g]
            o = _softmax_pv(s, va_s[k_rows, hd], sink_ref[g])
            for j in range(0, A_GROUP, 2):
                pair = jnp.concatenate([o[j * P_SUB:(j + 1) * P_SUB], o[(j + 1) * P_SUB:(j + 2) * P_SUB]], axis=1)
                c0 = (g * A_GROUP + j) * HEAD_DIM
                o_ref[q_rows, c0:c0 + 2 * HEAD_DIM] = pair.astype(BF16)

    for h in range(0, B_HEADS, 2):
        pair = []
        for hh in (h, h + 1):
            cols = slice(hh * HEAD_DIM, (hh + 1) * HEAD_DIM)
            s = _dot(qb_ref[:, cols], kbt_s[cols, :]) + bias_b_ref[hh]
            pair.append(_softmax_pv(s, vb_s[:, cols], None))
        c0 = A_QW + h * HEAD_DIM
        o_ref[:, c0:c0 + 2 * HEAD_DIM] = jnp.concatenate(pair, axis=1).astype(BF16)


def _attn_prompt_call(proj, kbt, kat, bias_a, sink, bias_b):
    t = proj.shape[0]
    back2 = lambda i: jnp.maximum(i - 2, 0)
    back1 = lambda i: jnp.maximum(i - 1, 0)
    prev_a = lambda i: jnp.maximum(i * (P_TILE // A_PREV) - 1, 0)
    in_specs = [
        pl.BlockSpec((P_TILE, 512), lambda i: (i, 0)),
        pl.BlockSpec((P_TILE, 512), lambda i: (i, 1)),
        pl.BlockSpec((B_W, P_TILE), lambda i: (0, back2(i))),
        pl.BlockSpec((B_W, P_TILE), lambda i: (0, back1(i))),
        pl.BlockSpec((B_W, P_TILE), lambda i: (0, i)),
        pl.BlockSpec((P_TILE, 512), lambda i: (back2(i), 3)),
        pl.BlockSpec((P_TILE, 512), lambda i: (back1(i), 3)),
        pl.BlockSpec((P_TILE, 512), lambda i: (i, 3)),
        pl.BlockSpec((A_KVW, P_TILE), lambda i: (0, i)),
        pl.BlockSpec((A_KVW, A_PREV), lambda i: (0, prev_a(i))),
        pl.BlockSpec((P_TILE, 256), lambda i: (i, 8)),
        pl.BlockSpec((A_PREV, 256), lambda i: (prev_a(i), 8)),
        pl.BlockSpec((None,) + bias_a.shape[1:], lambda i: (jnp.minimum(i, 1), 0, 0, 0, 0)),
        _const_spec(sink.shape),
        pl.BlockSpec((None,) + bias_b.shape[1:], lambda i: (jnp.minimum(i, 2), 0, 0, 0)),
    ]
    return pl.pallas_call(
        _attn_prompt_kernel,
        grid=(t // P_TILE,),
        in_specs=in_specs,
        out_specs=pl.BlockSpec((P_TILE, D_MODEL), lambda i: (i, 0)),
        out_shape=jax.ShapeDtypeStruct((t, D_MODEL), BF16),
        scratch_shapes=[pltpu.VMEM((B_W, PB_KEYS), BF16), pltpu.VMEM((PB_KEYS, B_W), BF16),
                        pltpu.VMEM((A_KVW, A_PREV + P_TILE), BF16), pltpu.VMEM((A_PREV + P_TILE, A_KVW), BF16)],
        compiler_params=_params(1),
        name="attn_prompt",
    )(proj, proj, kbt, kbt, kbt, proj, proj, proj, kat, kat, proj, proj, bias_a, sink, bias_b)


def _attn_sample_kernel(qa_ref, qb_ref, kb_ref, vb_ref, kva_ref, cak_ref, cav_ref, cbk_ref, cbv_ref,
                        bias_a_ref, sink_ref, bias_b_ref, o_ref, *, nb):
    for b in range(nb):
        rows = slice(b * CHUNK, (b + 1) * CHUNK)
        ka = jnp.concatenate([cak_ref[b].astype(BF16), kva_ref[rows, 0:A_KVW]], axis=0)
        va = jnp.concatenate([cav_ref[b].astype(BF16), kva_ref[rows, A_KVW:2 * A_KVW]], axis=0)
        kb = jnp.concatenate([cbk_ref[b].astype(BF16), kb_ref[rows, :]], axis=0)
        vb = jnp.concatenate([cbv_ref[b].astype(BF16), vb_ref[rows, :]], axis=0)
        o_ref[rows, :] = _attn_chunk(qa_ref[rows, :], qb_ref[rows, :], ka, va, kb, vb,
                                     bias_a_ref, sink_ref, bias_b_ref)


def _attn_sample_call(proj, ca_k, ca_v, cb_k, cb_v, e, bias_a, sink, bias_b, nb):
    t = proj.shape[0]
    nbatch = t // CHUNK
    tm = nb * CHUNK
    in_specs = [
        pl.BlockSpec((tm, 512), lambda i: (i, 0)),
        pl.BlockSpec((tm, 512), lambda i: (i, 1)),
        pl.BlockSpec((tm, 512), lambda i: (i, 2)),
        pl.BlockSpec((tm, 512), lambda i: (i, 3)),
        pl.BlockSpec((tm, 256), lambda i: (i, 8)),
        pl.BlockSpec((None, nb, A_PREV, A_KVW), lambda i: (e, i, 0, 0)),
        pl.BlockSpec((None, nb, A_PREV, A_KVW), lambda i: (e, i, 0, 0)),
        pl.BlockSpec((None, nb, B_PREV, B_W), lambda i: (e, i, 0, 0)),
        pl.BlockSpec((None, nb, B_PREV, B_W), lambda i: (e, i, 0, 0)),
        _const_spec(bias_a.shape), _const_spec(sink.shape), _const_spec(bias_b.shape),
    ]
    return pl.pallas_call(
        functools.partial(_attn_sample_kernel, nb=nb),
        grid=(nbatch // nb,),
        in_specs=in_specs,
        out_specs=pl.BlockSpec((tm, D_MODEL), lambda i: (i, 0)),
        out_shape=jax.ShapeDtypeStruct((t, D_MODEL), BF16),
        compiler_params=_params(1),
        name="attn_sample",
    )(proj, proj, proj, proj, proj, ca_k, ca_v, cb_k, cb_v, bias_a, sink, bias_b)


def _mlstm_chunk(qk, v, o, gates, c_s, n_s, m_s, tril, causal):
    lane = lax.broadcasted_iota(jnp.int32, gates.shape, 1)
    lf = jnp.minimum(gates, 0.0) - jnp.log(1.0 + jnp.exp(-jnp.abs(gates)))
    lf = jnp.where((lane >= M_HEADS) & (lane < 2 * M_HEADS), lf, 0.0)
    hi = lf.astype(BF16)
    r1 = lf - hi.astype(F32)
    mid = r1.astype(BF16)
    lo = (r1 - mid.astype(F32)).astype(BF16)
    bcum = _dot(tril, hi) + _dot(tril, mid) + _dot(tril, lo)
    comb = jnp.where(lane < M_HEADS, gates, bcum)
    comb_t = comb.T
    k_scale = M_DK ** -0.5
    outs = []
    for h in range(M_HEADS):
        q = qk[:, h * M_DK:(h + 1) * M_DK]
        k = qk[:, M_QKW + h * M_DK:M_QKW + (h + 1) * M_DK]
        vh = v[:, h * M_DV:(h + 1) * M_DV]
        ig_col = comb[:, h:h + 1]
        b_col = comb[:, M_HEADS + h:M_HEADS + h + 1]
        ig_row = comb_t[h:h + 1, :]
        b_row = comb_t[M_HEADS + h:M_HEADS + h + 1, :]
        m0 = m_s[h:h + 1, 0:1]
        c0 = c_s[h]
        n0 = n_s[h:h + 1, :]
        dmat = jnp.where(causal, b_col - b_row + ig_row, NEG_INF)
        inter = b_col + m0
        m_t = jnp.maximum(inter, jnp.max(dmat, axis=-1, keepdims=True))
        w = jnp.exp(dmat - m_t)
        sc = _dot_nt(q, k) * k_scale * w
        g_inter = jnp.exp(inter - m_t)
        num = _dot(sc.astype(BF16), vh) + g_inter * _dot(q, c0.astype(BF16))
        den = jnp.sum(sc, axis=-1, keepdims=True) + g_inter * jnp.sum(q.astype(F32) * n0, axis=-1, keepdims=True)
        outs.append(num / jnp.maximum(jnp.abs(den), jnp.exp(-m_t)))
        b_last = b_col[CHUNK - 1:CHUNK, :]
        g_col = b_last - b_col + ig_col
        m_new = jnp.maximum(b_last + m0, jnp.max(g_col, axis=0, keepdims=True))
        decay = jnp.exp(b_last + m0 - m_new)
        wk = jnp.exp(g_col - m_new) * (k.astype(F32) * k_scale)
        c_s[h] = decay * c0 + _dot_tn(wk.astype(BF16), vh)
        n_s[h:h + 1, :] = decay * n0 + jnp.sum(wk, axis=0, keepdims=True)
        m_s[h:h + 1, :] = jnp.broadcast_to(m_new, (1, 128))
    hs = jnp.concatenate(outs, axis=1)
    of = o.astype(F32)
    return (hs * (1.0 / (1.0 + jnp.exp(-of)))).astype(BF16)


def _mlstm_kernel(qk_ref, v_ref, o_ref, gates_ref, bias_ref, c_in, n_in, m_in,
                  hg_ref, c_out, n_out, m_out, c_s, n_s, m_s, *, chunks):
    j = pl.program_id(1)

    @pl.when(j == 0)
    def _():
        c_s[...] = c_in[0]
        n_s[...] = n_in[0]
        m_s[...] = m_in[0]

    row = lax.broadcasted_iota(jnp.int32, (CHUNK, CHUNK), 0)
    col = lax.broadcasted_iota(jnp.int32, (CHUNK, CHUNK), 1)
    causal = col <= row
    tril = jnp.where(causal, 1.0, 0.0).astype(BF16)
    for c in range(chunks):
        rows = slice(c * CHUNK, (c + 1) * CHUNK)
        hg_ref[rows, :] = _mlstm_chunk(qk_ref[rows, :], v_ref[rows, :], o_ref[rows, :],
                                       gates_ref[rows, :] + bias_ref[...], c_s, n_s, m_s, tril, causal)

    @pl.when(j == pl.num_programs(1) - 1)
    def _():
        c_out[0] = c_s[...]
        n_out[0] = n_s[...]
        m_out[0] = m_s[...]


def _mlstm_call(proj, gates, bias, c0, n0, m0, layer, chunks_per_step):
    t = proj.shape[0]
    nseq = c0.shape[1]
    rows = chunks_per_step * CHUNK
    nsteps = t // (nseq * rows)
    blk = lambda s, j: s * nsteps + j
    in_specs = [
        pl.BlockSpec((rows, 1024), lambda s, j: (blk(s, j), 0)),
        pl.BlockSpec((rows, 1024), lambda s, j: (blk(s, j), 1)),
        pl.BlockSpec((rows, 1024), lambda s, j: (blk(s, j), 2)),
        pl.BlockSpec((rows, 128), lambda s, j: (blk(s, j), 0)),
        _const_spec((1, 128)),
        pl.BlockSpec((None, 1, M_HEADS, M_DK, M_DV), lambda s, j: (layer, s, 0, 0, 0)),
        pl.BlockSpec((None, 1, M_HEADS, M_DK), lambda s, j: (layer, s, 0, 0)),
        pl.BlockSpec((None, 1, M_HEADS, 128), lambda s, j: (layer, s, 0, 0)),
    ]
    out_specs = (
        pl.BlockSpec((rows, M_VW), lambda s, j: (blk(s, j), 0)),
        pl.BlockSpec((1, M_HEADS, M_DK, M_DV), lambda s, j: (s, 0, 0, 0)),
        pl.BlockSpec((1, M_HEADS, M_DK), lambda s, j: (s, 0, 0)),
        pl.BlockSpec((1, M_HEADS, 128), lambda s, j: (s, 0, 0)),
    )
    out_shape = (
        jax.ShapeDtypeStruct((t, M_VW), BF16),
        jax.ShapeDtypeStruct((nseq, M_HEADS, M_DK, M_DV), F32),
        jax.ShapeDtypeStruct((nseq, M_HEADS, M_DK), F32),
        jax.ShapeDtypeStruct((nseq, M_HEADS, 128), F32),
    )
    return pl.pallas_call(
        functools.partial(_mlstm_kernel, chunks=chunks_per_step),
        grid=(nseq, nsteps),
        in_specs=in_specs,
        out_specs=out_specs,
        out_shape=out_shape,
        scratch_shapes=[pltpu.VMEM((M_HEADS, M_DK, M_DV), F32), pltpu.VMEM((M_HEADS, M_DK), F32),
                        pltpu.VMEM((M_HEADS, 128), F32)],
        compiler_params=_params(2),
        name="mlstm",
    )(proj, proj, proj, gates, bias, c0, n0, m0)


def _post_kernel(x_ref, a_ref, wout_ref, g_ref, wq_ref, mk_ref, mv_ref, wo_ref, o_ref, att_s, *, nb, tb):
    x1 = x_ref[...] + _dot(a_ref[...], wout_ref[...])
    q = _dot(_rms(x1, g_ref[...]).astype(BF16), wq_ref[...]).astype(BF16)
    scale = MEM_HD ** -0.5
    for b in range(nb):
        rows = slice(b * tb, (b + 1) * tb)
        for h in range(MEM_HEADS):
            cols = slice(h * MEM_HD, (h + 1) * MEM_HD)
            s = _dot_nt(q[rows, cols], mk_ref[b, :, cols].astype(BF16)) * scale
            e = jnp.exp(s - jnp.max(s, axis=-1, keepdims=True))
            p = (e / jnp.sum(e, axis=-1, keepdims=True)).astype(BF16)
            att_s[rows, cols] = _dot(p, mv_ref[b, :, cols].astype(BF16)).astype(BF16)
    o_ref[...] = x1 + _dot(att_s[...], wo_ref[...])


def _post_call(x, a, w_out, wl, g, w_q, mk, mv, w_o, layer, tm, tb):
    t = x.shape[0]
    nb = tm // tb
    shared = mk.shape[1] == 1
    mem_map = (lambda i: (layer, 0, 0, 0)) if shared else (lambda i: (layer, i, 0, 0))
    row = lambda i: (i, 0)
    in_specs = [
        pl.BlockSpec((tm, D_MODEL), row),
        pl.BlockSpec((tm, a.shape[1]), row),
        _layer_spec(w_out, wl),
        _layer_spec(g, layer),
        _layer_spec(w_q, layer),
        pl.BlockSpec((None, nb, MEM_TOKENS, D_MODEL), mem_map),
        pl.BlockSpec((None, nb, MEM_TOKENS, D_MODEL), mem_map),
        _layer_spec(w_o, layer),
    ]
    return pl.pallas_call(
        functools.partial(_post_kernel, nb=nb, tb=tb),
        grid=(t // tm,),
        in_specs=in_specs,
        out_specs=pl.BlockSpec((tm, D_MODEL), row),
        out_shape=jax.ShapeDtypeStruct((t, D_MODEL), F32),
        scratch_shapes=[pltpu.VMEM((tm, D_MODEL), BF16)],
        compiler_params=_params(1),
        name="post_cross",
    )(x, a, w_out, g, w_q, mk, mv, w_o)


def _conv3(u, cw, hists, seq_rows):
    w0, w1, w2 = cw[0:1], cw[1:2], cw[2:3]
    c = pltpu.roll(u, 2, 0) * w0 + pltpu.roll(u, 1, 0) * w1 + u * w2
    row = lax.broadcasted_iota(jnp.int32, (8, u.shape[1]), 0)
    pieces = []
    for s, hist in enumerate(hists):
        base = s * seq_rows
        t8 = u[base:base + 8]
        h0, h1 = hist[0:1], hist[1:2]
        u1 = jnp.where(row == 0, h1, pltpu.roll(t8, 1, 0))
        u2 = jnp.where(row == 0, h0, jnp.where(row == 1, h1, pltpu.roll(t8, 2, 0)))
        pieces += [u2 * w0 + u1 * w1 + t8 * w2, c[base + 8:base + seq_rows]]
    return jnp.concatenate(pieces, axis=0)


def _ffn_kernel(*refs, seq_rows, carry, final_norm):
    x_ref, g_ref, win_ref, cw_ref, wout_ref = refs[:5]
    pos = 5
    hist_ref = None
    if not carry:
        hist_ref = refs[pos]
        pos += 1
    gfin_ref = None
    if final_norm:
        gfin_ref = refs[pos]
        pos += 1
    o_ref, cs_ref, act_s = refs[pos:pos + 3]
    hist_s = refs[pos + 3] if carry else None

    tm = x_ref.shape[0]
    nseq = tm // seq_rows
    x = x_ref[...]
    h = _rms(x, g_ref[...]).astype(BF16)

    if carry:
        @pl.when(pl.program_id(0) == 0)
        def _():
            hist_s[...] = jnp.zeros_like(hist_s)

    for j in range(D_FF // FF_CHUNK):
        halves = []
        for half in range(2):
            c0 = half * D_FF + j * FF_CHUNK
            cols = slice(c0, c0 + FF_CHUNK)
            u = _dot(h, win_ref[:, cols])
            if carry:
                hists = [hist_s[:, cols]]
            else:
                hists = [hist_ref[s, :, cols] for s in range(nseq)]
            halves.append(_conv3(u, cw_ref[:, cols], hists, seq_rows))
            for s in range(nseq):
                last2 = u[(s + 1) * seq_rows - 2:(s + 1) * seq_rows]
                if carry:
                    hist_s[:, cols] = last2
                    cs_ref[:, cols] = last2
                else:
                    cs_ref[s, :, cols] = last2
        ca, cg = halves
        act = ca * (1.0 / (1.0 + jnp.exp(-ca))) * cg
        act_s[:, j * FF_CHUNK:(j + 1) * FF_CHUNK] = act.astype(BF16)

    y = x + _dot(act_s[...], wout_ref[...])
    if final_norm:
        y = _rms(y, gfin_ref[...])
    o_ref[...] = y


def _ffn_call(x, g, w_in, conv_w, w_out, layer, hist, g_final, tm):
    t = x.shape[0]
    carry = hist is None
    seq_rows = tm if carry else CHUNK
    nseq = tm // seq_rows
    row = lambda i: (i, 0)
    args = [x, g, w_in, conv_w, w_out]
    in_specs = [pl.BlockSpec((tm, D_MODEL), row), _layer_spec(g, layer),
                _layer_spec(w_in, layer), _layer_spec(conv_w, layer), _layer_spec(w_out, layer)]
    if carry:
        cs_shape, cs_spec = (2, 2 * D_FF), _const_spec((2, 2 * D_FF))
    else:
        args.append(hist)
        in_specs.append(pl.BlockSpec((None, nseq, 2, 2 * D_FF), lambda i: (layer, i, 0, 0)))
        cs_shape, cs_spec = hist.shape[1:], pl.BlockSpec((nseq, 2, 2 * D_FF), lambda i: (i, 0, 0))
    if g_final is not None:
        args.append(g_final.reshape(1, D_MODEL))
        in_specs.append(_const_spec((1, D_MODEL)))
    scratch = [pltpu.VMEM((tm, D_FF), BF16)]
    if carry:
        scratch.append(pltpu.VMEM((2, 2 * D_FF), F32))
    return pl.pallas_call(
        functools.partial(_ffn_kernel, seq_rows=seq_rows, carry=carry, final_norm=g_final is not None),
        grid=(t // tm,),
        in_specs=in_specs,
        out_specs=(pl.BlockSpec((tm, D_MODEL), row), cs_spec),
        out_shape=(jax.ShapeDtypeStruct((t, D_MODEL), F32), jax.ShapeDtypeStruct(cs_shape, F32)),
        scratch_shapes=scratch,
        compiler_params=_params(1),
        name="conv_ffn",
    )(*args)


def _alibi(n_q, n_k):
    q = np.arange(n_q)[:, None]
    dist = np.abs(q + A_PREV - np.arange(n_k)[None, :]).astype(np.float32)
    slopes = (2.0 ** (-8.0 * np.arange(1, A_HEADS + 1) / A_HEADS)).astype(np.float32)
    return -slopes[:, None, None] * dist[None]


def _band_mask(n_q, n_k, n_prev_chunks, first_valid_chunk):
    qc = (np.arange(n_q) // CHUNK)[:, None]
    kc = (np.arange(n_k) // CHUNK)[None, :]
    ok = (kc >= qc) & (kc <= qc + n_prev_chunks) & (kc >= first_valid_chunk)
    return np.where(ok, 0.0, NEG_INF).astype(np.float32)


def _alibi_sample_table():
    return jnp.asarray(_alibi(CHUNK, A_BAND).reshape(A_KV_HEADS, A_GROUP * CHUNK, A_BAND))


def _alibi_prompt_tables():
    n_prev = A_PREV // CHUNK
    base = _alibi(P_SUB, PA_KEYS)
    general = (base + _band_mask(P_SUB, PA_KEYS, n_prev, 0)).reshape(A_KV_HEADS, A_GROUP * P_SUB, PA_KEYS)
    first = (base + _band_mask(P_SUB, PA_KEYS, n_prev, n_prev)).reshape(A_KV_HEADS, A_GROUP * P_SUB, PA_KEYS)
    return jnp.asarray(np.stack([np.stack([first, general]), np.stack([general, general])]))


def _relpos_toeplitz(table, n_q, n_k):
    h, n_rel = table.shape
    n_lo = n_k - B_BAND
    n_hi = n_q + n_k - 1 - n_lo - n_rel
    ext = jnp.concatenate([jnp.broadcast_to(table[:, :1], (h, n_lo)), table,
                           jnp.broadcast_to(table[:, -1:], (h, n_hi))], axis=1)
    rev = ext[:, ::-1]
    return jnp.stack([rev[:, n_q - 1 - q:n_q - 1 - q + n_k] for q in range(n_q)], axis=1)


def _relpos_prompt_tables(table):
    n_prev = B_PREV // CHUNK
    masks = np.stack([_band_mask(P_TILE, PB_KEYS, n_prev, max(n_prev - t * (P_TILE // CHUNK), 0)) for t in range(3)])
    return _relpos_toeplitz(table, P_TILE, PB_KEYS)[None] + jnp.asarray(masks)[:, None]


def _row_tile(t):
    return PROMPT_TILE if t % PROMPT_TILE == 0 else 256


def kernel(x_prompt, x_sample, mem_prompt, cache_a_k, cache_a_v, cache_b_k, cache_b_v, state_mlstm_c, state_mlstm_n, state_mlstm_m, cache_mem_k, cache_mem_v, state_ffn_conv, norm_mix, norm_cross, norm_ffn, norm_final, w_in_att, w_out_att, sink_a, relpos_b, w_in_mlstm, b_igate, b_fgate, w_out_mlstm, w_mem_q, w_mem_k, w_mem_v, w_mem_o, w_ffn_in, conv_ffn, w_ffn_out):
    depth = norm_mix.shape[0]
    assert x_prompt.shape[0] == 1
    tp = x_prompt.shape[1]
    nb, ds = x_sample.shape[0], x_sample.shape[1]
    assert ds == CHUNK and tp % PROMPT_TILE == 0
    ts = nb * ds
    tm_p, tm_s = PROMPT_TILE, _row_tile(ts)
    xp = x_prompt[0]
    xs = x_sample.reshape(ts, D_MODEL)

    memkv = _memkv_call(mem_prompt[0], jnp.concatenate([w_mem_k, w_mem_v], axis=0).astype(BF16))
    p_mk, p_mv = memkv[:depth][:, None], memkv[depth:][:, None]
    s_mk = cache_mem_k.reshape(depth, nb, MEM_TOKENS, D_MODEL)
    s_mv = cache_mem_v.reshape(depth, nb, MEM_TOKENS, D_MODEL)

    g_mix = norm_mix.astype(F32)[:, None]
    g_cross = norm_cross.astype(F32)[:, None]
    g_ffn = norm_ffn.astype(F32)[:, None]
    wa = w_in_att
    scale = HEAD_DIM ** -0.5
    w_att = jnp.concatenate([wa[:, :, 0:512] * scale, wa[:, :, 768:1280] * scale, wa[:, :, 1280:2304],
                             wa[:, :, 512:768]], axis=2).astype(BF16)
    wm = w_in_mlstm
    w_mls = jnp.concatenate([wm, jnp.zeros(wm.shape[:2] + (128 - 2 * M_HEADS,), wm.dtype)], axis=2).astype(BF16)
    w_out_a, w_out_m = w_out_att.astype(BF16), w_out_mlstm.astype(BF16)
    w_q, w_o = w_mem_q.astype(BF16), w_mem_o.astype(BF16)
    w_fi, w_fo = w_ffn_in.astype(BF16), w_ffn_out.astype(BF16)
    conv_w = conv_ffn.astype(F32)
    hist_s = state_ffn_conv.astype(F32)
    n_even = cache_a_k.shape[0]
    ca_k = cache_a_k.reshape(n_even, nb, A_PREV, A_KVW)
    ca_v = cache_a_v.reshape(n_even, nb, A_PREV, A_KVW)
    cb_k = cache_b_k.reshape(n_even, nb, B_PREV, B_W)
    cb_v = cache_b_v.reshape(n_even, nb, B_PREV, B_W)
    st_c = state_mlstm_c.astype(F32)
    st_n = state_mlstm_n.astype(F32)
    st_m = jnp.broadcast_to(state_mlstm_m.astype(F32)[..., None], state_mlstm_m.shape + (128,))
    zc = jnp.zeros((1, 1, M_HEADS, M_DK, M_DV), F32)
    zn = jnp.zeros((1, 1, M_HEADS, M_DK), F32)
    zm = jnp.zeros((1, 1, M_HEADS, 128), F32)

    bias_a_s = _alibi_sample_table()
    bias_a_p = _alibi_prompt_tables()
    post_tm_s = 256 if ts % 256 == 0 else ts
    attn_nb = 4 if nb % 4 == 0 else 1

    p_ak, p_av, p_bk, p_bv, p_c, p_n, p_m, p_conv = [], [], [], [], [], [], [], []
    s_ak, s_av, s_bk, s_bv, s_c, s_n, s_m, s_conv = [], [], [], [], [], [], [], []
    for l in range(depth):
        if l % 2 == 0:
            e = l // 2
            proj_p, kv_p, kbt_p, kat_p = _att_proj(xp, g_mix, l, w_att, e, tm_p, prompt=True)
            proj_s, kv_s = _att_proj(xs, g_mix, l, w_att, e, tm_s, prompt=False)
            sk = sink_a[e].astype(F32)
            sink_p = jnp.repeat(sk, P_SUB).reshape(A_KV_HEADS, A_GROUP * P_SUB, 1)
            sink_s = jnp.repeat(sk, CHUNK).reshape(A_KV_HEADS, A_GROUP * CHUNK, 1)
            rel = relpos_b[e].astype(F32)
            mix_p = _attn_prompt_call(proj_p, kbt_p, kat_p, bias_a_p, sink_p, _relpos_prompt_tables(rel))
            mix_s = _attn_sample_call(proj_s, ca_k, ca_v, cb_k, cb_v, e, bias_a_s, sink_s,
                                      _relpos_toeplitz(rel, CHUNK, B_BAND), attn_nb)
            w_mix_out, wl = w_out_a, e
            p_ak.append(kv_p[tm_p - A_PREV:, 0:128].reshape(1, A_PREV, A_KV_HEADS, HEAD_DIM))
            p_av.append(kv_p[tm_p - A_PREV:, 128:256].reshape(1, A_PREV, A_KV_HEADS, HEAD_DIM))
            p_bk.append(kv_p[:, 256:768].reshape(1, B_PREV, B_HEADS, HEAD_DIM))
            p_bv.append(kv_p[:, 768:1280].reshape(1, B_PREV, B_HEADS, HEAD_DIM))
            s_ak.append(kv_s[:, 0:128].reshape(nb, ds, A_KV_HEADS, HEAD_DIM))
            s_av.append(kv_s[:, 128:256].reshape(nb, ds, A_KV_HEADS, HEAD_DIM))
            s_bk.append(kv_s[:, 256:768].reshape(nb, ds, B_HEADS, HEAD_DIM))
            s_bv.append(kv_s[:, 768:1280].reshape(nb, ds, B_HEADS, HEAD_DIM))
        else:
            o = l // 2
            gate_bias = jnp.concatenate([b_igate[o].astype(F32), b_fgate[o].astype(F32),
                                         jnp.zeros((128 - 2 * M_HEADS,), F32)]).reshape(1, 128)
            proj_p, gates_p = _mlstm_proj(xp, g_mix, l, w_mls, o, tm_p)
            proj_s, gates_s = _mlstm_proj(xs, g_mix, l, w_mls, o, tm_s)
            mix_p, c, n, m = _mlstm_call(proj_p, gates_p, gate_bias, zc, zn, zm, 0, 4)
            p_c.append(c); p_n.append(n); p_m.append(m[:, :, 0])
            mix_s, c, n, m = _mlstm_call(proj_s, gates_s, gate_bias, st_c, st_n, st_m, o, 1)
            s_c.append(c); s_n.append(n); s_m.append(m[:, :, 0])
            w_mix_out, wl = w_out_m, o

        xp = _post_call(xp, mix_p, w_mix_out, wl, g_cross, w_q, p_mk, p_mv, w_o, l, tm_p, tm_p)
        xs = _post_call(xs, mix_s, w_mix_out, wl, g_cross, w_q, s_mk, s_mv, w_o, l, post_tm_s, CHUNK)

        g_fin = norm_final if l == depth - 1 else None
        xp, cv = _ffn_call(xp, g_ffn, w_fi, conv_w, w_fo, l, None, g_fin, tm_p)
        p_conv.append(cv[None])
        xs, cv = _ffn_call(xs, g_ffn, w_fi, conv_w, w_fo, l, hist_s, g_fin, tm_s)
        s_conv.append(cv)

    p_mk, p_mv = memkv[:depth], memkv[depth:]
    mem_shape = (depth, 1, MEM_TOKENS, MEM_HEADS, MEM_HD)
    return (xp[None], xs.reshape(nb, ds, D_MODEL),
            jnp.stack(p_ak), jnp.stack(p_av), jnp.stack(p_bk), jnp.stack(p_bv),
            jnp.stack(p_c), jnp.stack(p_n), jnp.stack(p_m),
            p_mk.reshape(mem_shape), p_mv.reshape(mem_shape), jnp.stack(p_conv),
            jnp.stack(s_ak), jnp.stack(s_av), jnp.stack(s_bk), jnp.stack(s_bv),
            jnp.stack(s_c), jnp.stack(s_n), jnp.stack(s_m), jnp.stack(s_conv))
```

```python
import functools

import numpy as np
import jax
import jax.numpy as jnp
from jax import lax
from jax.experimental import pallas as pl
from jax.experimental.pallas import tpu as pltpu

F32 = jnp.float32
BF16 = jnp.bfloat16

D_MODEL = 1024
CHUNK = 64
HEAD_DIM = 64
A_HEADS = 8
A_KV_HEADS = 2
A_GROUP = A_HEADS // A_KV_HEADS
A_BAND = 3 * CHUNK
A_PREV = 2 * CHUNK
B_HEADS = 8
B_BAND = 9 * CHUNK
B_PREV = 8 * CHUNK
REL_CLIP = 128
A_QW = A_HEADS * HEAD_DIM
A_KVW = A_KV_HEADS * HEAD_DIM
B_W = B_HEADS * HEAD_DIM
ATT_PROJ = A_QW + 2 * A_KVW + 3 * B_W
KV32_W = 2 * A_KVW + 2 * B_W
M_HEADS = 8
M_DK = 64
M_DV = 128
M_QKW = M_HEADS * M_DK
M_VW = M_HEADS * M_DV
M_MAIN = 2 * M_QKW + 2 * M_VW
MEM_TOKENS = 256
MEM_HEADS = 4
MEM_HD = D_MODEL // MEM_HEADS
D_FF = 2816
FF_CHUNK = 256
EPS = 1e-6
NEG_INF = float("-inf")

VMEM_LIMIT_BYTES = 56 * 1024 * 1024
PROMPT_TILE = 512


def _params(n_axes):
    return pltpu.CompilerParams(dimension_semantics=("arbitrary",) * n_axes,
                                vmem_limit_bytes=VMEM_LIMIT_BYTES)


def _const_spec(shape):
    zeros = (0,) * len(shape)
    return pl.BlockSpec(shape, lambda *_: zeros)


def _layer_spec(stack, layer):
    idx = (layer,) + (0,) * (stack.ndim - 1)
    return pl.BlockSpec((None,) + stack.shape[1:], lambda *_: idx)


def _rms(x, g):
    return x * lax.rsqrt(jnp.mean(x * x, axis=-1, keepdims=True) + EPS) * g


def _dot(a, b):
    return jnp.dot(a, b, preferred_element_type=F32)


def _dot_nt(a, b):
    return lax.dot_general(a, b, (((1,), (1,)), ((), ())), preferred_element_type=F32)


def _dot_tn(a, b):
    return lax.dot_general(a, b, (((0,), (0,)), ((), ())), preferred_element_type=F32)


def _proj_kernel(x_ref, g_ref, w_ref, *out_refs, plan, transposed):
    h = _rms(x_ref[...], g_ref[...]).astype(BF16)
    for c0, cw, dsts in plan:
        r = _dot(h, w_ref[:, c0:c0 + cw])
        for oi, oc in dsts:
            out_refs[oi][:, oc:oc + cw] = r.astype(out_refs[oi].dtype)
        if c0 in transposed:
            oi, width = transposed[c0]
            out_refs[oi][...] = r[:, :width].T.astype(out_refs[oi].dtype)


def _proj_call(x, g, gl, w, wl, plan, transposed, out_shapes, out_specs, tm, name):
    t = x.shape[0]
    return pl.pallas_call(
        functools.partial(_proj_kernel, plan=plan, transposed=transposed),
        grid=(t // tm,),
        in_specs=[pl.BlockSpec((tm, D_MODEL), lambda i: (i, 0)),
                  _layer_spec(g, gl),
                  _layer_spec(w, wl)],
        out_specs=out_specs,
        out_shape=out_shapes,
        compiler_params=_params(1),
        name=name,
    )(x, g, w)


_ATT_PLAN = (
    (0, 512, ((0, 0),)),
    (512, 512, ((0, 512),)),
    (1024, 512, ((0, 1024), (1, 256))),
    (1536, 512, ((0, 1536), (1, 768))),
    (2048, 256, ((0, 2048), (1, 0))),
)


def _att_proj(x, g, gl, w_perm, wl, tm, prompt):
    t = x.shape[0]
    shapes = [jax.ShapeDtypeStruct((t, ATT_PROJ), BF16)]
    specs = [pl.BlockSpec((tm, ATT_PROJ), lambda i: (i, 0))]
    transposed = {}
    if prompt:
        shapes += [jax.ShapeDtypeStruct((tm, KV32_W), F32), jax.ShapeDtypeStruct((B_W, t), BF16),
                   jax.ShapeDtypeStruct((A_KVW, t), BF16)]
        specs += [_const_spec((tm, KV32_W)), pl.BlockSpec((B_W, tm), lambda i: (0, i)),
                  pl.BlockSpec((A_KVW, tm), lambda i: (0, i))]
        transposed = {1024: (2, B_W), 2048: (3, A_KVW)}
    else:
        shapes.append(jax.ShapeDtypeStruct((t, KV32_W), F32))
        specs.append(pl.BlockSpec((tm, KV32_W), lambda i: (i, 0)))
    return _proj_call(x, g, gl, w_perm, wl, _ATT_PLAN, transposed, tuple(shapes), tuple(specs), tm, "att_proj")


M_GATES_W = 2 * M_QKW
_MLSTM_PLAN = (tuple((c, 512, ((0, c),)) for c in range(0, M_MAIN, 512))
               + tuple((M_MAIN + c, 512, ((1, c),)) for c in range(0, M_GATES_W, 512)))


def _mlstm_proj(x, g, gl, w_rep, wl, tm):
    t = x.shape[0]
    return _proj_call(
        x, g, gl, w_rep, wl, _MLSTM_PLAN, {},
        (jax.ShapeDtypeStruct((t, M_MAIN), BF16), jax.ShapeDtypeStruct((t, M_GATES_W), F32)),
        (pl.BlockSpec((tm, M_MAIN), lambda i: (i, 0)), pl.BlockSpec((tm, M_GATES_W), lambda i: (i, 0))),
        tm, "mlstm_proj")


def _memkv_kernel(m_ref, w_ref, o_ref):
    o_ref[0] = _dot(m_ref[...].astype(BF16), w_ref[0])


def _memkv_call(mem, w):
    n = w.shape[0]
    return pl.pallas_call(
        _memkv_kernel,
        grid=(n,),
        in_specs=[_const_spec(mem.shape), pl.BlockSpec((1, D_MODEL, D_MODEL), lambda i: (i, 0, 0))],
        out_specs=pl.BlockSpec((1, MEM_TOKENS, D_MODEL), lambda i: (i, 0, 0)),
        out_shape=jax.ShapeDtypeStruct((n, MEM_TOKENS, D_MODEL), F32),
        compiler_params=_params(1),
        name="mem_kv",
    )(mem, w)


def _softmax_pv(s, v, sink):
    mx = jnp.max(s, axis=-1, keepdims=True)
    if sink is not None:
        mx = jnp.maximum(mx, sink)
    e = jnp.exp(s - mx)
    den = jnp.sum(e, axis=-1, keepdims=True)
    if sink is not None:
        den = den + jnp.exp(sink - mx)
    return _dot(e.astype(BF16), v) / den


def _attn_chunk(qa, qb, ka, va, kb, vb, bias_a_ref, sink_ref, bias_b_ref):
    outs = []
    for g in range(A_KV_HEADS):
        heads = range(g * A_GROUP, (g + 1) * A_GROUP)
        qs = jnp.concatenate([qa[:, h * HEAD_DIM:(h + 1) * HEAD_DIM] for h in heads], axis=0)
        s = _dot_nt(qs, ka[:, g * HEAD_DIM:(g + 1) * HEAD_DIM]) + bias_a_ref[g]
        o = _softmax_pv(s, va[:, g * HEAD_DIM:(g + 1) * HEAD_DIM], sink_ref[g])
        outs += [o[j * CHUNK:(j + 1) * CHUNK] for j in range(A_GROUP)]
    for h in range(B_HEADS):
        cols = slice(h * HEAD_DIM, (h + 1) * HEAD_DIM)
        s = _dot_nt(qb[:, cols], kb[:, cols]) + bias_b_ref[h]
        outs.append(_softmax_pv(s, vb[:, cols], None))
    return jnp.concatenate(outs, axis=1).astype(BF16)


P_TILE = 4 * CHUNK
P_SUB = 2 * CHUNK
PB_KEYS = B_PREV + P_TILE
PA_KEYS = A_PREV + P_SUB


def _softmax_pv_blocks(scores, values, sink):
    mx = functools.reduce(jnp.maximum, scores)
    mx = jnp.max(mx, axis=-1, keepdims=True)
    if sink is not None:
        mx = jnp.maximum(mx, sink)
    out = None
    for s, v in zip(scores, values):
        part = _dot(jnp.exp(s - mx).astype(BF16), v)
        out = part if out is None else out + part
    den = out[:, 128:256]
    if sink is not None:
        den = den + jnp.exp(sink - mx)
    return out[:, 0:128] / den


def _attn_prompt_kernel(qa_ref, qb_ref, kt2_ref, kt1_ref, kt0_ref, vb2_ref, vb1_ref, vb0_ref, kat_ref, katp_ref,
                        kva_ref, kvap_ref, bias_a_ref, sink_ref, bias_b_ref, o_ref):
    zero = jnp.zeros((), BF16)

    def low(n):
        return lax.broadcasted_iota(jnp.int32, (n, 128), 1) < HEAD_DIM

    va_cur = kva_ref[:, A_KVW:2 * A_KVW]
    va_prev = kvap_ref[:, A_KVW:2 * A_KVW]
    for sb in range(P_TILE // P_SUB):
        q_rows = slice(sb * P_SUB, (sb + 1) * P_SUB)
        if sb == 0:
            blocks = [(katp_ref[...], va_prev, 0), (kat_ref[:, 0:P_SUB], va_cur[0:P_SUB], A_PREV)]
        else:
            blocks = [(kat_ref[:, sb * P_SUB - A_PREV:(sb + 1) * P_SUB], va_cur[sb * P_SUB - A_PREV:(sb + 1) * P_SUB], 0)]
        for g in range(A_KV_HEADS):
            heads = range(g * A_GROUP, (g + 1) * A_GROUP)
            hd = slice(g * HEAD_DIM, (g + 1) * HEAD_DIM)
            qs = jnp.concatenate([qa_ref[q_rows, h * HEAD_DIM:(h + 1) * HEAD_DIM] for h in heads], axis=0)
            scores, values = [], []
            for kt, vv, c0 in blocks:
                n = vv.shape[0]
                scores.append(_dot(qs, kt[hd, :]) + bias_a_ref[sb, g, :, c0:c0 + n])
                swapped = jnp.concatenate([vv[:, HEAD_DIM:], vv[:, :HEAD_DIM]], axis=1)
                both = jnp.where(low(n), vv, swapped) if g == 0 else jnp.where(low(n), swapped, vv)
                values.append(jnp.concatenate([both, jnp.ones((n, 128), BF16)], axis=1))
            o = _softmax_pv_blocks(scores, values, sink_ref[g])
            for j in range(0, A_GROUP, 2):
                pair = jnp.where(low(P_SUB), o[j * P_SUB:(j + 1) * P_SUB], o[(j + 1) * P_SUB:(j + 2) * P_SUB])
                c0 = (g * A_GROUP + j) * HEAD_DIM
                o_ref[q_rows, c0:c0 + 2 * HEAD_DIM] = pair.astype(BF16)

    kv_blocks = ((kt2_ref, vb2_ref), (kt1_ref, vb1_ref), (kt0_ref, vb0_ref))
    for h in range(0, B_HEADS, 2):
        pair_cols = slice(h * HEAD_DIM, (h + 2) * HEAD_DIM)
        v_pairs = [v_ref[:, pair_cols] for _, v_ref in kv_blocks]
        outs = []
        for hh in (h, h + 1):
            cols = slice(hh * HEAD_DIM, (hh + 1) * HEAD_DIM)
            q = qb_ref[:, cols]
            scores = [_dot(q, k_ref[cols, :]) + bias_b_ref[hh, :, j * P_TILE:(j + 1) * P_TILE]
                      for j, (k_ref, _) in enumerate(kv_blocks)]
            own = [jnp.where(low(P_TILE), vp, zero) if hh == h else jnp.where(low(P_TILE), zero, vp) for vp in v_pairs]
            values = [jnp.concatenate([vp, jnp.ones((P_TILE, 128), BF16)], axis=1) for vp in own]
            outs.append(_softmax_pv_blocks(scores, values, None))
        o_ref[:, A_QW + h * HEAD_DIM:A_QW + (h + 2) * HEAD_DIM] = (outs[0] + outs[1]).astype(BF16)


def _attn_prompt_call(proj, kbt, kat, bias_a, sink, bias_b):
    t = proj.shape[0]
    back2 = lambda i: jnp.maximum(i - 2, 0)
    back1 = lambda i: jnp.maximum(i - 1, 0)
    prev_a = lambda i: jnp.maximum(i * (P_TILE // A_PREV) - 1, 0)
    in_specs = [
        pl.BlockSpec((P_TILE, 512), lambda i: (i, 0)),
        pl.BlockSpec((P_TILE, 512), lambda i: (i, 1)),
        pl.BlockSpec((B_W, P_TILE), lambda i: (0, back2(i))),
        pl.BlockSpec((B_W, P_TILE), lambda i: (0, back1(i))),
        pl.BlockSpec((B_W, P_TILE), lambda i: (0, i)),
        pl.BlockSpec((P_TILE, 512), lambda i: (back2(i), 3)),
        pl.BlockSpec((P_TILE, 512), lambda i: (back1(i), 3)),
        pl.BlockSpec((P_TILE, 512), lambda i: (i, 3)),
        pl.BlockSpec((A_KVW, P_TILE), lambda i: (0, i)),
        pl.BlockSpec((A_KVW, A_PREV), lambda i: (0, prev_a(i))),
        pl.BlockSpec((P_TILE, 256), lambda i: (i, 8)),
        pl.BlockSpec((A_PREV, 256), lambda i: (prev_a(i), 8)),
        pl.BlockSpec((None,) + bias_a.shape[1:], lambda i: (jnp.minimum(i, 1), 0, 0, 0, 0)),
        _const_spec(sink.shape),
        pl.BlockSpec((None,) + bias_b.shape[1:], lambda i: (jnp.minimum(i, 2), 0, 0, 0)),
    ]
    return pl.pallas_call(
        _attn_prompt_kernel,
        grid=(t // P_TILE,),
        in_specs=in_specs,
        out_specs=pl.BlockSpec((P_TILE, D_MODEL), lambda i: (i, 0)),
        out_shape=jax.ShapeDtypeStruct((t, D_MODEL), BF16),
        compiler_params=_params(1),
        name="attn_prompt",
    )(proj, proj, kbt, kbt, kbt, proj, proj, proj, kat, kat, proj, proj, bias_a, sink, bias_b)


def _attn_sample_kernel(qa_ref, qb_ref, kb_ref, vb_ref, kva_ref, cak_ref, cav_ref, cbk_ref, cbv_ref,
                        bias_a_ref, sink_ref, bias_b_ref, o_ref, *, nb):
    for b in range(nb):
        rows = slice(b * CHUNK, (b + 1) * CHUNK)
        ka = jnp.concatenate([cak_ref[b], kva_ref[rows, 0:A_KVW]], axis=0)
        va = jnp.concatenate([cav_ref[b], kva_ref[rows, A_KVW:2 * A_KVW]], axis=0)
        kb = jnp.concatenate([cbk_ref[b], kb_ref[rows, :]], axis=0)
        vb = jnp.concatenate([cbv_ref[b], vb_ref[rows, :]], axis=0)
        o_ref[rows, :] = _attn_chunk(qa_ref[rows, :], qb_ref[rows, :], ka, va, kb, vb,
                                     bias_a_ref, sink_ref, bias_b_ref)


def _attn_sample_call(proj, ca_k, ca_v, cb_k, cb_v, e, bias_a, sink, bias_b, nb):
    t = proj.shape[0]
    nbatch = t // CHUNK
    tm = nb * CHUNK
    in_specs = [
        pl.BlockSpec((tm, 512), lambda i: (i, 0)),
        pl.BlockSpec((tm, 512), lambda i: (i, 1)),
        pl.BlockSpec((tm, 512), lambda i: (i, 2)),
        pl.BlockSpec((tm, 512), lambda i: (i, 3)),
        pl.BlockSpec((tm, 256), lambda i: (i, 8)),
        pl.BlockSpec((None, nb, A_PREV, A_KVW), lambda i: (e, i, 0, 0)),
        pl.BlockSpec((None, nb, A_PREV, A_KVW), lambda i: (e, i, 0, 0)),
        pl.BlockSpec((None, nb, B_PREV, B_W), lambda i: (e, i, 0, 0)),
        pl.BlockSpec((None, nb, B_PREV, B_W), lambda i: (e, i, 0, 0)),
        _const_spec(bias_a.shape), _const_spec(sink.shape), _const_spec(bias_b.shape),
    ]
    return pl.pallas_call(
        functools.partial(_attn_sample_kernel, nb=nb),
        grid=(nbatch // nb,),
        in_specs=in_specs,
        out_specs=pl.BlockSpec((tm, D_MODEL), lambda i: (i, 0)),
        out_shape=jax.ShapeDtypeStruct((t, D_MODEL), BF16),
        compiler_params=_params(1),
        name="attn_sample",
    )(proj, proj, proj, proj, proj, ca_k, ca_v, cb_k, cb_v, bias_a, sink, bias_b)


M_PAIRS = M_HEADS // 2
M_QK_PAIR = 2 * M_DK
M_V_PAIR = 2 * M_DV


def _split3(x):
    hi = x.astype(BF16)
    r1 = x - hi.astype(F32)
    mid = r1.astype(BF16)
    return hi, mid, (r1 - mid.astype(F32)).astype(BF16)


def _widen_heads(x):
    first = lax.broadcasted_iota(jnp.int32, (x.shape[0], M_QK_PAIR), 1) < M_DK
    outs = []
    for p in range(M_PAIRS):
        src = x[:, p * M_QK_PAIR:(p + 1) * M_QK_PAIR]
        swapped = pltpu.roll(src, M_DK, 1)
        outs += [jnp.where(first, src, swapped), jnp.where(first, swapped, src)]
    return jnp.concatenate(outs, axis=1)


def _mlstm_chunk(q, k, v, o, gates, c_s, n_s, m_s):
    L = CHUNK
    wide = (L, M_QKW)
    row = lax.broadcasted_iota(jnp.int32, wide, 0)
    pos = lax.broadcasted_iota(jnp.int32, wide, 1) % M_DK
    ig = gates[:, 0:M_QKW]
    fg = gates[:, M_QKW:2 * M_QKW]
    lf = jnp.minimum(fg, 0.0) - jnp.log(1.0 + jnp.exp(-jnp.abs(fg)))

    t_row = lax.broadcasted_iota(jnp.int32, (L, L), 0)
    t_col = lax.broadcasted_iota(jnp.int32, (L, L), 1)
    tril = jnp.where(t_col <= t_row, 1.0, 0.0).astype(BF16)
    hi, mid, lo = _split3(lf)
    b_c = _dot(tril, hi) + _dot(tril, mid) + _dot(tril, lo)
    r_c = ig - b_c
    r_row = jnp.sum(jnp.where(row == pos, r_c, 0.0), axis=0, keepdims=True)

    cm = r_c
    for sh in (1, 2, 4, 8, 16, 32):
        cm = jnp.maximum(cm, jnp.where(row >= sh, pltpu.roll(cm, sh, 0), NEG_INF))
    m_prev = m_s[...]
    mm = jnp.maximum(m_prev, cm)
    w = jnp.exp(jnp.where(pos <= row, r_row - mm, NEG_INF))
    g_inter = jnp.exp(m_prev - mm)
    qf = q.astype(F32)
    qg = g_inter * qf
    n_prev = n_s[...]

    lane_qk = lax.broadcasted_iota(jnp.int32, (L, M_QK_PAIR), 1)
    lane_v = lax.broadcasted_iota(jnp.int32, (L, M_V_PAIR), 1)
    zero_b = jnp.zeros((), BF16)
    bd_r = lax.broadcasted_iota(jnp.int32, (M_QK_PAIR, M_V_PAIR), 0) < M_DK
    bd_c = lax.broadcasted_iota(jnp.int32, (M_QK_PAIR, M_V_PAIR), 1) < M_DV
    on_diag = bd_r == bd_c
    ones_bd = jnp.where(on_diag, 1.0, 0.0).astype(BF16)

    s_parts, k_bd, v_bd = [], [], []
    for p in range(M_PAIRS):
        qk_l = slice(p * M_QK_PAIR, (p + 1) * M_QK_PAIR)
        v_l = slice(p * M_V_PAIR, (p + 1) * M_V_PAIR)
        kp, vp = k[:, qk_l], v[:, v_l]
        k_bd.append(jnp.concatenate([jnp.where(lane_qk < M_DK, kp, zero_b),
                                     jnp.where(lane_qk >= M_DK, kp, zero_b)], axis=0))
        v_bd.append(jnp.concatenate([jnp.where(lane_v < M_DV, vp, zero_b),
                                     jnp.where(lane_v >= M_DV, vp, zero_b)], axis=0))
        s_parts.append(_dot_nt(q[:, qk_l], k_bd[p]))
    sc = jnp.concatenate(s_parts, axis=1) * w
    den_in = sc + qg * n_prev
    d_hi, d_mid, _ = _split3(den_in)
    sc_b = sc.astype(BF16)
    qg_b = qg.astype(BF16)

    nums, dens = [], []
    for p in range(M_PAIRS):
        qk_l = slice(p * M_QK_PAIR, (p + 1) * M_QK_PAIR)
        lhs = jnp.concatenate([sc_b[:, qk_l], qg_b[:, qk_l]], axis=1)
        rhs = jnp.concatenate([v_bd[p], c_s[p].astype(BF16)], axis=0)
        nums.append(_dot(lhs, rhs))
        dens.append(_dot(d_hi[:, qk_l], ones_bd) + _dot(d_mid[:, qk_l], ones_bd))
    num = jnp.concatenate(nums, axis=1)
    den = jnp.concatenate(dens, axis=1)
    floor = _widen_heads(jnp.exp(-b_c - mm))
    hs = num / jnp.maximum(jnp.abs(den), floor)

    b_last = b_c[L - 1:L, :]
    mm_last = mm[L - 1:L, :]
    decay = jnp.exp(m_prev - mm_last)
    wk = jnp.exp(r_c - mm_last) * k.astype(F32)
    n_s[...] = decay * n_prev + jnp.sum(wk, axis=0, keepdims=True)
    m_s[...] = b_last + mm_last
    decay_v = _widen_heads(jnp.broadcast_to(decay, (8, M_QKW)))[0:1]
    wk_b = wk.astype(BF16)
    for p in range(M_PAIRS):
        qk_l = slice(p * M_QK_PAIR, (p + 1) * M_QK_PAIR)
        v_l = slice(p * M_V_PAIR, (p + 1) * M_V_PAIR)
        upd = _dot_tn(wk_b[:, qk_l], v[:, v_l])
        c_s[p] = decay_v[:, v_l] * c_s[p] + jnp.where(on_diag, upd, 0.0)

    of = o.astype(F32)
    return (hs * (1.0 / (1.0 + jnp.exp(-of)))).astype(BF16)


def _mlstm_kernel(q_ref, k_ref, v_ref, o_ref, gates_ref, bias_ref, c_in, n_in, m_in,
                  hg_ref, c_out, n_out, m_out, c_s, n_s, m_s, *, chunks):
    j = pl.program_id(1)

    @pl.when(j == 0)
    def _():
        c_s[...] = c_in[0]
        n_s[...] = n_in[0]
        m_s[...] = m_in[0]

    for c in range(chunks):
        rows = slice(c * CHUNK, (c + 1) * CHUNK)
        hg_ref[rows, :] = _mlstm_chunk(q_ref[rows, :], k_ref[rows, :], v_ref[rows, :], o_ref[rows, :],
                                       gates_ref[rows, :] + bias_ref[...], c_s, n_s, m_s)

    @pl.when(j == pl.num_programs(1) - 1)
    def _():
        c_out[0] = c_s[...]
        n_out[0] = n_s[...]
        m_out[0] = m_s[...]


def _mlstm_call(proj, gates, bias, c0, n0, m0, layer, chunks_per_step):
    t = proj.shape[0]
    nseq = c0.shape[1]
    rows = chunks_per_step * CHUNK
    nsteps = t // (nseq * rows)
    blk = lambda s, j: s * nsteps + j
    c_blk = (1, M_PAIRS, M_QK_PAIR, M_V_PAIR)
    r_blk = (1, 1, M_QKW)
    in_specs = [
        pl.BlockSpec((rows, M_QKW), lambda s, j: (blk(s, j), 0)),
        pl.BlockSpec((rows, M_QKW), lambda s, j: (blk(s, j), 1)),
        pl.BlockSpec((rows, M_VW), lambda s, j: (blk(s, j), 1)),
        pl.BlockSpec((rows, M_VW), lambda s, j: (blk(s, j), 2)),
        pl.BlockSpec((rows, 2 * M_QKW), lambda s, j: (blk(s, j), 0)),
        _const_spec((1, 2 * M_QKW)),
        pl.BlockSpec((None,) + c_blk, lambda s, j: (layer, s, 0, 0, 0)),
        pl.BlockSpec((None,) + r_blk, lambda s, j: (layer, s, 0, 0)),
        pl.BlockSpec((None,) + r_blk, lambda s, j: (layer, s, 0, 0)),
    ]
    out_specs = (
        pl.BlockSpec((rows, M_VW), lambda s, j: (blk(s, j), 0)),
        pl.BlockSpec(c_blk, lambda s, j: (s, 0, 0, 0)),
        pl.BlockSpec(r_blk, lambda s, j: (s, 0, 0)),
        pl.BlockSpec(r_blk, lambda s, j: (s, 0, 0)),
    )
    out_shape = (
        jax.ShapeDtypeStruct((t, M_VW), BF16),
        jax.ShapeDtypeStruct((nseq,) + c_blk[1:], F32),
        jax.ShapeDtypeStruct((nseq,) + r_blk[1:], F32),
        jax.ShapeDtypeStruct((nseq,) + r_blk[1:], F32),
    )
    return pl.pallas_call(
        functools.partial(_mlstm_kernel, chunks=chunks_per_step),
        grid=(nseq, nsteps),
        in_specs=in_specs,
        out_specs=out_specs,
        out_shape=out_shape,
        scratch_shapes=[pltpu.VMEM(c_blk[1:], F32), pltpu.VMEM(r_blk[1:], F32), pltpu.VMEM(r_blk[1:], F32)],
        compiler_params=_params(2),
        name="mlstm",
    )(proj, proj, proj, proj, gates, bias, c0, n0, m0)


def _mlstm_state_to_kernel(c, n, m):
    lead = c.shape[:-3]
    cp = c.reshape(lead + (M_PAIRS, 2, M_DK, M_DV))
    z = jnp.zeros_like(cp[..., 0, :, :])
    top = jnp.concatenate([cp[..., 0, :, :], z], axis=-1)
    bot = jnp.concatenate([z, cp[..., 1, :, :]], axis=-1)
    c_bd = jnp.concatenate([top, bot], axis=-2)
    n_row = n.reshape(lead + (1, M_QKW))
    m_row = jnp.repeat(m, M_DK, axis=-1).reshape(lead + (1, M_QKW))
    return c_bd, n_row, m_row


def _mlstm_state_from_kernel(c_bd, n_row, m_row):
    lead = c_bd.shape[:-3]
    top = c_bd[..., :M_DK, :M_DV]
    bot = c_bd[..., M_DK:, M_DV:]
    c = jnp.stack([top, bot], axis=-3).reshape(lead + (M_HEADS, M_DK, M_DV))
    n = n_row.reshape(lead + (M_HEADS, M_DK))
    m = m_row.reshape(lead + (M_HEADS, M_DK))[..., 0]
    return c, n, m


def _post_kernel(x_ref, a_ref, wout_ref, g_ref, wq_ref, mk_ref, mv_ref, wo_ref, o_ref, att_s, *, nb, tb):
    x1 = x_ref[...] + _dot(a_ref[...], wout_ref[...])
    q = _dot(_rms(x1, g_ref[...]).astype(BF16), wq_ref[...]).astype(BF16)
    scale = MEM_HD ** -0.5
    for b in range(nb):
        rows = slice(b * tb, (b + 1) * tb)
        for h in range(MEM_HEADS):
            cols = slice(h * MEM_HD, (h + 1) * MEM_HD)
            s = _dot_nt(q[rows, cols], mk_ref[b, :, cols]) * scale
            e = jnp.exp(s - jnp.max(s, axis=-1, keepdims=True))
            p = (e / jnp.sum(e, axis=-1, keepdims=True)).astype(BF16)
            att_s[rows, cols] = _dot(p, mv_ref[b, :, cols]).astype(BF16)
    o_ref[...] = x1 + _dot(att_s[...], wo_ref[...])


def _post_call(x, a, w_out, wl, g, w_q, mk, mv, w_o, layer, tm, tb):
    t = x.shape[0]
    nb = tm // tb
    shared = mk.shape[1] == 1
    mem_map = (lambda i: (layer, 0, 0, 0)) if shared else (lambda i: (layer, i, 0, 0))
    row = lambda i: (i, 0)
    in_specs = [
        pl.BlockSpec((tm, D_MODEL), row),
        pl.BlockSpec((tm, a.shape[1]), row),
        _layer_spec(w_out, wl),
        _layer_spec(g, layer),
        _layer_spec(w_q, layer),
        pl.BlockSpec((None, nb, MEM_TOKENS, D_MODEL), mem_map),
        pl.BlockSpec((None, nb, MEM_TOKENS, D_MODEL), mem_map),
        _layer_spec(w_o, layer),
    ]
    return pl.pallas_call(
        functools.partial(_post_kernel, nb=nb, tb=tb),
        grid=(t // tm,),
        in_specs=in_specs,
        out_specs=pl.BlockSpec((tm, D_MODEL), row),
        out_shape=jax.ShapeDtypeStruct((t, D_MODEL), F32),
        scratch_shapes=[pltpu.VMEM((tm, D_MODEL), BF16)],
        compiler_params=_params(1),
        name="post_cross",
    )(x, a, w_out, g, w_q, mk, mv, w_o)


def _conv3(u, cw, hists, seq_rows):
    w0, w1, w2 = cw[0:1], cw[1:2], cw[2:3]
    c = pltpu.roll(u, 2, 0) * w0 + pltpu.roll(u, 1, 0) * w1 + u * w2
    row = lax.broadcasted_iota(jnp.int32, (8, u.shape[1]), 0)
    pieces = []
    for s, hist in enumerate(hists):
        base = s * seq_rows
        t8 = u[base:base + 8]
        h0, h1 = hist[0:1], hist[1:2]
        u1 = jnp.where(row == 0, h1, pltpu.roll(t8, 1, 0))
        u2 = jnp.where(row == 0, h0, jnp.where(row == 1, h1, pltpu.roll(t8, 2, 0)))
        pieces += [u2 * w0 + u1 * w1 + t8 * w2, c[base + 8:base + seq_rows]]
    return jnp.concatenate(pieces, axis=0)


def _ffn_kernel(*refs, seq_rows, carry, final_norm):
    x_ref, g_ref, win_ref, cw_ref, wout_ref = refs[:5]
    pos = 5
    hist_ref = None
    if not carry:
        hist_ref = refs[pos]
        pos += 1
    gfin_ref = None
    if final_norm:
        gfin_ref = refs[pos]
        pos += 1
    o_ref, cs_ref, act_s = refs[pos:pos + 3]
    hist_s = refs[pos + 3] if carry else None

    tm = x_ref.shape[0]
    nseq = tm // seq_rows
    x = x_ref[...]
    h = _rms(x, g_ref[...]).astype(BF16)

    if carry:
        @pl.when(pl.program_id(0) == 0)
        def _():
            hist_s[...] = jnp.zeros_like(hist_s)

    for j in range(D_FF // FF_CHUNK):
        halves = []
        for half in range(2):
            c0 = half * D_FF + j * FF_CHUNK
            cols = slice(c0, c0 + FF_CHUNK)
            u = _dot(h, win_ref[:, cols])
            if carry:
                hists = [hist_s[:, cols]]
            else:
                hists = [hist_ref[s, :, cols] for s in range(nseq)]
            halves.append(_conv3(u, cw_ref[:, cols], hists, seq_rows))
            for s in range(nseq):
                last2 = u[(s + 1) * seq_rows - 2:(s + 1) * seq_rows]
                if carry:
                    hist_s[:, cols] = last2
                    cs_ref[:, cols] = last2
                else:
                    cs_ref[s, :, cols] = last2
        ca, cg = halves
        act = ca * (1.0 / (1.0 + jnp.exp(-ca))) * cg
        act_s[:, j * FF_CHUNK:(j + 1) * FF_CHUNK] = act.astype(BF16)

    y = x + _dot(act_s[...], wout_ref[...])
    if final_norm:
        y = _rms(y, gfin_ref[...])
    o_ref[...] = y


def _ffn_call(x, g, w_in, conv_w, w_out, layer, hist, g_final, tm):
    t = x.shape[0]
    carry = hist is None
    seq_rows = tm if carry else CHUNK
    nseq = tm // seq_rows
    row = lambda i: (i, 0)
    args = [x, g, w_in, conv_w, w_out]
    in_specs = [pl.BlockSpec((tm, D_MODEL), row), _layer_spec(g, layer),
                _layer_spec(w_in, layer), _layer_spec(conv_w, layer), _layer_spec(w_out, layer)]
    if carry:
        cs_shape, cs_spec = (2, 2 * D_FF), _const_spec((2, 2 * D_FF))
    else:
        args.append(hist)
        in_specs.append(pl.BlockSpec((None, nseq, 2, 2 * D_FF), lambda i: (layer, i, 0, 0)))
        cs_shape, cs_spec = hist.shape[1:], pl.BlockSpec((nseq, 2, 2 * D_FF), lambda i: (i, 0, 0))
    if g_final is not None:
        args.append(g_final.reshape(1, D_MODEL))
        in_specs.append(_const_spec((1, D_MODEL)))
    scratch = [pltpu.VMEM((tm, D_FF), BF16)]
    if carry:
        scratch.append(pltpu.VMEM((2, 2 * D_FF), F32))
    return pl.pallas_call(
        functools.partial(_ffn_kernel, seq_rows=seq_rows, carry=carry, final_norm=g_final is not None),
        grid=(t // tm,),
        in_specs=in_specs,
        out_specs=(pl.BlockSpec((tm, D_MODEL), row), cs_spec),
        out_shape=(jax.ShapeDtypeStruct((t, D_MODEL), F32), jax.ShapeDtypeStruct(cs_shape, F32)),
        scratch_shapes=scratch,
        compiler_params=_params(1),
        name="conv_ffn",
    )(*args)


def _alibi(n_q, n_k):
    q = np.arange(n_q)[:, None]
    dist = np.abs(q + A_PREV - np.arange(n_k)[None, :]).astype(np.float32)
    slopes = (2.0 ** (-8.0 * np.arange(1, A_HEADS + 1) / A_HEADS)).astype(np.float32)
    return -slopes[:, None, None] * dist[None]


def _band_mask(n_q, n_k, n_prev_chunks, first_valid_chunk):
    qc = (np.arange(n_q) // CHUNK)[:, None]
    kc = (np.arange(n_k) // CHUNK)[None, :]
    ok = (kc >= qc) & (kc <= qc + n_prev_chunks) & (kc >= first_valid_chunk)
    return np.where(ok, 0.0, NEG_INF).astype(np.float32)


def _alibi_sample_table():
    return jnp.asarray(_alibi(CHUNK, A_BAND).reshape(A_KV_HEADS, A_GROUP * CHUNK, A_BAND))


def _alibi_prompt_tables():
    n_prev = A_PREV // CHUNK
    base = _alibi(P_SUB, PA_KEYS)
    general = (base + _band_mask(P_SUB, PA_KEYS, n_prev, 0)).reshape(A_KV_HEADS, A_GROUP * P_SUB, PA_KEYS)
    first = (base + _band_mask(P_SUB, PA_KEYS, n_prev, n_prev)).reshape(A_KV_HEADS, A_GROUP * P_SUB, PA_KEYS)
    return jnp.asarray(np.stack([np.stack([first, general]), np.stack([general, general])]))


def _relpos_toeplitz(table, n_q, n_k):
    h, n_rel = table.shape
    n_lo = n_k - B_BAND
    n_hi = n_q + n_k - 1 - n_lo - n_rel
    ext = jnp.concatenate([jnp.broadcast_to(table[:, :1], (h, n_lo + 1)), table,
                           jnp.broadcast_to(table[:, -1:], (h, n_hi))], axis=1)
    rev = ext[:, ::-1][:, None, :]
    length = n_q + n_k - 1
    tiled = jnp.broadcast_to(rev, (h, n_q, length + 1)).reshape(h, n_q * (length + 1))
    skew = tiled[:, :n_q * length].reshape(h, n_q, length)
    return skew[:, :, n_q - 1:n_q - 1 + n_k]


def _relpos_prompt_tables(table):
    n_prev = B_PREV // CHUNK
    masks = np.stack([_band_mask(P_TILE, PB_KEYS, n_prev, max(n_prev - t * (P_TILE // CHUNK), 0)) for t in range(3)])
    return _relpos_toeplitz(table, P_TILE, PB_KEYS)[None] + jnp.asarray(masks)[:, None]


def _row_tile(t):
    return PROMPT_TILE if t % PROMPT_TILE == 0 else 256


def kernel(x_prompt, x_sample, mem_prompt, cache_a_k, cache_a_v, cache_b_k, cache_b_v, state_mlstm_c, state_mlstm_n, state_mlstm_m, cache_mem_k, cache_mem_v, state_ffn_conv, norm_mix, norm_cross, norm_ffn, norm_final, w_in_att, w_out_att, sink_a, relpos_b, w_in_mlstm, b_igate, b_fgate, w_out_mlstm, w_mem_q, w_mem_k, w_mem_v, w_mem_o, w_ffn_in, conv_ffn, w_ffn_out):
    depth = norm_mix.shape[0]
    assert x_prompt.shape[0] == 1
    tp = x_prompt.shape[1]
    nb, ds = x_sample.shape[0], x_sample.shape[1]
    assert ds == CHUNK and tp % PROMPT_TILE == 0
    ts = nb * ds
    tm_p, tm_s = PROMPT_TILE, _row_tile(ts)
    xp = x_prompt[0]
    xs = x_sample.reshape(ts, D_MODEL)

    memkv = _memkv_call(mem_prompt[0], jnp.concatenate([w_mem_k, w_mem_v], axis=0).astype(BF16))
    p_mk, p_mv = memkv[:depth][:, None].astype(BF16), memkv[depth:][:, None].astype(BF16)
    s_mk = cache_mem_k.reshape(depth, nb, MEM_TOKENS, D_MODEL).astype(BF16)
    s_mv = cache_mem_v.reshape(depth, nb, MEM_TOKENS, D_MODEL).astype(BF16)

    g_mix = norm_mix.astype(F32)[:, None]
    g_cross = norm_cross.astype(F32)[:, None]
    g_ffn = norm_ffn.astype(F32)[:, None]
    wa = w_in_att
    scale = HEAD_DIM ** -0.5
    w_att = jnp.concatenate([wa[:, :, 0:512] * scale, wa[:, :, 768:1280] * scale, wa[:, :, 1280:2304],
                             wa[:, :, 512:768]], axis=2).astype(BF16)
    wm = w_in_mlstm
    k_scale = M_DK ** -0.5
    w_mls = jnp.concatenate([wm[:, :, 0:M_QKW], wm[:, :, M_QKW:2 * M_QKW] * k_scale, wm[:, :, 2 * M_QKW:M_MAIN],
                             jnp.repeat(wm[:, :, M_MAIN:], M_DK, axis=2)], axis=2).astype(BF16)
    w_out_a, w_out_m = w_out_att.astype(BF16), w_out_mlstm.astype(BF16)
    w_q, w_o = w_mem_q.astype(BF16), w_mem_o.astype(BF16)
    w_fi, w_fo = w_ffn_in.astype(BF16), w_ffn_out.astype(BF16)
    conv_w = conv_ffn.astype(F32)
    hist_s = state_ffn_conv.astype(F32)
    n_even = cache_a_k.shape[0]
    ca_k = cache_a_k.reshape(n_even, nb, A_PREV, A_KVW).astype(BF16)
    ca_v = cache_a_v.reshape(n_even, nb, A_PREV, A_KVW).astype(BF16)
    cb_k = cache_b_k.reshape(n_even, nb, B_PREV, B_W).astype(BF16)
    cb_v = cache_b_v.reshape(n_even, nb, B_PREV, B_W).astype(BF16)
    st_c, st_n, st_m = _mlstm_state_to_kernel(state_mlstm_c.astype(F32), state_mlstm_n.astype(F32),
                                              state_mlstm_m.astype(F32))
    zc = jnp.zeros((1, 1, M_PAIRS, M_QK_PAIR, M_V_PAIR), F32)
    zn = jnp.zeros((1, 1, 1, M_QKW), F32)
    zm = jnp.zeros((1, 1, 1, M_QKW), F32)

    bias_a_s = _alibi_sample_table()
    bias_a_p = _alibi_prompt_tables()
    post_tm_s = tm_s
    attn_nb = 8 if nb % 8 == 0 else 4

    p_ak, p_av, p_bk, p_bv, p_c, p_n, p_m, p_conv = [], [], [], [], [], [], [], []
    s_ak, s_av, s_bk, s_bv, s_c, s_n, s_m, s_conv = [], [], [], [], [], [], [], []
    for l in range(depth):
        if l % 2 == 0:
            e = l // 2
            proj_p, kv_p, kbt_p, kat_p = _att_proj(xp, g_mix, l, w_att, e, tm_p, prompt=True)
            proj_s, kv_s = _att_proj(xs, g_mix, l, w_att, e, tm_s, prompt=False)
            sk = sink_a[e].astype(F32)
            sink_p = jnp.repeat(sk, P_SUB).reshape(A_KV_HEADS, A_GROUP * P_SUB, 1)
            sink_s = jnp.repeat(sk, CHUNK).reshape(A_KV_HEADS, A_GROUP * CHUNK, 1)
            rel = relpos_b[e].astype(F32)
            mix_p = _attn_prompt_call(proj_p, kbt_p, kat_p, bias_a_p, sink_p, _relpos_prompt_tables(rel))
            mix_s = _attn_sample_call(proj_s, ca_k, ca_v, cb_k, cb_v, e, bias_a_s, sink_s,
                                      _relpos_toeplitz(rel, CHUNK, B_BAND), attn_nb)
            w_mix_out, wl = w_out_a, e
            p_ak.append(kv_p[tm_p - A_PREV:, 0:128].reshape(1, A_PREV, A_KV_HEADS, HEAD_DIM))
            p_av.append(kv_p[tm_p - A_PREV:, 128:256].reshape(1, A_PREV, A_KV_HEADS, HEAD_DIM))
            p_bk.append(kv_p[:, 256:768].reshape(1, B_PREV, B_HEADS, HEAD_DIM))
            p_bv.append(kv_p[:, 768:1280].reshape(1, B_PREV, B_HEADS, HEAD_DIM))
            s_ak.append(kv_s[:, 0:128].reshape(nb, ds, A_KV_HEADS, HEAD_DIM))
            s_av.append(kv_s[:, 128:256].reshape(nb, ds, A_KV_HEADS, HEAD_DIM))
            s_bk.append(kv_s[:, 256:768].reshape(nb, ds, B_HEADS, HEAD_DIM))
            s_bv.append(kv_s[:, 768:1280].reshape(nb, ds, B_HEADS, HEAD_DIM))
        else:
            o = l // 2
            gate_bias = jnp.concatenate([jnp.repeat(b_igate[o].astype(F32), M_DK),
                                         jnp.repeat(b_fgate[o].astype(F32), M_DK)]).reshape(1, M_GATES_W)
            proj_p, gates_p = _mlstm_proj(xp, g_mix, l, w_mls, o, tm_p)
            proj_s, gates_s = _mlstm_proj(xs, g_mix, l, w_mls, o, tm_s)
            mix_p, c, n, m = _mlstm_call(proj_p, gates_p, gate_bias, zc, zn, zm, 0, 4)
            c, n, m = _mlstm_state_from_kernel(c, n, m)
            p_c.append(c); p_n.append(n); p_m.append(m)
            mix_s, c, n, m = _mlstm_call(proj_s, gates_s, gate_bias, st_c, st_n, st_m, o, 1)
            c, n, m = _mlstm_state_from_kernel(c, n, m)
            s_c.append(c); s_n.append(n); s_m.append(m)
            w_mix_out, wl = w_out_m, o

        xp = _post_call(xp, mix_p, w_mix_out, wl, g_cross, w_q, p_mk, p_mv, w_o, l, tm_p, tm_p)
        xs = _post_call(xs, mix_s, w_mix_out, wl, g_cross, w_q, s_mk, s_mv, w_o, l, post_tm_s, CHUNK)

        g_fin = norm_final if l == depth - 1 else None
        xp, cv = _ffn_call(xp, g_ffn, w_fi, conv_w, w_fo, l, None, g_fin, tm_p)
        p_conv.append(cv[None])
        xs, cv = _ffn_call(xs, g_ffn, w_fi, conv_w, w_fo, l, hist_s, g_fin, tm_s)
        s_conv.append(cv)

    p_mk, p_mv = memkv[:depth], memkv[depth:]
    mem_shape = (depth, 1, MEM_TOKENS, MEM_HEADS, MEM_HD)
    return (xp[None], xs.reshape(nb, ds, D_MODEL),
            jnp.stack(p_ak), jnp.stack(p_av), jnp.stack(p_bk), jnp.stack(p_bv),
            jnp.stack(p_c), jnp.stack(p_n), jnp.stack(p_m),
            p_mk.reshape(mem_shape), p_mv.reshape(mem_shape), jnp.stack(p_conv),
            jnp.stack(s_ak), jnp.stack(s_av), jnp.stack(s_bk), jnp.stack(s_bv),
            jnp.stack(s_c), jnp.stack(s_n), jnp.stack(s_m), jnp.stack(s_conv))
```

```python
import functools

import numpy as np
import jax
import jax.numpy as jnp
from jax import lax
from jax.experimental import pallas as pl
from jax.experimental.pallas import tpu as pltpu

F32 = jnp.float32
BF16 = jnp.bfloat16

D_MODEL = 1024
CHUNK = 64
HEAD_DIM = 64
A_HEADS = 8
A_KV_HEADS = 2
A_GROUP = A_HEADS // A_KV_HEADS
A_BAND = 3 * CHUNK
A_PREV = 2 * CHUNK
B_HEADS = 8
B_BAND = 9 * CHUNK
B_PREV = 8 * CHUNK
REL_CLIP = 128
A_QW = A_HEADS * HEAD_DIM
A_KVW = A_KV_HEADS * HEAD_DIM
B_W = B_HEADS * HEAD_DIM
ATT_PROJ = A_QW + 2 * A_KVW + 3 * B_W
KV32_W = 2 * A_KVW + 2 * B_W
M_HEADS = 8
M_DK = 64
M_DV = 128
M_QKW = M_HEADS * M_DK
M_VW = M_HEADS * M_DV
M_MAIN = 2 * M_QKW + 2 * M_VW
MEM_TOKENS = 256
MEM_HEADS = 4
MEM_HD = D_MODEL // MEM_HEADS
D_FF = 2816
FF_CHUNK = 256
EPS = 1e-6
NEG_INF = float("-inf")

VMEM_LIMIT_BYTES = 56 * 1024 * 1024
PROMPT_TILE = 512


def _params(n_axes):
    return pltpu.CompilerParams(dimension_semantics=("arbitrary",) * n_axes,
                                vmem_limit_bytes=VMEM_LIMIT_BYTES)


def _const_spec(shape):
    zeros = (0,) * len(shape)
    return pl.BlockSpec(shape, lambda *_: zeros)


def _layer_spec(stack, layer):
    idx = (layer,) + (0,) * (stack.ndim - 1)
    return pl.BlockSpec((None,) + stack.shape[1:], lambda *_: idx)


def _rms(x, g):
    return x * lax.rsqrt(jnp.mean(x * x, axis=-1, keepdims=True) + EPS) * g


def _dot(a, b):
    return jnp.dot(a, b, preferred_element_type=F32)


def _dot_nt(a, b):
    return lax.dot_general(a, b, (((1,), (1,)), ((), ())), preferred_element_type=F32)


def _dot_tn(a, b):
    return lax.dot_general(a, b, (((0,), (0,)), ((), ())), preferred_element_type=F32)


def _proj_kernel(x_ref, g_ref, w_ref, *out_refs, plan, transposed):
    h = _rms(x_ref[...], g_ref[...]).astype(BF16)
    for c0, cw, dsts in plan:
        r = _dot(h, w_ref[:, c0:c0 + cw])
        for oi, oc in dsts:
            out_refs[oi][:, oc:oc + cw] = r.astype(out_refs[oi].dtype)
        if c0 in transposed:
            oi, width = transposed[c0]
            out_refs[oi][...] = r[:, :width].T.astype(out_refs[oi].dtype)


def _proj_call(x, g, gl, w, wl, plan, transposed, out_shapes, out_specs, tm, name):
    t = x.shape[0]
    return pl.pallas_call(
        functools.partial(_proj_kernel, plan=plan, transposed=transposed),
        grid=(t // tm,),
        in_specs=[pl.BlockSpec((tm, D_MODEL), lambda i: (i, 0)),
                  _layer_spec(g, gl),
                  _layer_spec(w, wl)],
        out_specs=out_specs,
        out_shape=out_shapes,
        compiler_params=_params(1),
        name=name,
    )(x, g, w)


_ATT_PLAN = (
    (0, 512, ((0, 0),)),
    (512, 512, ((0, 512),)),
    (1024, 512, ((0, 1024), (1, 256))),
    (1536, 512, ((0, 1536), (1, 768))),
    (2048, 256, ((0, 2048), (1, 0))),
)


def _att_proj(x, g, gl, w_perm, wl, tm, prompt):
    t = x.shape[0]
    shapes = [jax.ShapeDtypeStruct((t, ATT_PROJ), BF16)]
    specs = [pl.BlockSpec((tm, ATT_PROJ), lambda i: (i, 0))]
    transposed = {}
    if prompt:
        shapes += [jax.ShapeDtypeStruct((tm, KV32_W), F32), jax.ShapeDtypeStruct((B_W, t), BF16),
                   jax.ShapeDtypeStruct((A_KVW, t), BF16)]
        specs += [_const_spec((tm, KV32_W)), pl.BlockSpec((B_W, tm), lambda i: (0, i)),
                  pl.BlockSpec((A_KVW, tm), lambda i: (0, i))]
        transposed = {1024: (2, B_W), 2048: (3, A_KVW)}
    else:
        shapes.append(jax.ShapeDtypeStruct((t, KV32_W), F32))
        specs.append(pl.BlockSpec((tm, KV32_W), lambda i: (i, 0)))
    return _proj_call(x, g, gl, w_perm, wl, _ATT_PLAN, transposed, tuple(shapes), tuple(specs), tm, "att_proj")


M_GATES_W = 2 * M_QKW
_MLSTM_PLAN = (tuple((c, 512, ((0, c),)) for c in range(0, M_MAIN, 512))
               + tuple((M_MAIN + c, 512, ((1, c),)) for c in range(0, M_GATES_W, 512)))


def _mlstm_proj(x, g, gl, w_rep, wl, tm):
    t = x.shape[0]
    return _proj_call(
        x, g, gl, w_rep, wl, _MLSTM_PLAN, {},
        (jax.ShapeDtypeStruct((t, M_MAIN), BF16), jax.ShapeDtypeStruct((t, M_GATES_W), F32)),
        (pl.BlockSpec((tm, M_MAIN), lambda i: (i, 0)), pl.BlockSpec((tm, M_GATES_W), lambda i: (i, 0))),
        tm, "mlstm_proj")


def _memkv_kernel(m_ref, w_ref, o_ref):
    o_ref[0] = _dot(m_ref[...].astype(BF16), w_ref[0])


def _memkv_call(mem, w):
    n = w.shape[0]
    return pl.pallas_call(
        _memkv_kernel,
        grid=(n,),
        in_specs=[_const_spec(mem.shape), pl.BlockSpec((1, D_MODEL, D_MODEL), lambda i: (i, 0, 0))],
        out_specs=pl.BlockSpec((1, MEM_TOKENS, D_MODEL), lambda i: (i, 0, 0)),
        out_shape=jax.ShapeDtypeStruct((n, MEM_TOKENS, D_MODEL), F32),
        compiler_params=_params(1),
        name="mem_kv",
    )(mem, w)


def _low_half(n):
    return lax.broadcasted_iota(jnp.int32, (n, 128), 1) < HEAD_DIM


def _with_ones(v):
    return jnp.concatenate([v, jnp.ones((v.shape[0], 128), BF16)], axis=1)


def _one_head(v_pair, first):
    zero = jnp.zeros((), BF16)
    low = _low_half(v_pair.shape[0])
    return _with_ones(jnp.where(low, v_pair, zero) if first else jnp.where(low, zero, v_pair))


def _head_on_both_halves(v_pair, first):
    swapped = jnp.concatenate([v_pair[:, HEAD_DIM:], v_pair[:, :HEAD_DIM]], axis=1)
    low = _low_half(v_pair.shape[0])
    return _with_ones(jnp.where(low, v_pair, swapped) if first else jnp.where(low, swapped, v_pair))


def _softmax_pv_blocks(scores, values, sink):
    if all(s.shape == scores[0].shape for s in scores):
        mx = jnp.max(functools.reduce(jnp.maximum, scores), axis=-1, keepdims=True)
    else:
        mx = functools.reduce(jnp.maximum, [jnp.max(s, axis=-1, keepdims=True) for s in scores])
    if sink is not None:
        mx = jnp.maximum(mx, sink)
    out = None
    for s, v in zip(scores, values):
        part = _dot(jnp.exp(s - mx).astype(BF16), v)
        out = part if out is None else out + part
    den = out[:, 128:256]
    if sink is not None:
        den = den + jnp.exp(sink - mx)
    return out[:, 0:128] / den


P_TILE = 4 * CHUNK
P_SUB = 2 * CHUNK
PB_KEYS = B_PREV + P_TILE
PA_KEYS = A_PREV + P_SUB


def _attn_prompt_kernel(qa_ref, qb_ref, kt2_ref, kt1_ref, kt0_ref, vb2_ref, vb1_ref, vb0_ref, kat_ref, katp_ref,
                        kva_ref, kvap_ref, bias_a_ref, sink_ref, bias_b_ref, o_ref):

    va_cur = kva_ref[:, A_KVW:2 * A_KVW]
    va_prev = kvap_ref[:, A_KVW:2 * A_KVW]
    for sb in range(P_TILE // P_SUB):
        q_rows = slice(sb * P_SUB, (sb + 1) * P_SUB)
        if sb == 0:
            blocks = [(katp_ref[...], va_prev, 0), (kat_ref[:, 0:P_SUB], va_cur[0:P_SUB], A_PREV)]
        else:
            blocks = [(kat_ref[:, sb * P_SUB - A_PREV:(sb + 1) * P_SUB], va_cur[sb * P_SUB - A_PREV:(sb + 1) * P_SUB], 0)]
        for g in range(A_KV_HEADS):
            heads = range(g * A_GROUP, (g + 1) * A_GROUP)
            hd = slice(g * HEAD_DIM, (g + 1) * HEAD_DIM)
            qs = jnp.concatenate([qa_ref[q_rows, h * HEAD_DIM:(h + 1) * HEAD_DIM] for h in heads], axis=0)
            scores = [_dot(qs, kt[hd, :]) + bias_a_ref[sb, g, :, c0:c0 + vv.shape[0]] for kt, vv, c0 in blocks]
            values = [_head_on_both_halves(vv, g == 0) for _, vv, _ in blocks]
            o = _softmax_pv_blocks(scores, values, sink_ref[g])
            for j in range(0, A_GROUP, 2):
                pair = jnp.where(_low_half(P_SUB), o[j * P_SUB:(j + 1) * P_SUB], o[(j + 1) * P_SUB:(j + 2) * P_SUB])
                c0 = (g * A_GROUP + j) * HEAD_DIM
                o_ref[q_rows, c0:c0 + 2 * HEAD_DIM] = pair.astype(BF16)

    kv_blocks = ((kt2_ref, vb2_ref), (kt1_ref, vb1_ref), (kt0_ref, vb0_ref))
    for h in range(0, B_HEADS, 2):
        pair_cols = slice(h * HEAD_DIM, (h + 2) * HEAD_DIM)
        v_pairs = [v_ref[:, pair_cols] for _, v_ref in kv_blocks]
        outs = []
        for hh in (h, h + 1):
            cols = slice(hh * HEAD_DIM, (hh + 1) * HEAD_DIM)
            q = qb_ref[:, cols]
            scores = [_dot(q, k_ref[cols, :]) + bias_b_ref[hh, :, j * P_TILE:(j + 1) * P_TILE]
                      for j, (k_ref, _) in enumerate(kv_blocks)]
            values = [_one_head(vp, hh == h) for vp in v_pairs]
            outs.append(_softmax_pv_blocks(scores, values, None))
        o_ref[:, A_QW + h * HEAD_DIM:A_QW + (h + 2) * HEAD_DIM] = (outs[0] + outs[1]).astype(BF16)


def _attn_prompt_call(proj, kbt, kat, bias_a, sink, bias_b):
    t = proj.shape[0]
    back2 = lambda i: jnp.maximum(i - 2, 0)
    back1 = lambda i: jnp.maximum(i - 1, 0)
    prev_a = lambda i: jnp.maximum(i * (P_TILE // A_PREV) - 1, 0)
    in_specs = [
        pl.BlockSpec((P_TILE, 512), lambda i: (i, 0)),
        pl.BlockSpec((P_TILE, 512), lambda i: (i, 1)),
        pl.BlockSpec((B_W, P_TILE), lambda i: (0, back2(i))),
        pl.BlockSpec((B_W, P_TILE), lambda i: (0, back1(i))),
        pl.BlockSpec((B_W, P_TILE), lambda i: (0, i)),
        pl.BlockSpec((P_TILE, 512), lambda i: (back2(i), 3)),
        pl.BlockSpec((P_TILE, 512), lambda i: (back1(i), 3)),
        pl.BlockSpec((P_TILE, 512), lambda i: (i, 3)),
        pl.BlockSpec((A_KVW, P_TILE), lambda i: (0, i)),
        pl.BlockSpec((A_KVW, A_PREV), lambda i: (0, prev_a(i))),
        pl.BlockSpec((P_TILE, 256), lambda i: (i, 8)),
        pl.BlockSpec((A_PREV, 256), lambda i: (prev_a(i), 8)),
        pl.BlockSpec((None,) + bias_a.shape[1:], lambda i: (jnp.minimum(i, 1), 0, 0, 0, 0)),
        _const_spec(sink.shape),
        pl.BlockSpec((None,) + bias_b.shape[1:], lambda i: (jnp.minimum(i, 2), 0, 0, 0)),
    ]
    return pl.pallas_call(
        _attn_prompt_kernel,
        grid=(t // P_TILE,),
        in_specs=in_specs,
        out_specs=pl.BlockSpec((P_TILE, D_MODEL), lambda i: (i, 0)),
        out_shape=jax.ShapeDtypeStruct((t, D_MODEL), BF16),
        compiler_params=_params(1),
        name="attn_prompt",
    )(proj, proj, kbt, kbt, kbt, proj, proj, proj, kat, kat, proj, proj, bias_a, sink, bias_b)


def _attn_sample_kernel(qa_ref, qb_ref, kb_ref, vb_ref, kva_ref, cakt_ref, cav_ref, cbkt_ref, cbv_ref,
                        bias_a_ref, sink_ref, bias_b_ref, o_ref, *, nb):
    for b in range(nb):
        rows = slice(b * CHUNK, (b + 1) * CHUNK)
        va_cache, va_new = cav_ref[b], kva_ref[rows, A_KVW:2 * A_KVW]
        for g in range(A_KV_HEADS):
            heads = range(g * A_GROUP, (g + 1) * A_GROUP)
            hd = slice(g * HEAD_DIM, (g + 1) * HEAD_DIM)
            qs = jnp.concatenate([qa_ref[rows, h * HEAD_DIM:(h + 1) * HEAD_DIM] for h in heads], axis=0)
            scores = [_dot(qs, cakt_ref[b, g]) + bias_a_ref[g, :, 0:A_PREV],
                      _dot_nt(qs, kva_ref[rows, hd]) + bias_a_ref[g, :, A_PREV:A_BAND]]
            values = [_head_on_both_halves(va_cache, g == 0), _head_on_both_halves(va_new, g == 0)]
            o = _softmax_pv_blocks(scores, values, sink_ref[g])
            for j in range(0, A_GROUP, 2):
                pair = jnp.where(_low_half(CHUNK), o[j * CHUNK:(j + 1) * CHUNK], o[(j + 1) * CHUNK:(j + 2) * CHUNK])
                c0 = (g * A_GROUP + j) * HEAD_DIM
                o_ref[rows, c0:c0 + 2 * HEAD_DIM] = pair.astype(BF16)
        for h in range(0, B_HEADS, 2):
            pair_cols = slice(h * HEAD_DIM, (h + 2) * HEAD_DIM)
            v_cache, v_new = cbv_ref[b, :, pair_cols], vb_ref[rows, pair_cols]
            outs = []
            for hh in (h, h + 1):
                cols = slice(hh * HEAD_DIM, (hh + 1) * HEAD_DIM)
                q = qb_ref[rows, cols]
                scores = [_dot(q, cbkt_ref[b, hh]) + bias_b_ref[hh, :, 0:B_PREV],
                          _dot_nt(q, kb_ref[rows, cols]) + bias_b_ref[hh, :, B_PREV:B_BAND]]
                values = [_one_head(v_cache, hh == h), _one_head(v_new, hh == h)]
                outs.append(_softmax_pv_blocks(scores, values, None))
            o_ref[rows, A_QW + h * HEAD_DIM:A_QW + (h + 2) * HEAD_DIM] = (outs[0] + outs[1]).astype(BF16)


def _attn_sample_call(proj, ca_kt, ca_v, cb_kt, cb_v, e, bias_a, sink, bias_b, nb):
    t = proj.shape[0]
    nbatch = t // CHUNK
    tm = nb * CHUNK
    in_specs = [
        pl.BlockSpec((tm, 512), lambda i: (i, 0)),
        pl.BlockSpec((tm, 512), lambda i: (i, 1)),
        pl.BlockSpec((tm, 512), lambda i: (i, 2)),
        pl.BlockSpec((tm, 512), lambda i: (i, 3)),
        pl.BlockSpec((tm, 256), lambda i: (i, 8)),
        pl.BlockSpec((None, nb, A_KV_HEADS, HEAD_DIM, A_PREV), lambda i: (e, i, 0, 0, 0)),
        pl.BlockSpec((None, nb, A_PREV, A_KVW), lambda i: (e, i, 0, 0)),
        pl.BlockSpec((None, nb, B_HEADS, HEAD_DIM, B_PREV), lambda i: (e, i, 0, 0, 0)),
        pl.BlockSpec((None, nb, B_PREV, B_W), lambda i: (e, i, 0, 0)),
        _const_spec(bias_a.shape), _const_spec(sink.shape), _const_spec(bias_b.shape),
    ]
    return pl.pallas_call(
        functools.partial(_attn_sample_kernel, nb=nb),
        grid=(nbatch // nb,),
        in_specs=in_specs,
        out_specs=pl.BlockSpec((tm, D_MODEL), lambda i: (i, 0)),
        out_shape=jax.ShapeDtypeStruct((t, D_MODEL), BF16),
        compiler_params=_params(1),
        name="attn_sample",
    )(proj, proj, proj, proj, proj, ca_kt, ca_v, cb_kt, cb_v, bias_a, sink, bias_b)


M_PAIRS = M_HEADS // 2
M_QK_PAIR = 2 * M_DK
M_V_PAIR = 2 * M_DV


def _split3(x):
    hi = x.astype(BF16)
    r1 = x - hi.astype(F32)
    mid = r1.astype(BF16)
    return hi, mid, (r1 - mid.astype(F32)).astype(BF16)


def _widen_heads(x):
    first = lax.broadcasted_iota(jnp.int32, (x.shape[0], M_QK_PAIR), 1) < M_DK
    outs = []
    for p in range(M_PAIRS):
        src = x[:, p * M_QK_PAIR:(p + 1) * M_QK_PAIR]
        swapped = pltpu.roll(src, M_DK, 1)
        outs += [jnp.where(first, src, swapped), jnp.where(first, swapped, src)]
    return jnp.concatenate(outs, axis=1)


def _mlstm_chunk(q, k, v, o, gates, c_s, n_s, m_s):
    L = CHUNK
    wide = (L, M_QKW)
    row = lax.broadcasted_iota(jnp.int32, wide, 0)
    pos = lax.broadcasted_iota(jnp.int32, wide, 1) % M_DK
    ig = gates[:, 0:M_QKW]
    fg = gates[:, M_QKW:2 * M_QKW]
    lf = jnp.minimum(fg, 0.0) - jnp.log(1.0 + jnp.exp(-jnp.abs(fg)))

    t_row = lax.broadcasted_iota(jnp.int32, (L, L), 0)
    t_col = lax.broadcasted_iota(jnp.int32, (L, L), 1)
    tril = jnp.where(t_col <= t_row, 1.0, 0.0).astype(BF16)
    hi, mid, lo = _split3(lf)
    b_c = _dot(tril, hi) + _dot(tril, mid) + _dot(tril, lo)
    r_c = ig - b_c
    r_row = jnp.sum(jnp.where(row == pos, r_c, 0.0), axis=0, keepdims=True)

    cm = r_c
    for sh in (1, 2, 4, 8, 16, 32):
        cm = jnp.maximum(cm, jnp.where(row >= sh, pltpu.roll(cm, sh, 0), NEG_INF))
    m_prev = m_s[...]
    mm = jnp.maximum(m_prev, cm)
    w = jnp.exp(jnp.where(pos <= row, r_row - mm, NEG_INF))
    g_inter = jnp.exp(m_prev - mm)
    qf = q.astype(F32)
    qg = g_inter * qf
    n_prev = n_s[...]

    lane_qk = lax.broadcasted_iota(jnp.int32, (L, M_QK_PAIR), 1)
    lane_v = lax.broadcasted_iota(jnp.int32, (L, M_V_PAIR), 1)
    zero_b = jnp.zeros((), BF16)
    bd_r = lax.broadcasted_iota(jnp.int32, (M_QK_PAIR, M_V_PAIR), 0) < M_DK
    bd_c = lax.broadcasted_iota(jnp.int32, (M_QK_PAIR, M_V_PAIR), 1) < M_DV
    on_diag = bd_r == bd_c
    ones_bd = jnp.where(on_diag, 1.0, 0.0).astype(BF16)

    s_parts, k_bd, v_bd = [], [], []
    for p in range(M_PAIRS):
        qk_l = slice(p * M_QK_PAIR, (p + 1) * M_QK_PAIR)
        v_l = slice(p * M_V_PAIR, (p + 1) * M_V_PAIR)
        kp, vp = k[:, qk_l], v[:, v_l]
        k_bd.append(jnp.concatenate([jnp.where(lane_qk < M_DK, kp, zero_b),
                                     jnp.where(lane_qk >= M_DK, kp, zero_b)], axis=0))
        v_bd.append(jnp.concatenate([jnp.where(lane_v < M_DV, vp, zero_b),
                                     jnp.where(lane_v >= M_DV, vp, zero_b)], axis=0))
        s_parts.append(_dot_nt(q[:, qk_l], k_bd[p]))
    sc = jnp.concatenate(s_parts, axis=1) * w
    den_in = sc + qg * n_prev
    d_hi, d_mid, _ = _split3(den_in)
    sc_b = sc.astype(BF16)
    qg_b = qg.astype(BF16)

    nums, dens = [], []
    for p in range(M_PAIRS):
        qk_l = slice(p * M_QK_PAIR, (p + 1) * M_QK_PAIR)
        lhs = jnp.concatenate([sc_b[:, qk_l], qg_b[:, qk_l]], axis=1)
        rhs = jnp.concatenate([v_bd[p], c_s[p].astype(BF16)], axis=0)
        nums.append(_dot(lhs, rhs))
        dens.append(_dot(d_hi[:, qk_l], ones_bd) + _dot(d_mid[:, qk_l], ones_bd))
    num = jnp.concatenate(nums, axis=1)
    den = jnp.concatenate(dens, axis=1)
    floor = _widen_heads(jnp.exp(-b_c - mm))
    hs = num / jnp.maximum(jnp.abs(den), floor)

    b_last = b_c[L - 1:L, :]
    mm_last = mm[L - 1:L, :]
    decay = jnp.exp(m_prev - mm_last)
    wk = jnp.exp(r_c - mm_last) * k.astype(F32)
    n_s[...] = decay * n_prev + jnp.sum(wk, axis=0, keepdims=True)
    m_s[...] = b_last + mm_last
    decay_v = _widen_heads(jnp.broadcast_to(decay, (8, M_QKW)))[0:1]
    wk_b = wk.astype(BF16)
    for p in range(M_PAIRS):
        qk_l = slice(p * M_QK_PAIR, (p + 1) * M_QK_PAIR)
        v_l = slice(p * M_V_PAIR, (p + 1) * M_V_PAIR)
        upd = _dot_tn(wk_b[:, qk_l], v[:, v_l])
        c_s[p] = decay_v[:, v_l] * c_s[p] + jnp.where(on_diag, upd, 0.0)

    of = o.astype(F32)
    return (hs * (1.0 / (1.0 + jnp.exp(-of)))).astype(BF16)


def _mlstm_kernel(q_ref, k_ref, v_ref, o_ref, gates_ref, bias_ref, c_in, n_in, m_in,
                  hg_ref, c_out, n_out, m_out, c_s, n_s, m_s, *, chunks):
    j = pl.program_id(1)

    @pl.when(j == 0)
    def _():
        c_s[...] = c_in[0]
        n_s[...] = n_in[0]
        m_s[...] = m_in[0]

    for c in range(chunks):
        rows = slice(c * CHUNK, (c + 1) * CHUNK)
        hg_ref[rows, :] = _mlstm_chunk(q_ref[rows, :], k_ref[rows, :], v_ref[rows, :], o_ref[rows, :],
                                       gates_ref[rows, :] + bias_ref[...], c_s, n_s, m_s)

    @pl.when(j == pl.num_programs(1) - 1)
    def _():
        c_out[0] = c_s[...]
        n_out[0] = n_s[...]
        m_out[0] = m_s[...]


def _mlstm_call(proj, gates, bias, c0, n0, m0, layer, chunks_per_step):
    t = proj.shape[0]
    nseq = c0.shape[1]
    rows = chunks_per_step * CHUNK
    nsteps = t // (nseq * rows)
    blk = lambda s, j: s * nsteps + j
    c_blk = (1, M_PAIRS, M_QK_PAIR, M_V_PAIR)
    r_blk = (1, 1, M_QKW)
    in_specs = [
        pl.BlockSpec((rows, M_QKW), lambda s, j: (blk(s, j), 0)),
        pl.BlockSpec((rows, M_QKW), lambda s, j: (blk(s, j), 1)),
        pl.BlockSpec((rows, M_VW), lambda s, j: (blk(s, j), 1)),
        pl.BlockSpec((rows, M_VW), lambda s, j: (blk(s, j), 2)),
        pl.BlockSpec((rows, 2 * M_QKW), lambda s, j: (blk(s, j), 0)),
        _const_spec((1, 2 * M_QKW)),
        pl.BlockSpec((None,) + c_blk, lambda s, j: (layer, s, 0, 0, 0)),
        pl.BlockSpec((None,) + r_blk, lambda s, j: (layer, s, 0, 0)),
        pl.BlockSpec((None,) + r_blk, lambda s, j: (layer, s, 0, 0)),
    ]
    out_specs = (
        pl.BlockSpec((rows, M_VW), lambda s, j: (blk(s, j), 0)),
        pl.BlockSpec(c_blk, lambda s, j: (s, 0, 0, 0)),
        pl.BlockSpec(r_blk, lambda s, j: (s, 0, 0)),
        pl.BlockSpec(r_blk, lambda s, j: (s, 0, 0)),
    )
    out_shape = (
        jax.ShapeDtypeStruct((t, M_VW), BF16),
        jax.ShapeDtypeStruct((nseq,) + c_blk[1:], F32),
        jax.ShapeDtypeStruct((nseq,) + r_blk[1:], F32),
        jax.ShapeDtypeStruct((nseq,) + r_blk[1:], F32),
    )
    return pl.pallas_call(
        functools.partial(_mlstm_kernel, chunks=chunks_per_step),
        grid=(nseq, nsteps),
        in_specs=in_specs,
        out_specs=out_specs,
        out_shape=out_shape,
        scratch_shapes=[pltpu.VMEM(c_blk[1:], F32), pltpu.VMEM(r_blk[1:], F32), pltpu.VMEM(r_blk[1:], F32)],
        compiler_params=_params(2),
        name="mlstm",
    )(proj, proj, proj, proj, gates, bias, c0, n0, m0)


def _mlstm_state_to_kernel(c, n, m):
    lead = c.shape[:-3]
    cp = c.reshape(lead + (M_PAIRS, 2, M_DK, M_DV))
    z = jnp.zeros_like(cp[..., 0, :, :])
    top = jnp.concatenate([cp[..., 0, :, :], z], axis=-1)
    bot = jnp.concatenate([z, cp[..., 1, :, :]], axis=-1)
    c_bd = jnp.concatenate([top, bot], axis=-2)
    n_row = n.reshape(lead + (1, M_QKW))
    m_row = jnp.repeat(m, M_DK, axis=-1).reshape(lead + (1, M_QKW))
    return c_bd, n_row, m_row


def _mlstm_state_from_kernel(c_bd, n_row, m_row):
    lead = c_bd.shape[:-3]
    top = c_bd[..., :M_DK, :M_DV]
    bot = c_bd[..., M_DK:, M_DV:]
    c = jnp.stack([top, bot], axis=-3).reshape(lead + (M_HEADS, M_DK, M_DV))
    n = n_row.reshape(lead + (M_HEADS, M_DK))
    m = m_row.reshape(lead + (M_HEADS, M_DK))[..., 0]
    return c, n, m


def _post_kernel(x_ref, a_ref, wout_ref, g_ref, wq_ref, mk_ref, mv_ref, wo_ref, o_ref, att_s, *, nb, tb):
    x1 = x_ref[...] + _dot(a_ref[...], wout_ref[...])
    q = _dot(_rms(x1, g_ref[...]).astype(BF16), wq_ref[...]).astype(BF16)
    scale = MEM_HD ** -0.5
    for b in range(nb):
        rows = slice(b * tb, (b + 1) * tb)
        for h in range(MEM_HEADS):
            cols = slice(h * MEM_HD, (h + 1) * MEM_HD)
            s = _dot_nt(q[rows, cols], mk_ref[b, h]) * scale
            e = jnp.exp(s - jnp.max(s, axis=-1, keepdims=True))
            p = (e / jnp.sum(e, axis=-1, keepdims=True)).astype(BF16)
            att_s[rows, cols] = _dot(p, mv_ref[b, h]).astype(BF16)
    o_ref[...] = x1 + _dot(att_s[...], wo_ref[...])


def _post_call(x, a, w_out, wl, g, w_q, mk, mv, w_o, layer, tm, tb):
    t = x.shape[0]
    nb = tm // tb
    shared = mk.shape[1] == 1
    tail = (0,) * (mk.ndim - 2)
    mem_map = (lambda i: (layer, 0) + tail) if shared else (lambda i: (layer, i) + tail)
    mem_blk = (None, nb) + mk.shape[2:]
    row = lambda i: (i, 0)
    in_specs = [
        pl.BlockSpec((tm, D_MODEL), row),
        pl.BlockSpec((tm, a.shape[1]), row),
        _layer_spec(w_out, wl),
        _layer_spec(g, layer),
        _layer_spec(w_q, layer),
        pl.BlockSpec(mem_blk, mem_map),
        pl.BlockSpec(mem_blk, mem_map),
        _layer_spec(w_o, layer),
    ]
    return pl.pallas_call(
        functools.partial(_post_kernel, nb=nb, tb=tb),
        grid=(t // tm,),
        in_specs=in_specs,
        out_specs=pl.BlockSpec((tm, D_MODEL), row),
        out_shape=jax.ShapeDtypeStruct((t, D_MODEL), F32),
        scratch_shapes=[pltpu.VMEM((tm, D_MODEL), BF16)],
        compiler_params=_params(1),
        name="post_cross",
    )(x, a, w_out, g, w_q, mk, mv, w_o)


def _conv3(u, cw, hists, seq_rows):
    w0, w1, w2 = cw[0:1], cw[1:2], cw[2:3]
    c = pltpu.roll(u, 2, 0) * w0 + pltpu.roll(u, 1, 0) * w1 + u * w2
    row = lax.broadcasted_iota(jnp.int32, (8, u.shape[1]), 0)
    pieces = []
    for s, hist in enumerate(hists):
        base = s * seq_rows
        t8 = u[base:base + 8]
        h0, h1 = hist[0:1], hist[1:2]
        u1 = jnp.where(row == 0, h1, pltpu.roll(t8, 1, 0))
        u2 = jnp.where(row == 0, h0, jnp.where(row == 1, h1, pltpu.roll(t8, 2, 0)))
        pieces += [u2 * w0 + u1 * w1 + t8 * w2, c[base + 8:base + seq_rows]]
    return jnp.concatenate(pieces, axis=0)


def _ffn_kernel(*refs, seq_rows, carry, final_norm):
    x_ref, g_ref, win_ref, cw_ref, wout_ref = refs[:5]
    pos = 5
    hist_ref = None
    if not carry:
        hist_ref = refs[pos]
        pos += 1
    gfin_ref = None
    if final_norm:
        gfin_ref = refs[pos]
        pos += 1
    o_ref, cs_ref, act_s = refs[pos:pos + 3]
    hist_s = refs[pos + 3] if carry else None

    tm = x_ref.shape[0]
    nseq = tm // seq_rows
    x = x_ref[...]
    h = _rms(x, g_ref[...]).astype(BF16)

    if carry:
        @pl.when(pl.program_id(0) == 0)
        def _():
            hist_s[...] = jnp.zeros_like(hist_s)

    for j in range(D_FF // FF_CHUNK):
        halves = []
        for half in range(2):
            c0 = half * D_FF + j * FF_CHUNK
            cols = slice(c0, c0 + FF_CHUNK)
            u = _dot(h, win_ref[:, cols])
            if carry:
                hists = [hist_s[:, cols]]
            else:
                hists = [hist_ref[s, :, cols] for s in range(nseq)]
            halves.append(_conv3(u, cw_ref[:, cols], hists, seq_rows))
            for s in range(nseq):
                last2 = u[(s + 1) * seq_rows - 2:(s + 1) * seq_rows]
                if carry:
                    hist_s[:, cols] = last2
                    cs_ref[:, cols] = last2
                else:
                    cs_ref[s, :, cols] = last2
        ca, cg = halves
        act = ca * (1.0 / (1.0 + jnp.exp(-ca))) * cg
        act_s[:, j * FF_CHUNK:(j + 1) * FF_CHUNK] = act.astype(BF16)

    y = x + _dot(act_s[...], wout_ref[...])
    if final_norm:
        y = _rms(y, gfin_ref[...])
    o_ref[...] = y


def _ffn_call(x, g, w_in, conv_w, w_out, layer, hist, g_final, tm):
    t = x.shape[0]
    carry = hist is None
    seq_rows = tm if carry else CHUNK
    nseq = tm // seq_rows
    row = lambda i: (i, 0)
    args = [x, g, w_in, conv_w, w_out]
    in_specs = [pl.BlockSpec((tm, D_MODEL), row), _layer_spec(g, layer),
                _layer_spec(w_in, layer), _layer_spec(conv_w, layer), _layer_spec(w_out, layer)]
    if carry:
        cs_shape, cs_spec = (2, 2 * D_FF), _const_spec((2, 2 * D_FF))
    else:
        args.append(hist)
        in_specs.append(pl.BlockSpec((None, nseq, 2, 2 * D_FF), lambda i: (layer, i, 0, 0)))
        cs_shape, cs_spec = hist.shape[1:], pl.BlockSpec((nseq, 2, 2 * D_FF), lambda i: (i, 0, 0))
    if g_final is not None:
        args.append(g_final.reshape(1, D_MODEL))
        in_specs.append(_const_spec((1, D_MODEL)))
    scratch = [pltpu.VMEM((tm, D_FF), BF16)]
    if carry:
        scratch.append(pltpu.VMEM((2, 2 * D_FF), F32))
    return pl.pallas_call(
        functools.partial(_ffn_kernel, seq_rows=seq_rows, carry=carry, final_norm=g_final is not None),
        grid=(t // tm,),
        in_specs=in_specs,
        out_specs=(pl.BlockSpec((tm, D_MODEL), row), cs_spec),
        out_shape=(jax.ShapeDtypeStruct((t, D_MODEL), F32), jax.ShapeDtypeStruct(cs_shape, F32)),
        scratch_shapes=scratch,
        compiler_params=_params(1),
        name="conv_ffn",
    )(*args)


def _alibi(n_q, n_k):
    q = np.arange(n_q)[:, None]
    dist = np.abs(q + A_PREV - np.arange(n_k)[None, :]).astype(np.float32)
    slopes = (2.0 ** (-8.0 * np.arange(1, A_HEADS + 1) / A_HEADS)).astype(np.float32)
    return -slopes[:, None, None] * dist[None]


def _band_mask(n_q, n_k, n_prev_chunks, first_valid_chunk):
    qc = (np.arange(n_q) // CHUNK)[:, None]
    kc = (np.arange(n_k) // CHUNK)[None, :]
    ok = (kc >= qc) & (kc <= qc + n_prev_chunks) & (kc >= first_valid_chunk)
    return np.where(ok, 0.0, NEG_INF).astype(np.float32)


def _alibi_sample_table():
    return jnp.asarray(_alibi(CHUNK, A_BAND).reshape(A_KV_HEADS, A_GROUP * CHUNK, A_BAND))


def _alibi_prompt_tables():
    n_prev = A_PREV // CHUNK
    base = _alibi(P_SUB, PA_KEYS)
    general = (base + _band_mask(P_SUB, PA_KEYS, n_prev, 0)).reshape(A_KV_HEADS, A_GROUP * P_SUB, PA_KEYS)
    first = (base + _band_mask(P_SUB, PA_KEYS, n_prev, n_prev)).reshape(A_KV_HEADS, A_GROUP * P_SUB, PA_KEYS)
    return jnp.asarray(np.stack([np.stack([first, general]), np.stack([general, general])]))


def _relpos_toeplitz(table, n_q, n_k):
    h, n_rel = table.shape
    n_lo = n_k - B_BAND
    n_hi = n_q + n_k - 1 - n_lo - n_rel
    ext = jnp.concatenate([jnp.broadcast_to(table[:, :1], (h, n_lo + 1)), table,
                           jnp.broadcast_to(table[:, -1:], (h, n_hi))], axis=1)
    rev = ext[:, ::-1][:, None, :]
    length = n_q + n_k - 1
    tiled = jnp.broadcast_to(rev, (h, n_q, length + 1)).reshape(h, n_q * (length + 1))
    skew = tiled[:, :n_q * length].reshape(h, n_q, length)
    return skew[:, :, n_q - 1:n_q - 1 + n_k]


def _relpos_prompt_tables(table):
    n_prev = B_PREV // CHUNK
    masks = np.stack([_band_mask(P_TILE, PB_KEYS, n_prev, max(n_prev - t * (P_TILE // CHUNK), 0)) for t in range(3)])
    return _relpos_toeplitz(table, P_TILE, PB_KEYS)[None] + jnp.asarray(masks)[:, None]


def _row_tile(t):
    return PROMPT_TILE if t % PROMPT_TILE == 0 else 256


def kernel(x_prompt, x_sample, mem_prompt, cache_a_k, cache_a_v, cache_b_k, cache_b_v, state_mlstm_c, state_mlstm_n, state_mlstm_m, cache_mem_k, cache_mem_v, state_ffn_conv, norm_mix, norm_cross, norm_ffn, norm_final, w_in_att, w_out_att, sink_a, relpos_b, w_in_mlstm, b_igate, b_fgate, w_out_mlstm, w_mem_q, w_mem_k, w_mem_v, w_mem_o, w_ffn_in, conv_ffn, w_ffn_out):
    depth = norm_mix.shape[0]
    assert x_prompt.shape[0] == 1
    tp = x_prompt.shape[1]
    nb, ds = x_sample.shape[0], x_sample.shape[1]
    assert ds == CHUNK and tp % PROMPT_TILE == 0
    ts = nb * ds
    tm_p, tm_s = PROMPT_TILE, _row_tile(ts)
    xp = x_prompt[0]
    xs = x_sample.reshape(ts, D_MODEL)

    memkv = _memkv_call(mem_prompt[0], jnp.concatenate([w_mem_k, w_mem_v], axis=0).astype(BF16))
    head_major = lambda m: jnp.transpose(m, (0, 1, 3, 2, 4)).astype(BF16)
    split_heads = lambda m: m.reshape(depth, 1, MEM_TOKENS, MEM_HEADS, MEM_HD)
    p_mk, p_mv = head_major(split_heads(memkv[:depth])), head_major(split_heads(memkv[depth:]))
    s_mk, s_mv = head_major(cache_mem_k), head_major(cache_mem_v)

    g_mix = norm_mix.astype(F32)[:, None]
    g_cross = norm_cross.astype(F32)[:, None]
    g_ffn = norm_ffn.astype(F32)[:, None]
    wa = w_in_att
    scale = HEAD_DIM ** -0.5
    w_att = jnp.concatenate([wa[:, :, 0:512] * scale, wa[:, :, 768:1280] * scale, wa[:, :, 1280:2304],
                             wa[:, :, 512:768]], axis=2).astype(BF16)
    wm = w_in_mlstm
    k_scale = M_DK ** -0.5
    w_mls = jnp.concatenate([wm[:, :, 0:M_QKW], wm[:, :, M_QKW:2 * M_QKW] * k_scale, wm[:, :, 2 * M_QKW:M_MAIN],
                             jnp.repeat(wm[:, :, M_MAIN:], M_DK, axis=2)], axis=2).astype(BF16)
    w_out_a, w_out_m = w_out_att.astype(BF16), w_out_mlstm.astype(BF16)
    w_q, w_o = w_mem_q.astype(BF16), w_mem_o.astype(BF16)
    w_fi, w_fo = w_ffn_in.astype(BF16), w_ffn_out.astype(BF16)
    conv_w = conv_ffn.astype(F32)
    hist_s = state_ffn_conv.astype(F32)
    n_even = cache_a_k.shape[0]
    ca_kt = jnp.transpose(cache_a_k, (0, 1, 3, 4, 2)).astype(BF16)
    cb_kt = jnp.transpose(cache_b_k, (0, 1, 3, 4, 2)).astype(BF16)
    ca_v = cache_a_v.reshape(n_even, nb, A_PREV, A_KVW).astype(BF16)
    cb_v = cache_b_v.reshape(n_even, nb, B_PREV, B_W).astype(BF16)
    st_c, st_n, st_m = _mlstm_state_to_kernel(state_mlstm_c.astype(F32), state_mlstm_n.astype(F32),
                                              state_mlstm_m.astype(F32))
    zc = jnp.zeros((1, 1, M_PAIRS, M_QK_PAIR, M_V_PAIR), F32)
    zn = jnp.zeros((1, 1, 1, M_QKW), F32)
    zm = jnp.zeros((1, 1, 1, M_QKW), F32)

    bias_a_s = _alibi_sample_table()
    bias_a_p = _alibi_prompt_tables()
    post_tm_s = tm_s
    attn_nb = 8 if nb % 8 == 0 else 4

    p_ak, p_av, p_bk, p_bv, p_c, p_n, p_m, p_conv = [], [], [], [], [], [], [], []
    s_ak, s_av, s_bk, s_bv, s_c, s_n, s_m, s_conv = [], [], [], [], [], [], [], []
    for l in range(depth):
        if l % 2 == 0:
            e = l // 2
            proj_p, kv_p, kbt_p, kat_p = _att_proj(xp, g_mix, l, w_att, e, tm_p, prompt=True)
            proj_s, kv_s = _att_proj(xs, g_mix, l, w_att, e, tm_s, prompt=False)
            sk = sink_a[e].astype(F32)
            sink_p = jnp.repeat(sk, P_SUB).reshape(A_KV_HEADS, A_GROUP * P_SUB, 1)
            sink_s = jnp.repeat(sk, CHUNK).reshape(A_KV_HEADS, A_GROUP * CHUNK, 1)
            rel = relpos_b[e].astype(F32)
            mix_p = _attn_prompt_call(proj_p, kbt_p, kat_p, bias_a_p, sink_p, _relpos_prompt_tables(rel))
            mix_s = _attn_sample_call(proj_s, ca_kt, ca_v, cb_kt, cb_v, e, bias_a_s, sink_s,
                                      _relpos_toeplitz(rel, CHUNK, B_BAND), attn_nb)
            w_mix_out, wl = w_out_a, e
            p_ak.append(kv_p[tm_p - A_PREV:, 0:128].reshape(1, A_PREV, A_KV_HEADS, HEAD_DIM))
            p_av.append(kv_p[tm_p - A_PREV:, 128:256].reshape(1, A_PREV, A_KV_HEADS, HEAD_DIM))
            p_bk.append(kv_p[:, 256:768].reshape(1, B_PREV, B_HEADS, HEAD_DIM))
            p_bv.append(kv_p[:, 768:1280].reshape(1, B_PREV, B_HEADS, HEAD_DIM))
            s_ak.append(kv_s[:, 0:128].reshape(nb, ds, A_KV_HEADS, HEAD_DIM))
            s_av.append(kv_s[:, 128:256].reshape(nb, ds, A_KV_HEADS, HEAD_DIM))
            s_bk.append(kv_s[:, 256:768].reshape(nb, ds, B_HEADS, HEAD_DIM))
            s_bv.append(kv_s[:, 768:1280].reshape(nb, ds, B_HEADS, HEAD_DIM))
        else:
            o = l // 2
            gate_bias = jnp.concatenate([jnp.repeat(b_igate[o].astype(F32), M_DK),
                                         jnp.repeat(b_fgate[o].astype(F32), M_DK)]).reshape(1, M_GATES_W)
            proj_p, gates_p = _mlstm_proj(xp, g_mix, l, w_mls, o, tm_p)
            proj_s, gates_s = _mlstm_proj(xs, g_mix, l, w_mls, o, tm_s)
            mix_p, c, n, m = _mlstm_call(proj_p, gates_p, gate_bias, zc, zn, zm, 0, 4)
            c, n, m = _mlstm_state_from_kernel(c, n, m)
            p_c.append(c); p_n.append(n); p_m.append(m)
            mix_s, c, n, m = _mlstm_call(proj_s, gates_s, gate_bias, st_c, st_n, st_m, o, 1)
            c, n, m = _mlstm_state_from_kernel(c, n, m)
            s_c.append(c); s_n.append(n); s_m.append(m)
            w_mix_out, wl = w_out_m, o

        xp = _post_call(xp, mix_p, w_mix_out, wl, g_cross, w_q, p_mk, p_mv, w_o, l, tm_p, tm_p)
        xs = _post_call(xs, mix_s, w_mix_out, wl, g_cross, w_q, s_mk, s_mv, w_o, l, post_tm_s, CHUNK)

        g_fin = norm_final if l == depth - 1 else None
        xp, cv = _ffn_call(xp, g_ffn, w_fi, conv_w, w_fo, l, None, g_fin, tm_p)
        p_conv.append(cv[None])
        xs, cv = _ffn_call(xs, g_ffn, w_fi, conv_w, w_fo, l, hist_s, g_fin, tm_s)
        s_conv.append(cv)

    p_mk, p_mv = memkv[:depth], memkv[depth:]
    mem_shape = (depth, 1, MEM_TOKENS, MEM_HEADS, MEM_HD)
    return (xp[None], xs.reshape(nb, ds, D_MODEL),
            jnp.stack(p_ak), jnp.stack(p_av), jnp.stack(p_bk), jnp.stack(p_bv),
            jnp.stack(p_c), jnp.stack(p_n), jnp.stack(p_m),
            p_mk.reshape(mem_shape), p_mv.reshape(mem_shape), jnp.stack(p_conv),
            jnp.stack(s_ak), jnp.stack(s_av), jnp.stack(s_bk), jnp.stack(s_bv),
            jnp.stack(s_c), jnp.stack(s_n), jnp.stack(s_m), jnp.stack(s_conv))
```

```python
import functools

import numpy as np
import jax
import jax.numpy as jnp
from jax import lax
from jax.experimental import pallas as pl
from jax.experimental.pallas import tpu as pltpu

F32 = jnp.float32
BF16 = jnp.bfloat16

D_MODEL = 1024
CHUNK = 64
HEAD_DIM = 64
A_HEADS = 8
A_KV_HEADS = 2
A_GROUP = A_HEADS // A_KV_HEADS
A_BAND = 3 * CHUNK
A_PREV = 2 * CHUNK
B_HEADS = 8
B_BAND = 9 * CHUNK
B_PREV = 8 * CHUNK
REL_CLIP = 128
A_QW = A_HEADS * HEAD_DIM
A_KVW = A_KV_HEADS * HEAD_DIM
B_W = B_HEADS * HEAD_DIM
ATT_PROJ = A_QW + 2 * A_KVW + 3 * B_W
KV32_W = 2 * A_KVW + 2 * B_W
M_HEADS = 8
M_DK = 64
M_DV = 128
M_QKW = M_HEADS * M_DK
M_VW = M_HEADS * M_DV
M_MAIN = 2 * M_QKW + 2 * M_VW
MEM_TOKENS = 256
MEM_HEADS = 4
MEM_HD = D_MODEL // MEM_HEADS
D_FF = 2816
FF_CHUNK = 256
EPS = 1e-6
NEG_INF = float("-inf")

VMEM_LIMIT_BYTES = 56 * 1024 * 1024
PROMPT_TILE = 512
WIDE_TILE = 1024


def _params(n_axes):
    return pltpu.CompilerParams(dimension_semantics=("arbitrary",) * n_axes,
                                vmem_limit_bytes=VMEM_LIMIT_BYTES)


def _const_spec(shape):
    zeros = (0,) * len(shape)
    return pl.BlockSpec(shape, lambda *_: zeros)


def _layer_spec(stack, layer):
    idx = (layer,) + (0,) * (stack.ndim - 1)
    return pl.BlockSpec((None,) + stack.shape[1:], lambda *_: idx, pipeline_mode=pl.Buffered(1))


def _rms(x, g):
    return x * lax.rsqrt(jnp.mean(x * x, axis=-1, keepdims=True) + EPS) * g


def _dot(a, b):
    return jnp.dot(a, b, preferred_element_type=F32)


def _dot_nt(a, b):
    return lax.dot_general(a, b, (((1,), (1,)), ((), ())), preferred_element_type=F32)


def _dot_tn(a, b):
    return lax.dot_general(a, b, (((0,), (0,)), ((), ())), preferred_element_type=F32)


def _proj_kernel(x_ref, g_ref, w_ref, *out_refs, plan, transposed):
    h = _rms(x_ref[...], g_ref[...]).astype(BF16)
    for c0, cw, dsts in plan:
        r = _dot(h, w_ref[:, c0:c0 + cw])
        for oi, oc in dsts:
            out_refs[oi][:, oc:oc + cw] = r.astype(out_refs[oi].dtype)
        if c0 in transposed:
            oi, width = transposed[c0]
            out_refs[oi][...] = r[:, :width].T.astype(out_refs[oi].dtype)


def _proj_call(x, g, gl, w, wl, plan, transposed, out_shapes, out_specs, tm, name):
    t = x.shape[0]
    return pl.pallas_call(
        functools.partial(_proj_kernel, plan=plan, transposed=transposed),
        grid=(t // tm,),
        in_specs=[pl.BlockSpec((tm, D_MODEL), lambda i: (i, 0)),
                  _layer_spec(g, gl),
                  _layer_spec(w, wl)],
        out_specs=out_specs,
        out_shape=out_shapes,
        compiler_params=_params(1),
        name=name,
    )(x, g, w)


_ATT_PLAN = (
    (0, 512, ((0, 0),)),
    (512, 512, ((0, 512),)),
    (1024, 512, ((0, 1024), (1, 256))),
    (1536, 512, ((0, 1536), (1, 768))),
    (2048, 256, ((0, 2048), (1, 0))),
)


def _att_proj(x, g, gl, w_perm, wl, tm, prompt):
    t = x.shape[0]
    shapes = [jax.ShapeDtypeStruct((t, ATT_PROJ), BF16)]
    specs = [pl.BlockSpec((tm, ATT_PROJ), lambda i: (i, 0))]
    transposed = {}
    if prompt:
        shapes += [jax.ShapeDtypeStruct((tm, KV32_W), F32), jax.ShapeDtypeStruct((B_W, t), BF16),
                   jax.ShapeDtypeStruct((A_KVW, t), BF16)]
        specs += [_const_spec((tm, KV32_W)), pl.BlockSpec((B_W, tm), lambda i: (0, i)),
                  pl.BlockSpec((A_KVW, tm), lambda i: (0, i))]
        transposed = {1024: (2, B_W), 2048: (3, A_KVW)}
    else:
        shapes.append(jax.ShapeDtypeStruct((t, KV32_W), F32))
        specs.append(pl.BlockSpec((tm, KV32_W), lambda i: (i, 0)))
    return _proj_call(x, g, gl, w_perm, wl, _ATT_PLAN, transposed, tuple(shapes), tuple(specs), tm, "att_proj")


M_GATES_W = 2 * M_QKW
_MLSTM_PLAN = (tuple((c, 512, ((0, c),)) for c in range(0, M_MAIN, 512))
               + tuple((M_MAIN + c, 512, ((1, c),)) for c in range(0, M_GATES_W, 512)))


def _mlstm_proj(x, g, gl, w_rep, wl, tm):
    t = x.shape[0]
    return _proj_call(
        x, g, gl, w_rep, wl, _MLSTM_PLAN, {},
        (jax.ShapeDtypeStruct((t, M_MAIN), BF16), jax.ShapeDtypeStruct((t, M_GATES_W), F32)),
        (pl.BlockSpec((tm, M_MAIN), lambda i: (i, 0)), pl.BlockSpec((tm, M_GATES_W), lambda i: (i, 0))),
        tm, "mlstm_proj")


def _memkv_kernel(m_ref, w_ref, o_ref):
    o_ref[0] = _dot(m_ref[...].astype(BF16), w_ref[0])


def _memkv_call(mem, w):
    n = w.shape[0]
    return pl.pallas_call(
        _memkv_kernel,
        grid=(n,),
        in_specs=[_const_spec(mem.shape), pl.BlockSpec((1, D_MODEL, D_MODEL), lambda i: (i, 0, 0))],
        out_specs=pl.BlockSpec((1, MEM_TOKENS, D_MODEL), lambda i: (i, 0, 0)),
        out_shape=jax.ShapeDtypeStruct((n, MEM_TOKENS, D_MODEL), F32),
        compiler_params=_params(1),
        name="mem_kv",
    )(mem, w)


def _low_half(n):
    return lax.broadcasted_iota(jnp.int32, (n, 128), 1) < HEAD_DIM


def _with_ones(v):
    return jnp.concatenate([v, jnp.ones((v.shape[0], 128), BF16)], axis=1)


def _one_head(v_pair, first):
    zero = jnp.zeros((), BF16)
    low = _low_half(v_pair.shape[0])
    return _with_ones(jnp.where(low, v_pair, zero) if first else jnp.where(low, zero, v_pair))


def _head_on_both_halves(v_pair, first):
    swapped = jnp.concatenate([v_pair[:, HEAD_DIM:], v_pair[:, :HEAD_DIM]], axis=1)
    low = _low_half(v_pair.shape[0])
    return _with_ones(jnp.where(low, v_pair, swapped) if first else jnp.where(low, swapped, v_pair))


def _attend(units):
    scores = [scores_fn() for scores_fn, _, _ in units]
    maxes = []
    for sc, (_, _, sink) in zip(scores, units):
        if all(s.shape == sc[0].shape for s in sc):
            mx = jnp.max(functools.reduce(jnp.maximum, sc), axis=-1, keepdims=True)
        else:
            mx = functools.reduce(jnp.maximum, [jnp.max(s, axis=-1, keepdims=True) for s in sc])
        maxes.append(mx if sink is None else jnp.maximum(mx, sink))
    exps = [[jnp.exp(s - mx).astype(BF16) for s in sc] for sc, mx in zip(scores, maxes)]
    outs = []
    for es, (_, values_fn, _) in zip(exps, units):
        outs.append(functools.reduce(lambda a, b: a + b, [_dot(e, v) for e, v in zip(es, values_fn())]))
    results = []
    for out, mx, (_, _, sink) in zip(outs, maxes, units):
        den = out[:, 128:256]
        if sink is not None:
            den = den + jnp.exp(sink - mx)
        results.append(out[:, 0:128] / den)
    return results


P_TILE = 4 * CHUNK
P_SUB = 2 * CHUNK
PB_KEYS = B_PREV + P_TILE
PA_KEYS = A_PREV + P_SUB


def _attn_prompt_kernel(qa_ref, qb_ref, kt2_ref, kt1_ref, kt0_ref, vb2_ref, vb1_ref, vb0_ref, kat_ref, katp_ref,
                        kva_ref, kvap_ref, bias_a_ref, sink_ref, bias_b_ref, o_ref):

    def unit_a(sb, g):
        q_rows = slice(sb * P_SUB, (sb + 1) * P_SUB)
        hd = slice(g * HEAD_DIM, (g + 1) * HEAD_DIM)
        if sb == 0:
            blocks = [(lambda: katp_ref[hd, :], None, 0, A_PREV),
                      (lambda: kat_ref[hd, 0:P_SUB], slice(0, P_SUB), A_PREV, P_SUB)]
        else:
            k_rows = slice(sb * P_SUB - A_PREV, (sb + 1) * P_SUB)
            blocks = [(lambda: kat_ref[hd, k_rows], k_rows, 0, PA_KEYS)]

        def scores():
            qs = jnp.concatenate([qa_ref[q_rows, h * HEAD_DIM:(h + 1) * HEAD_DIM]
                                  for h in range(g * A_GROUP, (g + 1) * A_GROUP)], axis=0)
            return [_dot(qs, kt()) + bias_a_ref[sb, g, :, c0:c0 + n] for kt, _, c0, n in blocks]

        def values():
            rows = [kvap_ref[:, A_KVW:2 * A_KVW] if r is None else kva_ref[r, A_KVW:2 * A_KVW] for _, r, _, _ in blocks]
            return [_head_on_both_halves(vv, g == 0) for vv in rows]

        return scores, values, sink_ref[g]

    kv_blocks = ((kt2_ref, vb2_ref), (kt1_ref, vb1_ref), (kt0_ref, vb0_ref))

    def unit_b(h):
        cols = slice(h * HEAD_DIM, (h + 1) * HEAD_DIM)
        pair_cols = slice((h - h % 2) * HEAD_DIM, (h - h % 2 + 2) * HEAD_DIM)

        def scores():
            q = qb_ref[:, cols]
            return [_dot(q, k_ref[cols, :]) + bias_b_ref[h, :, j * P_TILE:(j + 1) * P_TILE]
                    for j, (k_ref, _) in enumerate(kv_blocks)]

        def values():
            return [_one_head(v_ref[:, pair_cols], h % 2 == 0) for _, v_ref in kv_blocks]

        return scores, values, None

    a_ids = [(sb, g) for sb in range(P_TILE // P_SUB) for g in range(A_KV_HEADS)]
    res = _attend([unit_a(sb, g) for sb, g in a_ids] + [unit_b(h) for h in range(B_HEADS)])
    for (sb, g), o in zip(a_ids, res):
        q_rows = slice(sb * P_SUB, (sb + 1) * P_SUB)
        for j in range(0, A_GROUP, 2):
            pair = jnp.where(_low_half(P_SUB), o[j * P_SUB:(j + 1) * P_SUB], o[(j + 1) * P_SUB:(j + 2) * P_SUB])
            c0 = (g * A_GROUP + j) * HEAD_DIM
            o_ref[q_rows, c0:c0 + 2 * HEAD_DIM] = pair.astype(BF16)
    res_b = res[len(a_ids):]
    for h in range(0, B_HEADS, 2):
        o_ref[:, A_QW + h * HEAD_DIM:A_QW + (h + 2) * HEAD_DIM] = (res_b[h] + res_b[h + 1]).astype(BF16)


def _attn_prompt_call(proj, kbt, kat, bias_a, sink, bias_b):
    t = proj.shape[0]
    back2 = lambda i: jnp.maximum(i - 2, 0)
    back1 = lambda i: jnp.maximum(i - 1, 0)
    prev_a = lambda i: jnp.maximum(i * (P_TILE // A_PREV) - 1, 0)
    in_specs = [
        pl.BlockSpec((P_TILE, 512), lambda i: (i, 0)),
        pl.BlockSpec((P_TILE, 512), lambda i: (i, 1)),
        pl.BlockSpec((B_W, P_TILE), lambda i: (0, back2(i))),
        pl.BlockSpec((B_W, P_TILE), lambda i: (0, back1(i))),
        pl.BlockSpec((B_W, P_TILE), lambda i: (0, i)),
        pl.BlockSpec((P_TILE, 512), lambda i: (back2(i), 3)),
        pl.BlockSpec((P_TILE, 512), lambda i: (back1(i), 3)),
        pl.BlockSpec((P_TILE, 512), lambda i: (i, 3)),
        pl.BlockSpec((A_KVW, P_TILE), lambda i: (0, i)),
        pl.BlockSpec((A_KVW, A_PREV), lambda i: (0, prev_a(i))),
        pl.BlockSpec((P_TILE, 256), lambda i: (i, 8)),
        pl.BlockSpec((A_PREV, 256), lambda i: (prev_a(i), 8)),
        pl.BlockSpec((None,) + bias_a.shape[1:], lambda i: (jnp.minimum(i, 1), 0, 0, 0, 0)),
        _const_spec(sink.shape),
        pl.BlockSpec((None,) + bias_b.shape[1:], lambda i: (jnp.minimum(i, 2), 0, 0, 0)),
    ]
    return pl.pallas_call(
        _attn_prompt_kernel,
        grid=(t // P_TILE,),
        in_specs=in_specs,
        out_specs=pl.BlockSpec((P_TILE, D_MODEL), lambda i: (i, 0)),
        out_shape=jax.ShapeDtypeStruct((t, D_MODEL), BF16),
        compiler_params=_params(1),
        name="attn_prompt",
    )(proj, proj, kbt, kbt, kbt, proj, proj, proj, kat, kat, proj, proj, bias_a, sink, bias_b)


def _attn_sample_kernel(qa_ref, qb_ref, kb_ref, vb_ref, kva_ref, cakt_ref, cav_ref, cbkt_ref, cbv_ref,
                        bias_a_ref, sink_ref, bias_b_ref, o_ref, *, nb):
    def unit_a(b, g):
        rows = slice(b * CHUNK, (b + 1) * CHUNK)
        hd = slice(g * HEAD_DIM, (g + 1) * HEAD_DIM)

        def scores():
            qs = jnp.concatenate([qa_ref[rows, h * HEAD_DIM:(h + 1) * HEAD_DIM]
                                  for h in range(g * A_GROUP, (g + 1) * A_GROUP)], axis=0)
            return [_dot(qs, cakt_ref[b, g]) + bias_a_ref[g, :, 0:A_PREV],
                    _dot_nt(qs, kva_ref[rows, hd]) + bias_a_ref[g, :, A_PREV:A_BAND]]

        def values():
            return [_head_on_both_halves(cav_ref[b], g == 0),
                    _head_on_both_halves(kva_ref[rows, A_KVW:2 * A_KVW], g == 0)]

        return scores, values, sink_ref[g]

    def unit_b(b, h):
        rows = slice(b * CHUNK, (b + 1) * CHUNK)
        cols = slice(h * HEAD_DIM, (h + 1) * HEAD_DIM)
        pair_cols = slice((h - h % 2) * HEAD_DIM, (h - h % 2 + 2) * HEAD_DIM)

        def scores():
            q = qb_ref[rows, cols]
            return [_dot(q, cbkt_ref[b, h]) + bias_b_ref[h, :, 0:B_PREV],
                    _dot_nt(q, kb_ref[rows, cols]) + bias_b_ref[h, :, B_PREV:B_BAND]]

        def values():
            return [_one_head(cbv_ref[b, :, pair_cols], h % 2 == 0), _one_head(vb_ref[rows, pair_cols], h % 2 == 0)]

        return scores, values, None

    a_ids = [(b, g) for b in range(nb) for g in range(A_KV_HEADS)]
    b_ids = [(b, h) for b in range(nb) for h in range(B_HEADS)]
    res = _attend([unit_a(b, g) for b, g in a_ids] + [unit_b(b, h) for b, h in b_ids])
    for (b, g), o in zip(a_ids, res):
        rows = slice(b * CHUNK, (b + 1) * CHUNK)
        for j in range(0, A_GROUP, 2):
            pair = jnp.where(_low_half(CHUNK), o[j * CHUNK:(j + 1) * CHUNK], o[(j + 1) * CHUNK:(j + 2) * CHUNK])
            c0 = (g * A_GROUP + j) * HEAD_DIM
            o_ref[rows, c0:c0 + 2 * HEAD_DIM] = pair.astype(BF16)
    res_b = res[len(a_ids):]
    for i in range(0, len(b_ids), 2):
        b, h = b_ids[i]
        rows = slice(b * CHUNK, (b + 1) * CHUNK)
        o_ref[rows, A_QW + h * HEAD_DIM:A_QW + (h + 2) * HEAD_DIM] = (res_b[i] + res_b[i + 1]).astype(BF16)


def _attn_sample_call(proj, ca_kt, ca_v, cb_kt, cb_v, e, bias_a, sink, bias_b, nb):
    t = proj.shape[0]
    nbatch = t // CHUNK
    tm = nb * CHUNK
    in_specs = [
        pl.BlockSpec((tm, 512), lambda i: (i, 0)),
        pl.BlockSpec((tm, 512), lambda i: (i, 1)),
        pl.BlockSpec((tm, 512), lambda i: (i, 2)),
        pl.BlockSpec((tm, 512), lambda i: (i, 3)),
        pl.BlockSpec((tm, 256), lambda i: (i, 8)),
        pl.BlockSpec((None, nb, A_KV_HEADS, HEAD_DIM, A_PREV), lambda i: (e, i, 0, 0, 0)),
        pl.BlockSpec((None, nb, A_PREV, A_KVW), lambda i: (e, i, 0, 0)),
        pl.BlockSpec((None, nb, B_HEADS, HEAD_DIM, B_PREV), lambda i: (e, i, 0, 0, 0)),
        pl.BlockSpec((None, nb, B_PREV, B_W), lambda i: (e, i, 0, 0)),
        _const_spec(bias_a.shape), _const_spec(sink.shape), _const_spec(bias_b.shape),
    ]
    return pl.pallas_call(
        functools.partial(_attn_sample_kernel, nb=nb),
        grid=(nbatch // nb,),
        in_specs=in_specs,
        out_specs=pl.BlockSpec((tm, D_MODEL), lambda i: (i, 0)),
        out_shape=jax.ShapeDtypeStruct((t, D_MODEL), BF16),
        compiler_params=_params(1),
        name="attn_sample",
    )(proj, proj, proj, proj, proj, ca_kt, ca_v, cb_kt, cb_v, bias_a, sink, bias_b)


M_PAIRS = M_HEADS // 2
M_QK_PAIR = 2 * M_DK
M_V_PAIR = 2 * M_DV


def _split3(x):
    hi = x.astype(BF16)
    r1 = x - hi.astype(F32)
    mid = r1.astype(BF16)
    return hi, mid, (r1 - mid.astype(F32)).astype(BF16)


def _widen_heads(x):
    first = lax.broadcasted_iota(jnp.int32, (x.shape[0], M_QK_PAIR), 1) < M_DK
    outs = []
    for p in range(M_PAIRS):
        src = x[:, p * M_QK_PAIR:(p + 1) * M_QK_PAIR]
        swapped = pltpu.roll(src, M_DK, 1)
        outs += [jnp.where(first, src, swapped), jnp.where(first, swapped, src)]
    return jnp.concatenate(outs, axis=1)


def _mlstm_chunk(q, k, v, o, gates, c_s, n_s, m_s):
    L = CHUNK
    wide = (L, M_QKW)
    row = lax.broadcasted_iota(jnp.int32, wide, 0)
    pos = lax.broadcasted_iota(jnp.int32, wide, 1) % M_DK
    ig = gates[:, 0:M_QKW]
    fg = gates[:, M_QKW:2 * M_QKW]
    lf = jnp.minimum(fg, 0.0) - jnp.log(1.0 + jnp.exp(-jnp.abs(fg)))

    t_row = lax.broadcasted_iota(jnp.int32, (L, L), 0)
    t_col = lax.broadcasted_iota(jnp.int32, (L, L), 1)
    tril = jnp.where(t_col <= t_row, 1.0, 0.0).astype(BF16)
    hi, mid, lo = _split3(lf)
    b_c = _dot(tril, hi) + _dot(tril, mid) + _dot(tril, lo)
    r_c = ig - b_c
    r_row = jnp.sum(jnp.where(row == pos, r_c, 0.0), axis=0, keepdims=True)

    cm = r_c
    for sh in (1, 2, 4, 8, 16, 32):
        cm = jnp.maximum(cm, jnp.where(row >= sh, pltpu.roll(cm, sh, 0), NEG_INF))
    m_prev = m_s[...]
    mm = jnp.maximum(m_prev, cm)
    w = jnp.exp(jnp.where(pos <= row, r_row - mm, NEG_INF))
    g_inter = jnp.exp(m_prev - mm)
    qf = q.astype(F32)
    qg = g_inter * qf
    n_prev = n_s[...]

    lane_qk = lax.broadcasted_iota(jnp.int32, (L, M_QK_PAIR), 1)
    lane_v = lax.broadcasted_iota(jnp.int32, (L, M_V_PAIR), 1)
    zero_b = jnp.zeros((), BF16)
    bd_r = lax.broadcasted_iota(jnp.int32, (M_QK_PAIR, M_V_PAIR), 0) < M_DK
    bd_c = lax.broadcasted_iota(jnp.int32, (M_QK_PAIR, M_V_PAIR), 1) < M_DV
    on_diag = bd_r == bd_c
    ones_bd = jnp.where(on_diag, 1.0, 0.0).astype(BF16)

    s_parts, k_bd, v_bd = [], [], []
    for p in range(M_PAIRS):
        qk_l = slice(p * M_QK_PAIR, (p + 1) * M_QK_PAIR)
        v_l = slice(p * M_V_PAIR, (p + 1) * M_V_PAIR)
        kp, vp = k[:, qk_l], v[:, v_l]
        k_bd.append(jnp.concatenate([jnp.where(lane_qk < M_DK, kp, zero_b),
                                     jnp.where(lane_qk >= M_DK, kp, zero_b)], axis=0))
        v_bd.append(jnp.concatenate([jnp.where(lane_v < M_DV, vp, zero_b),
                                     jnp.where(lane_v >= M_DV, vp, zero_b)], axis=0))
        s_parts.append(_dot_nt(q[:, qk_l], k_bd[p]))
    sc = jnp.concatenate(s_parts, axis=1) * w
    den_in = sc + qg * n_prev
    d_hi, d_mid, _ = _split3(den_in)
    sc_b = sc.astype(BF16)
    qg_b = qg.astype(BF16)

    nums, dens = [], []
    for p in range(M_PAIRS):
        qk_l = slice(p * M_QK_PAIR, (p + 1) * M_QK_PAIR)
        lhs = jnp.concatenate([sc_b[:, qk_l], qg_b[:, qk_l]], axis=1)
        rhs = jnp.concatenate([v_bd[p], c_s[p].astype(BF16)], axis=0)
        nums.append(_dot(lhs, rhs))
        dens.append(_dot(d_hi[:, qk_l], ones_bd) + _dot(d_mid[:, qk_l], ones_bd))
    num = jnp.concatenate(nums, axis=1)
    den = jnp.concatenate(dens, axis=1)
    floor = _widen_heads(jnp.exp(-b_c - mm))
    hs = num / jnp.maximum(jnp.abs(den), floor)

    b_last = b_c[L - 1:L, :]
    mm_last = mm[L - 1:L, :]
    decay = jnp.exp(m_prev - mm_last)
    wk = jnp.exp(r_c - mm_last) * k.astype(F32)
    n_s[...] = decay * n_prev + jnp.sum(wk, axis=0, keepdims=True)
    m_s[...] = b_last + mm_last
    decay_v = _widen_heads(jnp.broadcast_to(decay, (8, M_QKW)))[0:1]
    wk_b = wk.astype(BF16)
    for p in range(M_PAIRS):
        qk_l = slice(p * M_QK_PAIR, (p + 1) * M_QK_PAIR)
        v_l = slice(p * M_V_PAIR, (p + 1) * M_V_PAIR)
        upd = _dot_tn(wk_b[:, qk_l], v[:, v_l])
        c_s[p] = decay_v[:, v_l] * c_s[p] + jnp.where(on_diag, upd, 0.0)

    of = o.astype(F32)
    return (hs * (1.0 / (1.0 + jnp.exp(-of)))).astype(BF16)


def _mlstm_kernel(q_ref, k_ref, v_ref, o_ref, gates_ref, bias_ref, c_in, n_in, m_in,
                  hg_ref, c_out, n_out, m_out, c_s, n_s, m_s, *, chunks):
    j = pl.program_id(1)

    @pl.when(j == 0)
    def _():
        c_s[...] = c_in[0]
        n_s[...] = n_in[0]
        m_s[...] = m_in[0]

    for c in range(chunks):
        rows = slice(c * CHUNK, (c + 1) * CHUNK)
        hg_ref[rows, :] = _mlstm_chunk(q_ref[rows, :], k_ref[rows, :], v_ref[rows, :], o_ref[rows, :],
                                       gates_ref[rows, :] + bias_ref[...], c_s, n_s, m_s)

    @pl.when(j == pl.num_programs(1) - 1)
    def _():
        c_out[0] = c_s[...]
        n_out[0] = n_s[...]
        m_out[0] = m_s[...]


def _mlstm_call(proj, gates, bias, c0, n0, m0, layer, chunks_per_step):
    t = proj.shape[0]
    nseq = c0.shape[1]
    rows = chunks_per_step * CHUNK
    nsteps = t // (nseq * rows)
    blk = lambda s, j: s * nsteps + j
    c_blk = (1, M_PAIRS, M_QK_PAIR, M_V_PAIR)
    r_blk = (1, 1, M_QKW)
    in_specs = [
        pl.BlockSpec((rows, M_QKW), lambda s, j: (blk(s, j), 0)),
        pl.BlockSpec((rows, M_QKW), lambda s, j: (blk(s, j), 1)),
        pl.BlockSpec((rows, M_VW), lambda s, j: (blk(s, j), 1)),
        pl.BlockSpec((rows, M_VW), lambda s, j: (blk(s, j), 2)),
        pl.BlockSpec((rows, 2 * M_QKW), lambda s, j: (blk(s, j), 0)),
        _const_spec((1, 2 * M_QKW)),
        pl.BlockSpec((None,) + c_blk, lambda s, j: (layer, s, 0, 0, 0)),
        pl.BlockSpec((None,) + r_blk, lambda s, j: (layer, s, 0, 0)),
        pl.BlockSpec((None,) + r_blk, lambda s, j: (layer, s, 0, 0)),
    ]
    out_specs = (
        pl.BlockSpec((rows, M_VW), lambda s, j: (blk(s, j), 0)),
        pl.BlockSpec(c_blk, lambda s, j: (s, 0, 0, 0)),
        pl.BlockSpec(r_blk, lambda s, j: (s, 0, 0)),
        pl.BlockSpec(r_blk, lambda s, j: (s, 0, 0)),
    )
    out_shape = (
        jax.ShapeDtypeStruct((t, M_VW), BF16),
        jax.ShapeDtypeStruct((nseq,) + c_blk[1:], F32),
        jax.ShapeDtypeStruct((nseq,) + r_blk[1:], F32),
        jax.ShapeDtypeStruct((nseq,) + r_blk[1:], F32),
    )
    return pl.pallas_call(
        functools.partial(_mlstm_kernel, chunks=chunks_per_step),
        grid=(nseq, nsteps),
        in_specs=in_specs,
        out_specs=out_specs,
        out_shape=out_shape,
        scratch_shapes=[pltpu.VMEM(c_blk[1:], F32), pltpu.VMEM(r_blk[1:], F32), pltpu.VMEM(r_blk[1:], F32)],
        compiler_params=_params(2),
        name="mlstm",
    )(proj, proj, proj, proj, gates, bias, c0, n0, m0)


def _mlstm_state_to_kernel(c, n, m):
    lead = c.shape[:-3]
    cp = c.reshape(lead + (M_PAIRS, 2, M_DK, M_DV))
    z = jnp.zeros_like(cp[..., 0, :, :])
    top = jnp.concatenate([cp[..., 0, :, :], z], axis=-1)
    bot = jnp.concatenate([z, cp[..., 1, :, :]], axis=-1)
    c_bd = jnp.concatenate([top, bot], axis=-2)
    n_row = n.reshape(lead + (1, M_QKW))
    m_row = jnp.repeat(m, M_DK, axis=-1).reshape(lead + (1, M_QKW))
    return c_bd, n_row, m_row


def _mlstm_state_from_kernel(c_bd, n_row, m_row):
    lead = c_bd.shape[:-3]
    top = c_bd[..., :M_DK, :M_DV]
    bot = c_bd[..., M_DK:, M_DV:]
    c = jnp.stack([top, bot], axis=-3).reshape(lead + (M_HEADS, M_DK, M_DV))
    n = n_row.reshape(lead + (M_HEADS, M_DK))
    m = m_row.reshape(lead + (M_HEADS, M_DK))[..., 0]
    return c, n, m


def _post_kernel(x_ref, a_ref, wout_ref, g_ref, wq_ref, mk_ref, mv_ref, wo_ref, o_ref, att_s, *, nb, tb):
    x1 = x_ref[...] + _dot(a_ref[...], wout_ref[...])
    q = _dot(_rms(x1, g_ref[...]).astype(BF16), wq_ref[...]).astype(BF16)
    scale = MEM_HD ** -0.5
    units = [(b, h) for b in range(nb) for h in range(MEM_HEADS)]
    where = lambda b, h: (slice(b * tb, (b + 1) * tb), slice(h * MEM_HD, (h + 1) * MEM_HD))
    scores = [_dot_nt(q[where(b, h)], mk_ref[b, h]) * scale for b, h in units]
    probs = []
    for s in scores:
        e = jnp.exp(s - jnp.max(s, axis=-1, keepdims=True))
        probs.append((e / jnp.sum(e, axis=-1, keepdims=True)).astype(BF16))
    for (b, h), p in zip(units, probs):
        att_s[where(b, h)] = _dot(p, mv_ref[b, h]).astype(BF16)
    o_ref[...] = x1 + _dot(att_s[...], wo_ref[...])


def _post_call(x, a, w_out, wl, g, w_q, mk, mv, w_o, layer, tm, tb):
    t = x.shape[0]
    nb = tm // tb
    shared = mk.shape[1] == 1
    tail = (0,) * (mk.ndim - 2)
    mem_map = (lambda i: (layer, 0) + tail) if shared else (lambda i: (layer, i) + tail)
    mem_blk = (None, nb) + mk.shape[2:]
    row = lambda i: (i, 0)
    in_specs = [
        pl.BlockSpec((tm, D_MODEL), row),
        pl.BlockSpec((tm, a.shape[1]), row),
        _layer_spec(w_out, wl),
        _layer_spec(g, layer),
        _layer_spec(w_q, layer),
        pl.BlockSpec(mem_blk, mem_map),
        pl.BlockSpec(mem_blk, mem_map),
        _layer_spec(w_o, layer),
    ]
    return pl.pallas_call(
        functools.partial(_post_kernel, nb=nb, tb=tb),
        grid=(t // tm,),
        in_specs=in_specs,
        out_specs=pl.BlockSpec((tm, D_MODEL), row),
        out_shape=jax.ShapeDtypeStruct((t, D_MODEL), F32),
        scratch_shapes=[pltpu.VMEM((tm, D_MODEL), BF16)],
        compiler_params=_params(1),
        name="post_cross",
    )(x, a, w_out, g, w_q, mk, mv, w_o)


def _conv3(u, cw, hists, seq_rows):
    w0, w1, w2 = cw[0:1], cw[1:2], cw[2:3]
    c = pltpu.roll(u, 2, 0) * w0 + pltpu.roll(u, 1, 0) * w1 + u * w2
    row = lax.broadcasted_iota(jnp.int32, (8, u.shape[1]), 0)
    pieces = []
    for s, hist in enumerate(hists):
        base = s * seq_rows
        t8 = u[base:base + 8]
        h0, h1 = hist[0:1], hist[1:2]
        u1 = jnp.where(row == 0, h1, pltpu.roll(t8, 1, 0))
        u2 = jnp.where(row == 0, h0, jnp.where(row == 1, h1, pltpu.roll(t8, 2, 0)))
        pieces += [u2 * w0 + u1 * w1 + t8 * w2, c[base + 8:base + seq_rows]]
    return jnp.concatenate(pieces, axis=0)


def _ffn_kernel(*refs, seq_rows, carry, final_norm):
    x_ref, g_ref, win_ref, cw_ref, wout_ref = refs[:5]
    pos = 5
    hist_ref = None
    if not carry:
        hist_ref = refs[pos]
        pos += 1
    gfin_ref = None
    if final_norm:
        gfin_ref = refs[pos]
        pos += 1
    o_ref, cs_ref, act_s = refs[pos:pos + 3]
    hist_s = refs[pos + 3] if carry else None

    tm = x_ref.shape[0]
    nseq = tm // seq_rows
    x = x_ref[...]
    h = _rms(x, g_ref[...]).astype(BF16)

    if carry:
        @pl.when(pl.program_id(0) == 0)
        def _():
            hist_s[...] = jnp.zeros_like(hist_s)

    for j in range(D_FF // FF_CHUNK):
        halves = []
        for half in range(2):
            c0 = half * D_FF + j * FF_CHUNK
            cols = slice(c0, c0 + FF_CHUNK)
            u = _dot(h, win_ref[:, cols])
            if carry:
                hists = [hist_s[:, cols]]
            else:
                hists = [hist_ref[s, :, cols] for s in range(nseq)]
            halves.append(_conv3(u, cw_ref[:, cols], hists, seq_rows))
            for s in range(nseq):
                last2 = u[(s + 1) * seq_rows - 2:(s + 1) * seq_rows]
                if carry:
                    hist_s[:, cols] = last2
                    cs_ref[:, cols] = last2
                else:
                    cs_ref[s, :, cols] = last2
        ca, cg = halves
        act = ca * (1.0 / (1.0 + jnp.exp(-ca))) * cg
        act_s[:, j * FF_CHUNK:(j + 1) * FF_CHUNK] = act.astype(BF16)

    y = x + _dot(act_s[...], wout_ref[...])
    if final_norm:
        y = _rms(y, gfin_ref[...])
    o_ref[...] = y


def _ffn_call(x, g, w_in, conv_w, w_out, layer, hist, g_final, tm):
    t = x.shape[0]
    carry = hist is None
    seq_rows = tm if carry else CHUNK
    nseq = tm // seq_rows
    row = lambda i: (i, 0)
    args = [x, g, w_in, conv_w, w_out]
    in_specs = [pl.BlockSpec((tm, D_MODEL), row), _layer_spec(g, layer),
                _layer_spec(w_in, layer), _layer_spec(conv_w, layer), _layer_spec(w_out, layer)]
    if carry:
        cs_shape, cs_spec = (2, 2 * D_FF), _const_spec((2, 2 * D_FF))
    else:
        args.append(hist)
        in_specs.append(pl.BlockSpec((None, nseq, 2, 2 * D_FF), lambda i: (layer, i, 0, 0)))
        cs_shape, cs_spec = hist.shape[1:], pl.BlockSpec((nseq, 2, 2 * D_FF), lambda i: (i, 0, 0))
    if g_final is not None:
        args.append(g_final.reshape(1, D_MODEL))
        in_specs.append(_const_spec((1, D_MODEL)))
    scratch = [pltpu.VMEM((tm, D_FF), BF16)]
    if carry:
        scratch.append(pltpu.VMEM((2, 2 * D_FF), F32))
    return pl.pallas_call(
        functools.partial(_ffn_kernel, seq_rows=seq_rows, carry=carry, final_norm=g_final is not None),
        grid=(t // tm,),
        in_specs=in_specs,
        out_specs=(pl.BlockSpec((tm, D_MODEL), row), cs_spec),
        out_shape=(jax.ShapeDtypeStruct((t, D_MODEL), F32), jax.ShapeDtypeStruct(cs_shape, F32)),
        scratch_shapes=scratch,
        compiler_params=_params(1),
        name="conv_ffn",
    )(*args)


def _alibi(n_q, n_k):
    q = np.arange(n_q)[:, None]
    dist = np.abs(q + A_PREV - np.arange(n_k)[None, :]).astype(np.float32)
    slopes = (2.0 ** (-8.0 * np.arange(1, A_HEADS + 1) / A_HEADS)).astype(np.float32)
    return -slopes[:, None, None] * dist[None]


def _band_mask(n_q, n_k, n_prev_chunks, first_valid_chunk):
    qc = (np.arange(n_q) // CHUNK)[:, None]
    kc = (np.arange(n_k) // CHUNK)[None, :]
    ok = (kc >= qc) & (kc <= qc + n_prev_chunks) & (kc >= first_valid_chunk)
    return np.where(ok, 0.0, NEG_INF).astype(np.float32)


def _alibi_sample_table():
    return jnp.asarray(_alibi(CHUNK, A_BAND).reshape(A_KV_HEADS, A_GROUP * CHUNK, A_BAND))


def _alibi_prompt_tables():
    n_prev = A_PREV // CHUNK
    base = _alibi(P_SUB, PA_KEYS)
    general = (base + _band_mask(P_SUB, PA_KEYS, n_prev, 0)).reshape(A_KV_HEADS, A_GROUP * P_SUB, PA_KEYS)
    first = (base + _band_mask(P_SUB, PA_KEYS, n_prev, n_prev)).reshape(A_KV_HEADS, A_GROUP * P_SUB, PA_KEYS)
    return jnp.asarray(np.stack([np.stack([first, general]), np.stack([general, general])]))


def _relpos_toeplitz(table, n_q, n_k):
    h, n_rel = table.shape
    n_lo = n_k - B_BAND
    n_hi = n_q + n_k - 1 - n_lo - n_rel
    ext = jnp.concatenate([jnp.broadcast_to(table[:, :1], (h, n_lo + 1)), table,
                           jnp.broadcast_to(table[:, -1:], (h, n_hi))], axis=1)
    rev = ext[:, ::-1][:, None, :]
    length = n_q + n_k - 1
    tiled = jnp.broadcast_to(rev, (h, n_q, length + 1)).reshape(h, n_q * (length + 1))
    skew = tiled[:, :n_q * length].reshape(h, n_q, length)
    return skew[:, :, n_q - 1:n_q - 1 + n_k]


def _relpos_prompt_tables(table):
    n_prev = B_PREV // CHUNK
    masks = np.stack([_band_mask(P_TILE, PB_KEYS, n_prev, max(n_prev - t * (P_TILE // CHUNK), 0)) for t in range(3)])
    return _relpos_toeplitz(table, P_TILE, PB_KEYS)[None] + jnp.asarray(masks)[:, None]


def _row_tile(t):
    return PROMPT_TILE if t % PROMPT_TILE == 0 else 256


def kernel(x_prompt, x_sample, mem_prompt, cache_a_k, cache_a_v, cache_b_k, cache_b_v, state_mlstm_c, state_mlstm_n, state_mlstm_m, cache_mem_k, cache_mem_v, state_ffn_conv, norm_mix, norm_cross, norm_ffn, norm_final, w_in_att, w_out_att, sink_a, relpos_b, w_in_mlstm, b_igate, b_fgate, w_out_mlstm, w_mem_q, w_mem_k, w_mem_v, w_mem_o, w_ffn_in, conv_ffn, w_ffn_out):
    depth = norm_mix.shape[0]
    assert x_prompt.shape[0] == 1
    tp = x_prompt.shape[1]
    nb, ds = x_sample.shape[0], x_sample.shape[1]
    assert ds == CHUNK and tp % PROMPT_TILE == 0
    ts = nb * ds
    tm_p, tm_s = PROMPT_TILE, _row_tile(ts)
    tm_big = WIDE_TILE if tp % WIDE_TILE == 0 else tm_p
    xp = x_prompt[0]
    xs = x_sample.reshape(ts, D_MODEL)

    memkv = _memkv_call(mem_prompt[0], jnp.concatenate([w_mem_k, w_mem_v], axis=0).astype(BF16))
    head_major = lambda m: jnp.transpose(m, (0, 1, 3, 2, 4)).astype(BF16)
    split_heads = lambda m: m.reshape(depth, 1, MEM_TOKENS, MEM_HEADS, MEM_HD)
    p_mk, p_mv = head_major(split_heads(memkv[:depth])), head_major(split_heads(memkv[depth:]))
    s_mk, s_mv = head_major(cache_mem_k), head_major(cache_mem_v)

    g_mix = norm_mix.astype(F32)[:, None]
    g_cross = norm_cross.astype(F32)[:, None]
    g_ffn = norm_ffn.astype(F32)[:, None]
    wa = w_in_att
    scale = HEAD_DIM ** -0.5
    w_att = jnp.concatenate([wa[:, :, 0:512] * scale, wa[:, :, 768:1280] * scale, wa[:, :, 1280:2304],
                             wa[:, :, 512:768]], axis=2).astype(BF16)
    wm = w_in_mlstm
    k_scale = M_DK ** -0.5
    w_mls = jnp.concatenate([wm[:, :, 0:M_QKW], wm[:, :, M_QKW:2 * M_QKW] * k_scale, wm[:, :, 2 * M_QKW:M_MAIN],
                             jnp.repeat(wm[:, :, M_MAIN:], M_DK, axis=2)], axis=2).astype(BF16)
    w_out_a, w_out_m = w_out_att.astype(BF16), w_out_mlstm.astype(BF16)
    w_q, w_o = w_mem_q.astype(BF16), w_mem_o.astype(BF16)
    w_fi, w_fo = w_ffn_in.astype(BF16), w_ffn_out.astype(BF16)
    conv_w = conv_ffn.astype(F32)
    hist_s = state_ffn_conv.astype(F32)
    n_even = cache_a_k.shape[0]
    ca_kt = jnp.transpose(cache_a_k, (0, 1, 3, 4, 2)).astype(BF16)
    cb_kt = jnp.transpose(cache_b_k, (0, 1, 3, 4, 2)).astype(BF16)
    ca_v = cache_a_v.reshape(n_even, nb, A_PREV, A_KVW).astype(BF16)
    cb_v = cache_b_v.reshape(n_even, nb, B_PREV, B_W).astype(BF16)
    st_c, st_n, st_m = _mlstm_state_to_kernel(state_mlstm_c.astype(F32), state_mlstm_n.astype(F32),
                                              state_mlstm_m.astype(F32))
    zc = jnp.zeros((1, 1, M_PAIRS, M_QK_PAIR, M_V_PAIR), F32)
    zn = jnp.zeros((1, 1, 1, M_QKW), F32)
    zm = jnp.zeros((1, 1, 1, M_QKW), F32)

    bias_a_s = _alibi_sample_table()
    bias_a_p = _alibi_prompt_tables()
    post_tm_s = tm_s
    attn_nb = 8 if nb % 8 == 0 else 4

    p_ak, p_av, p_bk, p_bv, p_c, p_n, p_m, p_conv = [], [], [], [], [], [], [], []
    s_ak, s_av, s_bk, s_bv, s_c, s_n, s_m, s_conv = [], [], [], [], [], [], [], []
    for l in range(depth):
        if l % 2 == 0:
            e = l // 2
            proj_p, kv_p, kbt_p, kat_p = _att_proj(xp, g_mix, l, w_att, e, tm_p, prompt=True)
            proj_s, kv_s = _att_proj(xs, g_mix, l, w_att, e, tm_s, prompt=False)
            sk = sink_a[e].astype(F32)
            sink_p = jnp.repeat(sk, P_SUB).reshape(A_KV_HEADS, A_GROUP * P_SUB, 1)
            sink_s = jnp.repeat(sk, CHUNK).reshape(A_KV_HEADS, A_GROUP * CHUNK, 1)
            rel = relpos_b[e].astype(F32)
            mix_p = _attn_prompt_call(proj_p, kbt_p, kat_p, bias_a_p, sink_p, _relpos_prompt_tables(rel))
            mix_s = _attn_sample_call(proj_s, ca_kt, ca_v, cb_kt, cb_v, e, bias_a_s, sink_s,
                                      _relpos_toeplitz(rel, CHUNK, B_BAND), attn_nb)
            w_mix_out, wl = w_out_a, e
            p_ak.append(kv_p[tm_p - A_PREV:, 0:128].reshape(1, A_PREV, A_KV_HEADS, HEAD_DIM))
            p_av.append(kv_p[tm_p - A_PREV:, 128:256].reshape(1, A_PREV, A_KV_HEADS, HEAD_DIM))
            p_bk.append(kv_p[:, 256:768].reshape(1, B_PREV, B_HEADS, HEAD_DIM))
            p_bv.append(kv_p[:, 768:1280].reshape(1, B_PREV, B_HEADS, HEAD_DIM))
            s_ak.append(kv_s[:, 0:128].reshape(nb, ds, A_KV_HEADS, HEAD_DIM))
            s_av.append(kv_s[:, 128:256].reshape(nb, ds, A_KV_HEADS, HEAD_DIM))
            s_bk.append(kv_s[:, 256:768].reshape(nb, ds, B_HEADS, HEAD_DIM))
            s_bv.append(kv_s[:, 768:1280].reshape(nb, ds, B_HEADS, HEAD_DIM))
        else:
            o = l // 2
            gate_bias = jnp.concatenate([jnp.repeat(b_igate[o].astype(F32), M_DK),
                                         jnp.repeat(b_fgate[o].astype(F32), M_DK)]).reshape(1, M_GATES_W)
            proj_p, gates_p = _mlstm_proj(xp, g_mix, l, w_mls, o, tm_p)
            proj_s, gates_s = _mlstm_proj(xs, g_mix, l, w_mls, o, tm_s)
            mix_p, c, n, m = _mlstm_call(proj_p, gates_p, gate_bias, zc, zn, zm, 0, 4)
            c, n, m = _mlstm_state_from_kernel(c, n, m)
            p_c.append(c); p_n.append(n); p_m.append(m)
            mix_s, c, n, m = _mlstm_call(proj_s, gates_s, gate_bias, st_c, st_n, st_m, o, 1)
            c, n, m = _mlstm_state_from_kernel(c, n, m)
            s_c.append(c); s_n.append(n); s_m.append(m)
            w_mix_out, wl = w_out_m, o

        xp = _post_call(xp, mix_p, w_mix_out, wl, g_cross, w_q, p_mk, p_mv, w_o, l, tm_big, tm_big)
        xs = _post_call(xs, mix_s, w_mix_out, wl, g_cross, w_q, s_mk, s_mv, w_o, l, post_tm_s, CHUNK)

        g_fin = norm_final if l == depth - 1 else None
        xp, cv = _ffn_call(xp, g_ffn, w_fi, conv_w, w_fo, l, None, g_fin, tm_big)
        p_conv.append(cv[None])
        xs, cv = _ffn_call(xs, g_ffn, w_fi, conv_w, w_fo, l, hist_s, g_fin, tm_s)
        s_conv.append(cv)

    p_mk, p_mv = memkv[:depth], memkv[depth:]
    mem_shape = (depth, 1, MEM_TOKENS, MEM_HEADS, MEM_HD)
    return (xp[None], xs.reshape(nb, ds, D_MODEL),
            jnp.stack(p_ak), jnp.stack(p_av), jnp.stack(p_bk), jnp.stack(p_bv),
            jnp.stack(p_c), jnp.stack(p_n), jnp.stack(p_m),
            p_mk.reshape(mem_shape), p_mv.reshape(mem_shape), jnp.stack(p_conv),
            jnp.stack(s_ak), jnp.stack(s_av), jnp.stack(s_bk), jnp.stack(s_bv),
            jnp.stack(s_c), jnp.stack(s_n), jnp.stack(s_m), jnp.stack(s_conv))
```

```python
import functools

import numpy as np
import jax
import jax.numpy as jnp
from jax import lax
from jax.experimental import pallas as pl
from jax.experimental.pallas import tpu as pltpu

F32 = jnp.float32
BF16 = jnp.bfloat16

D_MODEL = 1024
CHUNK = 64
HEAD_DIM = 64
A_HEADS = 8
A_KV_HEADS = 2
A_GROUP = A_HEADS // A_KV_HEADS
A_BAND = 3 * CHUNK
A_PREV = 2 * CHUNK
B_HEADS = 8
B_BAND = 9 * CHUNK
B_PREV = 8 * CHUNK
REL_CLIP = 128
A_QW = A_HEADS * HEAD_DIM
A_KVW = A_KV_HEADS * HEAD_DIM
B_W = B_HEADS * HEAD_DIM
ATT_PROJ = A_QW + 2 * A_KVW + 3 * B_W
KV32_W = 2 * A_KVW + 2 * B_W
M_HEADS = 8
M_DK = 64
M_DV = 128
M_QKW = M_HEADS * M_DK
M_VW = M_HEADS * M_DV
M_MAIN = 2 * M_QKW + 2 * M_VW
MEM_TOKENS = 256
MEM_HEADS = 4
MEM_HD = D_MODEL // MEM_HEADS
D_FF = 2816
FF_CHUNK = 256
EPS = 1e-6
NEG_INF = float("-inf")

VMEM_LIMIT_BYTES = 56 * 1024 * 1024
PROMPT_TILE = 512
WIDE_TILE = 1024


def _params(n_axes):
    return pltpu.CompilerParams(dimension_semantics=("arbitrary",) * n_axes,
                                vmem_limit_bytes=VMEM_LIMIT_BYTES)


def _const_spec(shape):
    zeros = (0,) * len(shape)
    return pl.BlockSpec(shape, lambda *_: zeros)


def _layer_spec(stack, layer):
    idx = (layer,) + (0,) * (stack.ndim - 1)
    return pl.BlockSpec((None,) + stack.shape[1:], lambda *_: idx, pipeline_mode=pl.Buffered(1))


def _rms(x, g):
    return x * lax.rsqrt(jnp.mean(x * x, axis=-1, keepdims=True) + EPS) * g


def _dot(a, b):
    return jnp.dot(a, b, preferred_element_type=F32)


def _dot_nt(a, b):
    return lax.dot_general(a, b, (((1,), (1,)), ((), ())), preferred_element_type=F32)


def _dot_tn(a, b):
    return lax.dot_general(a, b, (((0,), (0,)), ((), ())), preferred_element_type=F32)


def _proj_kernel(x_ref, g_ref, w_ref, *out_refs, plan, transposed, replicated):
    h = _rms(x_ref[...], g_ref[...]).astype(BF16)
    for c0, cw, dsts in plan:
        r = _dot(h, w_ref[:, c0:c0 + cw])
        for oi, oc in dsts:
            out_refs[oi][:, oc:oc + cw] = r.astype(out_refs[oi].dtype)
        if c0 in transposed:
            oi, width = transposed[c0]
            out_refs[oi][...] = r[:, :width].T.astype(out_refs[oi].dtype)
        if c0 in replicated:
            oi, n_cols = replicated[c0]
            first = lax.broadcasted_iota(jnp.int32, (r.shape[0], 128), 1) < 64
            for j in range(0, n_cols, 2):
                out_refs[oi][:, j * 64:(j + 2) * 64] = jnp.where(first, r[:, j:j + 1], r[:, j + 1:j + 2])


def _proj_call(x, g, gl, w, wl, plan, transposed, replicated, out_shapes, out_specs, tm, name):
    t = x.shape[0]
    return pl.pallas_call(
        functools.partial(_proj_kernel, plan=plan, transposed=transposed, replicated=replicated),
        grid=(t // tm,),
        in_specs=[pl.BlockSpec((tm, D_MODEL), lambda i: (i, 0)),
                  _layer_spec(g, gl),
                  _layer_spec(w, wl)],
        out_specs=out_specs,
        out_shape=out_shapes,
        compiler_params=_params(1),
        name=name,
    )(x, g, w)


_ATT_PLAN = (
    (2048, 256, ((0, 2048), (1, 0))),
    (1024, 512, ((0, 1024), (1, 256))),
    (1536, 512, ((0, 1536), (1, 768))),
    (0, 512, ((0, 0),)),
    (512, 512, ((0, 512),)),
)


def _att_proj(x, g, gl, w_perm, wl, tm, prompt):
    t = x.shape[0]
    shapes = [jax.ShapeDtypeStruct((t, ATT_PROJ), BF16)]
    specs = [pl.BlockSpec((tm, ATT_PROJ), lambda i: (i, 0))]
    transposed = {}
    if prompt:
        shapes += [jax.ShapeDtypeStruct((tm, KV32_W), F32), jax.ShapeDtypeStruct((B_W, t), BF16),
                   jax.ShapeDtypeStruct((A_KVW, t), BF16)]
        specs += [_const_spec((tm, KV32_W)), pl.BlockSpec((B_W, tm), lambda i: (0, i)),
                  pl.BlockSpec((A_KVW, tm), lambda i: (0, i))]
        transposed = {1024: (2, B_W), 2048: (3, A_KVW)}
    else:
        shapes.append(jax.ShapeDtypeStruct((t, KV32_W), F32))
        specs.append(pl.BlockSpec((tm, KV32_W), lambda i: (i, 0)))
    return _proj_call(x, g, gl, w_perm, wl, _ATT_PLAN, transposed, {}, tuple(shapes), tuple(specs), tm, "att_proj")


M_GATES_W = 2 * M_QKW
_MLSTM_PLAN = ((M_MAIN, 128, ()),) + tuple((c, 512, ((0, c),)) for c in range(0, M_MAIN, 512))


def _mlstm_proj(x, g, gl, w_pad, wl, tm):
    t = x.shape[0]
    return _proj_call(
        x, g, gl, w_pad, wl, _MLSTM_PLAN, {}, {M_MAIN: (1, 2 * M_HEADS)},
        (jax.ShapeDtypeStruct((t, M_MAIN), BF16), jax.ShapeDtypeStruct((t, M_GATES_W), F32)),
        (pl.BlockSpec((tm, M_MAIN), lambda i: (i, 0)), pl.BlockSpec((tm, M_GATES_W), lambda i: (i, 0))),
        tm, "mlstm_proj")


def _memkv_kernel(m_ref, w_ref, o_ref):
    o_ref[0] = _dot(m_ref[...].astype(BF16), w_ref[0])


def _memkv_call(mem, w):
    n = w.shape[0]
    return pl.pallas_call(
        _memkv_kernel,
        grid=(n,),
        in_specs=[_const_spec(mem.shape), pl.BlockSpec((1, D_MODEL, D_MODEL), lambda i: (i, 0, 0))],
        out_specs=pl.BlockSpec((1, MEM_TOKENS, D_MODEL), lambda i: (i, 0, 0)),
        out_shape=jax.ShapeDtypeStruct((n, MEM_TOKENS, D_MODEL), F32),
        compiler_params=_params(1),
        name="mem_kv",
    )(mem, w)


def _low_half(n):
    return lax.broadcasted_iota(jnp.int32, (n, 128), 1) < HEAD_DIM


def _with_ones(v):
    return jnp.concatenate([v, jnp.ones((v.shape[0], 128), BF16)], axis=1)


def _one_head(v_pair, first):
    zero = jnp.zeros((), BF16)
    low = _low_half(v_pair.shape[0])
    return _with_ones(jnp.where(low, v_pair, zero) if first else jnp.where(low, zero, v_pair))


def _head_on_both_halves(v_pair, first):
    swapped = jnp.concatenate([v_pair[:, HEAD_DIM:], v_pair[:, :HEAD_DIM]], axis=1)
    low = _low_half(v_pair.shape[0])
    return _with_ones(jnp.where(low, v_pair, swapped) if first else jnp.where(low, swapped, v_pair))


def _attend(units):
    scores = [scores_fn() for scores_fn, _, _ in units]
    maxes = []
    for sc, (_, _, sink) in zip(scores, units):
        if all(s.shape == sc[0].shape for s in sc):
            mx = jnp.max(functools.reduce(jnp.maximum, sc), axis=-1, keepdims=True)
        else:
            mx = functools.reduce(jnp.maximum, [jnp.max(s, axis=-1, keepdims=True) for s in sc])
        maxes.append(mx if sink is None else jnp.maximum(mx, sink))
    exps = [[jnp.exp(s - mx).astype(BF16) for s in sc] for sc, mx in zip(scores, maxes)]
    outs = []
    for es, (_, values_fn, _) in zip(exps, units):
        outs.append(functools.reduce(lambda a, b: a + b, [_dot(e, v) for e, v in zip(es, values_fn())]))
    results = []
    for out, mx, (_, _, sink) in zip(outs, maxes, units):
        den = out[:, 128:256]
        if sink is not None:
            den = den + jnp.exp(sink - mx)
        results.append(out[:, 0:128] / den)
    return results


P_TILE = 4 * CHUNK
P_SUB = 2 * CHUNK
PB_KEYS = B_PREV + P_TILE
PA_KEYS = A_PREV + P_SUB


def _attn_prompt_kernel(qa_ref, qb_ref, kt2_ref, kt1_ref, kt0_ref, vb2_ref, vb1_ref, vb0_ref, kat_ref, katp_ref,
                        kva_ref, kvap_ref, bias_a_ref, sink_ref, bias_b_ref, o_ref):

    def unit_a(sb, g):
        q_rows = slice(sb * P_SUB, (sb + 1) * P_SUB)
        hd = slice(g * HEAD_DIM, (g + 1) * HEAD_DIM)
        if sb == 0:
            blocks = [(lambda: katp_ref[hd, :], None, 0, A_PREV),
                      (lambda: kat_ref[hd, 0:P_SUB], slice(0, P_SUB), A_PREV, P_SUB)]
        else:
            k_rows = slice(sb * P_SUB - A_PREV, (sb + 1) * P_SUB)
            blocks = [(lambda: kat_ref[hd, k_rows], k_rows, 0, PA_KEYS)]

        def scores():
            qs = jnp.concatenate([qa_ref[q_rows, h * HEAD_DIM:(h + 1) * HEAD_DIM]
                                  for h in range(g * A_GROUP, (g + 1) * A_GROUP)], axis=0)
            return [_dot(qs, kt()) + bias_a_ref[sb, g, :, c0:c0 + n] for kt, _, c0, n in blocks]

        def values():
            rows = [kvap_ref[:, A_KVW:2 * A_KVW] if r is None else kva_ref[r, A_KVW:2 * A_KVW] for _, r, _, _ in blocks]
            return [_head_on_both_halves(vv, g == 0) for vv in rows]

        return scores, values, sink_ref[g]

    kv_blocks = ((kt2_ref, vb2_ref), (kt1_ref, vb1_ref), (kt0_ref, vb0_ref))

    def unit_b(h):
        cols = slice(h * HEAD_DIM, (h + 1) * HEAD_DIM)
        pair_cols = slice((h - h % 2) * HEAD_DIM, (h - h % 2 + 2) * HEAD_DIM)

        def scores():
            q = qb_ref[:, cols]
            return [_dot(q, k_ref[cols, :]) + bias_b_ref[h, :, j * P_TILE:(j + 1) * P_TILE]
                    for j, (k_ref, _) in enumerate(kv_blocks)]

        def values():
            return [_one_head(v_ref[:, pair_cols], h % 2 == 0) for _, v_ref in kv_blocks]

        return scores, values, None

    a_ids = [(sb, g) for sb in range(P_TILE // P_SUB) for g in range(A_KV_HEADS)]
    res = _attend([unit_a(sb, g) for sb, g in a_ids] + [unit_b(h) for h in range(B_HEADS)])
    for (sb, g), o in zip(a_ids, res):
        q_rows = slice(sb * P_SUB, (sb + 1) * P_SUB)
        for j in range(0, A_GROUP, 2):
            pair = jnp.where(_low_half(P_SUB), o[j * P_SUB:(j + 1) * P_SUB], o[(j + 1) * P_SUB:(j + 2) * P_SUB])
            c0 = (g * A_GROUP + j) * HEAD_DIM
            o_ref[q_rows, c0:c0 + 2 * HEAD_DIM] = pair.astype(BF16)
    res_b = res[len(a_ids):]
    for h in range(0, B_HEADS, 2):
        o_ref[:, A_QW + h * HEAD_DIM:A_QW + (h + 2) * HEAD_DIM] = (res_b[h] + res_b[h + 1]).astype(BF16)


def _attn_prompt_call(proj, kbt, kat, bias_a, sink, bias_b):
    t = proj.shape[0]
    back2 = lambda i: jnp.maximum(i - 2, 0)
    back1 = lambda i: jnp.maximum(i - 1, 0)
    prev_a = lambda i: jnp.maximum(i * (P_TILE // A_PREV) - 1, 0)
    in_specs = [
        pl.BlockSpec((P_TILE, 512), lambda i: (i, 0)),
        pl.BlockSpec((P_TILE, 512), lambda i: (i, 1)),
        pl.BlockSpec((B_W, P_TILE), lambda i: (0, back2(i))),
        pl.BlockSpec((B_W, P_TILE), lambda i: (0, back1(i))),
        pl.BlockSpec((B_W, P_TILE), lambda i: (0, i)),
        pl.BlockSpec((P_TILE, 512), lambda i: (back2(i), 3)),
        pl.BlockSpec((P_TILE, 512), lambda i: (back1(i), 3)),
        pl.BlockSpec((P_TILE, 512), lambda i: (i, 3)),
        pl.BlockSpec((A_KVW, P_TILE), lambda i: (0, i)),
        pl.BlockSpec((A_KVW, A_PREV), lambda i: (0, prev_a(i))),
        pl.BlockSpec((P_TILE, 256), lambda i: (i, 8)),
        pl.BlockSpec((A_PREV, 256), lambda i: (prev_a(i), 8)),
        pl.BlockSpec((None,) + bias_a.shape[1:], lambda i: (jnp.minimum(i, 1), 0, 0, 0, 0)),
        _const_spec(sink.shape),
        pl.BlockSpec((None,) + bias_b.shape[1:], lambda i: (jnp.minimum(i, 2), 0, 0, 0)),
    ]
    return pl.pallas_call(
        _attn_prompt_kernel,
        grid=(t // P_TILE,),
        in_specs=in_specs,
        out_specs=pl.BlockSpec((P_TILE, D_MODEL), lambda i: (i, 0)),
        out_shape=jax.ShapeDtypeStruct((t, D_MODEL), BF16),
        compiler_params=_params(1),
        name="attn_prompt",
    )(proj, proj, kbt, kbt, kbt, proj, proj, proj, kat, kat, proj, proj, bias_a, sink, bias_b)


def _attn_sample_kernel(qa_ref, qb_ref, kb_ref, vb_ref, kva_ref, cakt_ref, cav_ref, cbkt_ref, cbv_ref,
                        bias_a_ref, sink_ref, bias_b_ref, o_ref, *, nb):
    def unit_a(b, g):
        rows = slice(b * CHUNK, (b + 1) * CHUNK)
        hd = slice(g * HEAD_DIM, (g + 1) * HEAD_DIM)

        def scores():
            qs = jnp.concatenate([qa_ref[rows, h * HEAD_DIM:(h + 1) * HEAD_DIM]
                                  for h in range(g * A_GROUP, (g + 1) * A_GROUP)], axis=0)
            return [_dot(qs, cakt_ref[b, g]) + bias_a_ref[g, :, 0:A_PREV],
                    _dot_nt(qs, kva_ref[rows, hd]) + bias_a_ref[g, :, A_PREV:A_BAND]]

        def values():
            return [_head_on_both_halves(cav_ref[b], g == 0),
                    _head_on_both_halves(kva_ref[rows, A_KVW:2 * A_KVW], g == 0)]

        return scores, values, sink_ref[g]

    def unit_b(b, h):
        rows = slice(b * CHUNK, (b + 1) * CHUNK)
        cols = slice(h * HEAD_DIM, (h + 1) * HEAD_DIM)
        pair_cols = slice((h - h % 2) * HEAD_DIM, (h - h % 2 + 2) * HEAD_DIM)

        def scores():
            q = qb_ref[rows, cols]
            return [_dot(q, cbkt_ref[b, h]) + bias_b_ref[h, :, 0:B_PREV],
                    _dot_nt(q, kb_ref[rows, cols]) + bias_b_ref[h, :, B_PREV:B_BAND]]

        def values():
            return [_one_head(cbv_ref[b, :, pair_cols], h % 2 == 0), _one_head(vb_ref[rows, pair_cols], h % 2 == 0)]

        return scores, values, None

    a_ids = [(b, g) for b in range(nb) for g in range(A_KV_HEADS)]
    b_ids = [(b, h) for b in range(nb) for h in range(B_HEADS)]
    res = _attend([unit_a(b, g) for b, g in a_ids] + [unit_b(b, h) for b, h in b_ids])
    for (b, g), o in zip(a_ids, res):
        rows = slice(b * CHUNK, (b + 1) * CHUNK)
        for j in range(0, A_GROUP, 2):
            pair = jnp.where(_low_half(CHUNK), o[j * CHUNK:(j + 1) * CHUNK], o[(j + 1) * CHUNK:(j + 2) * CHUNK])
            c0 = (g * A_GROUP + j) * HEAD_DIM
            o_ref[rows, c0:c0 + 2 * HEAD_DIM] = pair.astype(BF16)
    res_b = res[len(a_ids):]
    for i in range(0, len(b_ids), 2):
        b, h = b_ids[i]
        rows = slice(b * CHUNK, (b + 1) * CHUNK)
        o_ref[rows, A_QW + h * HEAD_DIM:A_QW + (h + 2) * HEAD_DIM] = (res_b[i] + res_b[i + 1]).astype(BF16)


def _attn_sample_call(proj, ca_kt, ca_v, cb_kt, cb_v, e, bias_a, sink, bias_b, nb):
    t = proj.shape[0]
    nbatch = t // CHUNK
    tm = nb * CHUNK
    in_specs = [
        pl.BlockSpec((tm, 512), lambda i: (i, 0)),
        pl.BlockSpec((tm, 512), lambda i: (i, 1)),
        pl.BlockSpec((tm, 512), lambda i: (i, 2)),
        pl.BlockSpec((tm, 512), lambda i: (i, 3)),
        pl.BlockSpec((tm, 256), lambda i: (i, 8)),
        pl.BlockSpec((None, nb, A_KV_HEADS, HEAD_DIM, A_PREV), lambda i: (e, i, 0, 0, 0)),
        pl.BlockSpec((None, nb, A_PREV, A_KVW), lambda i: (e, i, 0, 0)),
        pl.BlockSpec((None, nb, B_HEADS, HEAD_DIM, B_PREV), lambda i: (e, i, 0, 0, 0)),
        pl.BlockSpec((None, nb, B_PREV, B_W), lambda i: (e, i, 0, 0)),
        _const_spec(bias_a.shape), _const_spec(sink.shape), _const_spec(bias_b.shape),
    ]
    return pl.pallas_call(
        functools.partial(_attn_sample_kernel, nb=nb),
        grid=(nbatch // nb,),
        in_specs=in_specs,
        out_specs=pl.BlockSpec((tm, D_MODEL), lambda i: (i, 0)),
        out_shape=jax.ShapeDtypeStruct((t, D_MODEL), BF16),
        compiler_params=_params(1),
        name="attn_sample",
    )(proj, proj, proj, proj, proj, ca_kt, ca_v, cb_kt, cb_v, bias_a, sink, bias_b)


M_PAIRS = M_HEADS // 2
M_QK_PAIR = 2 * M_DK
M_V_PAIR = 2 * M_DV


def _split3(x):
    hi = x.astype(BF16)
    r1 = x - hi.astype(F32)
    mid = r1.astype(BF16)
    return hi, mid, (r1 - mid.astype(F32)).astype(BF16)


def _widen_heads(x):
    first = lax.broadcasted_iota(jnp.int32, (x.shape[0], M_QK_PAIR), 1) < M_DK
    outs = []
    for p in range(M_PAIRS):
        src = x[:, p * M_QK_PAIR:(p + 1) * M_QK_PAIR]
        swapped = pltpu.roll(src, M_DK, 1)
        outs += [jnp.where(first, src, swapped), jnp.where(first, swapped, src)]
    return jnp.concatenate(outs, axis=1)


def _mlstm_chunks(chunks, states, chained):
    L = CHUNK
    n_chunks = len(chunks)
    wide = (L, M_QKW)
    row = lax.broadcasted_iota(jnp.int32, wide, 0)
    pos = lax.broadcasted_iota(jnp.int32, wide, 1) % M_DK
    t_row = lax.broadcasted_iota(jnp.int32, (L, L), 0)
    t_col = lax.broadcasted_iota(jnp.int32, (L, L), 1)
    tril = jnp.where(t_col <= t_row, 1.0, 0.0).astype(BF16)
    lane_qk = lax.broadcasted_iota(jnp.int32, (L, M_QK_PAIR), 1)
    lane_v = lax.broadcasted_iota(jnp.int32, (L, M_V_PAIR), 1)
    zero_b = jnp.zeros((), BF16)
    bd_r = lax.broadcasted_iota(jnp.int32, (M_QK_PAIR, M_V_PAIR), 0) < M_DK
    bd_c = lax.broadcasted_iota(jnp.int32, (M_QK_PAIR, M_V_PAIR), 1) < M_DV
    on_diag = bd_r == bd_c
    ones_bd = jnp.where(on_diag, 1.0, 0.0).astype(BF16)
    qk_lanes = [slice(p * M_QK_PAIR, (p + 1) * M_QK_PAIR) for p in range(M_PAIRS)]
    v_lanes = [slice(p * M_V_PAIR, (p + 1) * M_V_PAIR) for p in range(M_PAIRS)]

    pre = []
    for q, k, v, o, gates in chunks:
        ig = gates[:, 0:M_QKW]
        fg = gates[:, M_QKW:2 * M_QKW]
        lf = jnp.minimum(fg, 0.0) - jnp.log(1.0 + jnp.exp(-jnp.abs(fg)))
        hi, mid, lo = _split3(lf)
        b_c = _dot(tril, hi) + _dot(tril, mid) + _dot(tril, lo)
        r_c = ig - b_c
        r_row = jnp.sum(jnp.where(row == pos, r_c, 0.0), axis=0, keepdims=True)
        cm = r_c
        for sh in (1, 2, 4, 8, 16, 32):
            cm = jnp.maximum(cm, jnp.where(row >= sh, pltpu.roll(cm, sh, 0), NEG_INF))
        s_parts, v_bd = [], []
        for p in range(M_PAIRS):
            kp, vp = k[:, qk_lanes[p]], v[:, v_lanes[p]]
            k_bd = jnp.concatenate([jnp.where(lane_qk < M_DK, kp, zero_b),
                                    jnp.where(lane_qk >= M_DK, kp, zero_b)], axis=0)
            v_bd.append(jnp.concatenate([jnp.where(lane_v < M_DV, vp, zero_b),
                                         jnp.where(lane_v >= M_DV, vp, zero_b)], axis=0))
            s_parts.append(_dot_nt(q[:, qk_lanes[p]], k_bd))
        pre.append((b_c, r_c, r_row, cm, jnp.concatenate(s_parts, axis=1), v_bd))

    m_prevs, m_state = [], states[0][2]
    for c, (b_c, _, _, cm, _, _) in enumerate(pre):
        m_prev = m_state if chained else states[c][2]
        m_prevs.append(m_prev)
        m_state = b_c[L - 1:L, :] + jnp.maximum(m_prev, cm[L - 1:L, :])
    m_finals = [m_state] if chained else [b_c[L - 1:L, :] + jnp.maximum(m_prevs[c], cm[L - 1:L, :])
                                          for c, (b_c, _, _, cm, _, _) in enumerate(pre)]

    mid_vals = []
    for (q, k, v, o, gates), (b_c, r_c, r_row, cm, s_all, v_bd), m_prev in zip(chunks, pre, m_prevs):
        mm = jnp.maximum(m_prev, cm)
        w = jnp.exp(jnp.where(pos <= row, r_row - mm, NEG_INF))
        qg = jnp.exp(m_prev - mm) * q.astype(F32)
        sc = s_all * w
        mm_last = mm[L - 1:L, :]
        decay = jnp.exp(m_prev - mm_last)
        wk = jnp.exp(r_c - mm_last) * k.astype(F32)
        floor = _widen_heads(jnp.exp(-b_c - mm))
        mid_vals.append((sc, qg, decay, wk, floor))

    n_prevs, n_state = [], states[0][1]
    n_finals = []
    for c, (_, _, decay, wk, _) in enumerate(mid_vals):
        n_prev = n_state if chained else states[c][1]
        n_prevs.append(n_prev)
        n_state = decay * n_prev + jnp.sum(wk, axis=0, keepdims=True)
        n_finals.append(n_state)
    if chained:
        n_finals = n_finals[-1:]

    dens, upds = [], []
    for (q, k, v, o, gates), (sc, qg, decay, wk, _), n_prev in zip(chunks, mid_vals, n_prevs):
        d_hi, d_mid, _ = _split3(sc + qg * n_prev)
        dens.append(jnp.concatenate([_dot(d_hi[:, l], ones_bd) + _dot(d_mid[:, l], ones_bd) for l in qk_lanes], axis=1))
        wk_b = wk.astype(BF16)
        upds.append([jnp.where(on_diag, _dot_tn(wk_b[:, qk_lanes[p]], v[:, v_lanes[p]]), 0.0) for p in range(M_PAIRS)])

    nums, c_state, c_finals = [], states[0][0], []
    for c, ((sc, qg, decay, _, _), (_, _, _, _, _, v_bd)) in enumerate(zip(mid_vals, pre)):
        c_prev = c_state if chained else states[c][0]
        sc_b, qg_b = sc.astype(BF16), qg.astype(BF16)
        decay_v = _widen_heads(jnp.broadcast_to(decay, (8, M_QKW)))[0:1]
        parts, c_state = [], []
        for p in range(M_PAIRS):
            lhs = jnp.concatenate([sc_b[:, qk_lanes[p]], qg_b[:, qk_lanes[p]]], axis=1)
            rhs = jnp.concatenate([v_bd[p], c_prev[p].astype(BF16)], axis=0)
            parts.append(_dot(lhs, rhs))
            c_state.append(decay_v[:, v_lanes[p]] * c_prev[p] + upds[c][p])
        nums.append(jnp.concatenate(parts, axis=1))
        c_finals.append(c_state)
    if chained:
        c_finals = c_finals[-1:]

    outs = []
    for (q, k, v, o, gates), num, den, (_, _, _, _, floor) in zip(chunks, nums, dens, mid_vals):
        hs = num / jnp.maximum(jnp.abs(den), floor)
        outs.append((hs * (1.0 / (1.0 + jnp.exp(-o.astype(F32))))).astype(BF16))
    return outs, list(zip(c_finals, n_finals, m_finals))


def _mlstm_kernel(q_ref, k_ref, v_ref, o_ref, gates_ref, bias_ref, c_in, n_in, m_in,
                  hg_ref, c_out, n_out, m_out, c_s, n_s, m_s, *, seqs, chunks):
    assert seqs == 1 or chunks == 1
    chained = seqs == 1
    j = pl.program_id(1)

    if chained:
        @pl.when(j == 0)
        def _():
            c_s[...] = c_in[0]
            n_s[...] = n_in[0]
            m_s[...] = m_in[0]
        states = [([c_s[p] for p in range(M_PAIRS)], n_s[...], m_s[...])]
    else:
        states = [([c_in[s, p] for p in range(M_PAIRS)], n_in[s], m_in[s]) for s in range(seqs)]

    data = []
    for c in range(seqs * chunks):
        rows = slice(c * CHUNK, (c + 1) * CHUNK)
        data.append((q_ref[rows, :], k_ref[rows, :], v_ref[rows, :], o_ref[rows, :], gates_ref[rows, :] + bias_ref[...]))
    outs, finals = _mlstm_chunks(data, states, chained)
    for c, out in enumerate(outs):
        hg_ref[c * CHUNK:(c + 1) * CHUNK, :] = out

    if chained:
        c_fin, n_fin, m_fin = finals[0]
        for p in range(M_PAIRS):
            c_s[p] = c_fin[p]
        n_s[...] = n_fin
        m_s[...] = m_fin

        @pl.when(j == pl.num_programs(1) - 1)
        def _():
            c_out[0] = c_s[...]
            n_out[0] = n_s[...]
            m_out[0] = m_s[...]
    else:
        for s, (c_fin, n_fin, m_fin) in enumerate(finals):
            for p in range(M_PAIRS):
                c_out[s, p] = c_fin[p]
            n_out[s] = n_fin
            m_out[s] = m_fin


def _mlstm_call(proj, gates, bias, c0, n0, m0, layer, seqs_per_step, chunks_per_step):
    t = proj.shape[0]
    nseq = c0.shape[1]
    rows = seqs_per_step * chunks_per_step * CHUNK
    nsteps = t // (nseq * chunks_per_step * CHUNK)
    assert nseq % seqs_per_step == 0 and (seqs_per_step == 1 or nsteps == 1)
    blk = lambda s, j: s * nsteps + j
    c_blk = (seqs_per_step, M_PAIRS, M_QK_PAIR, M_V_PAIR)
    r_blk = (seqs_per_step, 1, M_QKW)
    in_specs = [
        pl.BlockSpec((rows, M_QKW), lambda s, j: (blk(s, j), 0)),
        pl.BlockSpec((rows, M_QKW), lambda s, j: (blk(s, j), 1)),
        pl.BlockSpec((rows, M_VW), lambda s, j: (blk(s, j), 1)),
        pl.BlockSpec((rows, M_VW), lambda s, j: (blk(s, j), 2)),
        pl.BlockSpec((rows, 2 * M_QKW), lambda s, j: (blk(s, j), 0)),
        _const_spec((1, 2 * M_QKW)),
        pl.BlockSpec((None,) + c_blk, lambda s, j: (layer, s, 0, 0, 0)),
        pl.BlockSpec((None,) + r_blk, lambda s, j: (layer, s, 0, 0)),
        pl.BlockSpec((None,) + r_blk, lambda s, j: (layer, s, 0, 0)),
    ]
    out_specs = (
        pl.BlockSpec((rows, M_VW), lambda s, j: (blk(s, j), 0)),
        pl.BlockSpec(c_blk, lambda s, j: (s, 0, 0, 0)),
        pl.BlockSpec(r_blk, lambda s, j: (s, 0, 0)),
        pl.BlockSpec(r_blk, lambda s, j: (s, 0, 0)),
    )
    out_shape = (
        jax.ShapeDtypeStruct((t, M_VW), BF16),
        jax.ShapeDtypeStruct((nseq,) + c_blk[1:], F32),
        jax.ShapeDtypeStruct((nseq,) + r_blk[1:], F32),
        jax.ShapeDtypeStruct((nseq,) + r_blk[1:], F32),
    )
    return pl.pallas_call(
        functools.partial(_mlstm_kernel, seqs=seqs_per_step, chunks=chunks_per_step),
        grid=(nseq // seqs_per_step, nsteps),
        in_specs=in_specs,
        out_specs=out_specs,
        out_shape=out_shape,
        scratch_shapes=[pltpu.VMEM(c_blk[1:], F32), pltpu.VMEM(r_blk[1:], F32), pltpu.VMEM(r_blk[1:], F32)],
        compiler_params=_params(2),
        name="mlstm",
    )(proj, proj, proj, proj, gates, bias, c0, n0, m0)


def _mlstm_state_to_kernel(c, n, m):
    lead = c.shape[:-3]
    cp = c.reshape(lead + (M_PAIRS, 2, M_DK, M_DV))
    z = jnp.zeros_like(cp[..., 0, :, :])
    top = jnp.concatenate([cp[..., 0, :, :], z], axis=-1)
    bot = jnp.concatenate([z, cp[..., 1, :, :]], axis=-1)
    c_bd = jnp.concatenate([top, bot], axis=-2)
    n_row = n.reshape(lead + (1, M_QKW))
    m_row = jnp.repeat(m, M_DK, axis=-1).reshape(lead + (1, M_QKW))
    return c_bd, n_row, m_row


def _mlstm_state_from_kernel(c_bd, n_row, m_row):
    lead = c_bd.shape[:-3]
    top = c_bd[..., :M_DK, :M_DV]
    bot = c_bd[..., M_DK:, M_DV:]
    c = jnp.stack([top, bot], axis=-3).reshape(lead + (M_HEADS, M_DK, M_DV))
    n = n_row.reshape(lead + (M_HEADS, M_DK))
    m = m_row.reshape(lead + (M_HEADS, M_DK))[..., 0]
    return c, n, m


def _post_kernel(x_ref, a_ref, wout_ref, g_ref, wq_ref, mk_ref, mv_ref, wo_ref, o_ref, att_s, *, nb, tb):
    x1 = x_ref[...] + _dot(a_ref[...], wout_ref[...])
    q = _dot(_rms(x1, g_ref[...]).astype(BF16), wq_ref[...]).astype(BF16)
    scale = MEM_HD ** -0.5
    units = [(b, h) for b in range(nb) for h in range(MEM_HEADS)]
    where = lambda b, h: (slice(b * tb, (b + 1) * tb), slice(h * MEM_HD, (h + 1) * MEM_HD))
    scores = [_dot_nt(q[where(b, h)], mk_ref[b, h]) * scale for b, h in units]
    probs = []
    for s in scores:
        e = jnp.exp(s - jnp.max(s, axis=-1, keepdims=True))
        probs.append((e / jnp.sum(e, axis=-1, keepdims=True)).astype(BF16))
    for (b, h), p in zip(units, probs):
        att_s[where(b, h)] = _dot(p, mv_ref[b, h]).astype(BF16)
    o_ref[...] = x1 + _dot(att_s[...], wo_ref[...])


def _post_call(x, a, w_out, wl, g, w_q, mk, mv, w_o, layer, tm, tb):
    t = x.shape[0]
    nb = tm // tb
    shared = mk.shape[1] == 1
    tail = (0,) * (mk.ndim - 2)
    mem_map = (lambda i: (layer, 0) + tail) if shared else (lambda i: (layer, i) + tail)
    mem_blk = (None, nb) + mk.shape[2:]
    row = lambda i: (i, 0)
    in_specs = [
        pl.BlockSpec((tm, D_MODEL), row),
        pl.BlockSpec((tm, a.shape[1]), row),
        _layer_spec(w_out, wl),
        _layer_spec(g, layer),
        _layer_spec(w_q, layer),
        pl.BlockSpec(mem_blk, mem_map),
        pl.BlockSpec(mem_blk, mem_map),
        _layer_spec(w_o, layer),
    ]
    return pl.pallas_call(
        functools.partial(_post_kernel, nb=nb, tb=tb),
        grid=(t // tm,),
        in_specs=in_specs,
        out_specs=pl.BlockSpec((tm, D_MODEL), row),
        out_shape=jax.ShapeDtypeStruct((t, D_MODEL), F32),
        scratch_shapes=[pltpu.VMEM((tm, D_MODEL), BF16)],
        compiler_params=_params(1),
        name="post_cross",
    )(x, a, w_out, g, w_q, mk, mv, w_o)


def _conv3(u, cw, hists, seq_rows):
    w0, w1, w2 = cw[0:1], cw[1:2], cw[2:3]
    c = pltpu.roll(u, 2, 0) * w0 + pltpu.roll(u, 1, 0) * w1 + u * w2
    row = lax.broadcasted_iota(jnp.int32, (8, u.shape[1]), 0)
    pieces = []
    for s, hist in enumerate(hists):
        base = s * seq_rows
        t8 = u[base:base + 8]
        h0, h1 = hist[0:1], hist[1:2]
        u1 = jnp.where(row == 0, h1, pltpu.roll(t8, 1, 0))
        u2 = jnp.where(row == 0, h0, jnp.where(row == 1, h1, pltpu.roll(t8, 2, 0)))
        pieces += [u2 * w0 + u1 * w1 + t8 * w2, c[base + 8:base + seq_rows]]
    return jnp.concatenate(pieces, axis=0)


def _ffn_kernel(*refs, seq_rows, carry, final_norm):
    x_ref, g_ref, win_ref, cw_ref, wout_ref = refs[:5]
    pos = 5
    hist_ref = None
    if not carry:
        hist_ref = refs[pos]
        pos += 1
    gfin_ref = None
    if final_norm:
        gfin_ref = refs[pos]
        pos += 1
    o_ref, cs_ref, act_s = refs[pos:pos + 3]
    hist_s = refs[pos + 3] if carry else None

    tm = x_ref.shape[0]
    nseq = tm // seq_rows
    x = x_ref[...]
    h = _rms(x, g_ref[...]).astype(BF16)

    if carry:
        @pl.when(pl.program_id(0) == 0)
        def _():
            hist_s[...] = jnp.zeros_like(hist_s)

    for j in range(D_FF // FF_CHUNK):
        halves = []
        for half in range(2):
            c0 = half * D_FF + j * FF_CHUNK
            cols = slice(c0, c0 + FF_CHUNK)
            u = _dot(h, win_ref[:, cols])
            if carry:
                hists = [hist_s[:, cols]]
            else:
                hists = [hist_ref[s, :, cols] for s in range(nseq)]
            halves.append(_conv3(u, cw_ref[:, cols], hists, seq_rows))
            for s in range(nseq):
                last2 = u[(s + 1) * seq_rows - 2:(s + 1) * seq_rows]
                if carry:
                    hist_s[:, cols] = last2
                    cs_ref[:, cols] = last2
                else:
                    cs_ref[s, :, cols] = last2
        ca, cg = halves
        act = ca * (1.0 / (1.0 + jnp.exp(-ca))) * cg
        act_s[:, j * FF_CHUNK:(j + 1) * FF_CHUNK] = act.astype(BF16)

    y = x + _dot(act_s[...], wout_ref[...])
    if final_norm:
        y = _rms(y, gfin_ref[...])
    o_ref[...] = y


def _ffn_call(x, g, w_in, conv_w, w_out, layer, hist, g_final, tm):
    t = x.shape[0]
    carry = hist is None
    seq_rows = tm if carry else CHUNK
    nseq = tm // seq_rows
    row = lambda i: (i, 0)
    args = [x, g, w_in, conv_w, w_out]
    in_specs = [pl.BlockSpec((tm, D_MODEL), row), _layer_spec(g, layer),
                _layer_spec(w_in, layer), _layer_spec(conv_w, layer), _layer_spec(w_out, layer)]
    if carry:
        cs_shape, cs_spec = (2, 2 * D_FF), _const_spec((2, 2 * D_FF))
    else:
        args.append(hist)
        in_specs.append(pl.BlockSpec((None, nseq, 2, 2 * D_FF), lambda i: (layer, i, 0, 0)))
        cs_shape, cs_spec = hist.shape[1:], pl.BlockSpec((nseq, 2, 2 * D_FF), lambda i: (i, 0, 0))
    if g_final is not None:
        args.append(g_final.reshape(1, D_MODEL))
        in_specs.append(_const_spec((1, D_MODEL)))
    scratch = [pltpu.VMEM((tm, D_FF), BF16)]
    if carry:
        scratch.append(pltpu.VMEM((2, 2 * D_FF), F32))
    return pl.pallas_call(
        functools.partial(_ffn_kernel, seq_rows=seq_rows, carry=carry, final_norm=g_final is not None),
        grid=(t // tm,),
        in_specs=in_specs,
        out_specs=(pl.BlockSpec((tm, D_MODEL), row), cs_spec),
        out_shape=(jax.ShapeDtypeStruct((t, D_MODEL), F32), jax.ShapeDtypeStruct(cs_shape, F32)),
        scratch_shapes=scratch,
        compiler_params=_params(1),
        name="conv_ffn",
    )(*args)


def _alibi(n_q, n_k):
    q = np.arange(n_q)[:, None]
    dist = np.abs(q + A_PREV - np.arange(n_k)[None, :]).astype(np.float32)
    slopes = (2.0 ** (-8.0 * np.arange(1, A_HEADS + 1) / A_HEADS)).astype(np.float32)
    return -slopes[:, None, None] * dist[None]


def _band_mask(n_q, n_k, n_prev_chunks, first_valid_chunk):
    qc = (np.arange(n_q) // CHUNK)[:, None]
    kc = (np.arange(n_k) // CHUNK)[None, :]
    ok = (kc >= qc) & (kc <= qc + n_prev_chunks) & (kc >= first_valid_chunk)
    return np.where(ok, 0.0, NEG_INF).astype(np.float32)


def _alibi_sample_table():
    return jnp.asarray(_alibi(CHUNK, A_BAND).reshape(A_KV_HEADS, A_GROUP * CHUNK, A_BAND))


def _alibi_prompt_tables():
    n_prev = A_PREV // CHUNK
    base = _alibi(P_SUB, PA_KEYS)
    general = (base + _band_mask(P_SUB, PA_KEYS, n_prev, 0)).reshape(A_KV_HEADS, A_GROUP * P_SUB, PA_KEYS)
    first = (base + _band_mask(P_SUB, PA_KEYS, n_prev, n_prev)).reshape(A_KV_HEADS, A_GROUP * P_SUB, PA_KEYS)
    return jnp.asarray(np.stack([np.stack([first, general]), np.stack([general, general])]))


def _relpos_toeplitz(table, n_q, n_k):
    h, n_rel = table.shape
    n_lo = n_k - B_BAND
    n_hi = n_q + n_k - 1 - n_lo - n_rel
    ext = jnp.concatenate([jnp.broadcast_to(table[:, :1], (h, n_lo + 1)), table,
                           jnp.broadcast_to(table[:, -1:], (h, n_hi))], axis=1)
    rev = ext[:, ::-1][:, None, :]
    length = n_q + n_k - 1
    tiled = jnp.broadcast_to(rev, (h, n_q, length + 1)).reshape(h, n_q * (length + 1))
    skew = tiled[:, :n_q * length].reshape(h, n_q, length)
    return skew[:, :, n_q - 1:n_q - 1 + n_k]


def _relpos_prompt_tables(table):
    n_prev = B_PREV // CHUNK
    masks = np.stack([_band_mask(P_TILE, PB_KEYS, n_prev, max(n_prev - t * (P_TILE // CHUNK), 0)) for t in range(3)])
    return _relpos_toeplitz(table, P_TILE, PB_KEYS)[None] + jnp.asarray(masks)[:, None]


def _row_tile(t):
    return PROMPT_TILE if t % PROMPT_TILE == 0 else 256


def kernel(x_prompt, x_sample, mem_prompt, cache_a_k, cache_a_v, cache_b_k, cache_b_v, state_mlstm_c, state_mlstm_n, state_mlstm_m, cache_mem_k, cache_mem_v, state_ffn_conv, norm_mix, norm_cross, norm_ffn, norm_final, w_in_att, w_out_att, sink_a, relpos_b, w_in_mlstm, b_igate, b_fgate, w_out_mlstm, w_mem_q, w_mem_k, w_mem_v, w_mem_o, w_ffn_in, conv_ffn, w_ffn_out):
    depth = norm_mix.shape[0]
    assert x_prompt.shape[0] == 1
    tp = x_prompt.shape[1]
    nb, ds = x_sample.shape[0], x_sample.shape[1]
    assert ds == CHUNK and tp % PROMPT_TILE == 0
    ts = nb * ds
    tm_p, tm_s = PROMPT_TILE, _row_tile(ts)
    tm_big = WIDE_TILE if tp % WIDE_TILE == 0 else tm_p
    xp = x_prompt[0]
    xs = x_sample.reshape(ts, D_MODEL)

    memkv = _memkv_call(mem_prompt[0], jnp.concatenate([w_mem_k, w_mem_v], axis=0).astype(BF16))
    head_major = lambda m: jnp.transpose(m, (0, 1, 3, 2, 4)).astype(BF16)
    split_heads = lambda m: m.reshape(depth, 1, MEM_TOKENS, MEM_HEADS, MEM_HD)
    p_mk, p_mv = head_major(split_heads(memkv[:depth])), head_major(split_heads(memkv[depth:]))
    s_mk, s_mv = head_major(cache_mem_k), head_major(cache_mem_v)

    g_mix = norm_mix.astype(F32)[:, None]
    g_cross = norm_cross.astype(F32)[:, None]
    g_ffn = norm_ffn.astype(F32)[:, None]
    wa = w_in_att
    scale = HEAD_DIM ** -0.5
    w_att = jnp.concatenate([wa[:, :, 0:512] * scale, wa[:, :, 768:1280] * scale, wa[:, :, 1280:2304],
                             wa[:, :, 512:768]], axis=2).astype(BF16)
    wm = w_in_mlstm
    k_scale = M_DK ** -0.5
    w_mls = jnp.concatenate([wm[:, :, 0:M_QKW], wm[:, :, M_QKW:2 * M_QKW] * k_scale, wm[:, :, 2 * M_QKW:M_MAIN],
                             wm[:, :, M_MAIN:], jnp.zeros(wm.shape[:2] + (128 - 2 * M_HEADS,), wm.dtype)],
                            axis=2).astype(BF16)
    w_out_a, w_out_m = w_out_att.astype(BF16), w_out_mlstm.astype(BF16)
    w_q, w_o = w_mem_q.astype(BF16), w_mem_o.astype(BF16)
    w_fi, w_fo = w_ffn_in.astype(BF16), w_ffn_out.astype(BF16)
    conv_w = conv_ffn.astype(F32)
    hist_s = state_ffn_conv.astype(F32)
    n_even = cache_a_k.shape[0]
    ca_kt = jnp.transpose(cache_a_k, (0, 1, 3, 4, 2)).astype(BF16)
    cb_kt = jnp.transpose(cache_b_k, (0, 1, 3, 4, 2)).astype(BF16)
    ca_v = cache_a_v.reshape(n_even, nb, A_PREV, A_KVW).astype(BF16)
    cb_v = cache_b_v.reshape(n_even, nb, B_PREV, B_W).astype(BF16)
    st_c, st_n, st_m = _mlstm_state_to_kernel(state_mlstm_c.astype(F32), state_mlstm_n.astype(F32),
                                              state_mlstm_m.astype(F32))
    zc = jnp.zeros((1, 1, M_PAIRS, M_QK_PAIR, M_V_PAIR), F32)
    zn = jnp.zeros((1, 1, 1, M_QKW), F32)
    zm = jnp.zeros((1, 1, 1, M_QKW), F32)

    bias_a_s = _alibi_sample_table()
    bias_a_p = _alibi_prompt_tables()
    post_tm_s = tm_s
    attn_nb = 8 if nb % 8 == 0 else 4

    p_ak, p_av, p_bk, p_bv, p_c, p_n, p_m, p_conv = [], [], [], [], [], [], [], []
    s_ak, s_av, s_bk, s_bv, s_c, s_n, s_m, s_conv = [], [], [], [], [], [], [], []
    for l in range(depth):
        if l % 2 == 0:
            e = l // 2
            proj_p, kv_p, kbt_p, kat_p = _att_proj(xp, g_mix, l, w_att, e, tm_big, prompt=True)
            proj_s, kv_s = _att_proj(xs, g_mix, l, w_att, e, tm_s, prompt=False)
            sk = sink_a[e].astype(F32)
            sink_p = jnp.repeat(sk, P_SUB).reshape(A_KV_HEADS, A_GROUP * P_SUB, 1)
            sink_s = jnp.repeat(sk, CHUNK).reshape(A_KV_HEADS, A_GROUP * CHUNK, 1)
            rel = relpos_b[e].astype(F32)
            mix_p = _attn_prompt_call(proj_p, kbt_p, kat_p, bias_a_p, sink_p, _relpos_prompt_tables(rel))
            mix_s = _attn_sample_call(proj_s, ca_kt, ca_v, cb_kt, cb_v, e, bias_a_s, sink_s,
                                      _relpos_toeplitz(rel, CHUNK, B_BAND), attn_nb)
            w_mix_out, wl = w_out_a, e
            p_ak.append(kv_p[-A_PREV:, 0:128].reshape(1, A_PREV, A_KV_HEADS, HEAD_DIM))
            p_av.append(kv_p[-A_PREV:, 128:256].reshape(1, A_PREV, A_KV_HEADS, HEAD_DIM))
            p_bk.append(kv_p[-B_PREV:, 256:768].reshape(1, B_PREV, B_HEADS, HEAD_DIM))
            p_bv.append(kv_p[-B_PREV:, 768:1280].reshape(1, B_PREV, B_HEADS, HEAD_DIM))
            s_ak.append(kv_s[:, 0:128].reshape(nb, ds, A_KV_HEADS, HEAD_DIM))
            s_av.append(kv_s[:, 128:256].reshape(nb, ds, A_KV_HEADS, HEAD_DIM))
            s_bk.append(kv_s[:, 256:768].reshape(nb, ds, B_HEADS, HEAD_DIM))
            s_bv.append(kv_s[:, 768:1280].reshape(nb, ds, B_HEADS, HEAD_DIM))
        else:
            o = l // 2
            gate_bias = jnp.concatenate([jnp.repeat(b_igate[o].astype(F32), M_DK),
                                         jnp.repeat(b_fgate[o].astype(F32), M_DK)]).reshape(1, M_GATES_W)
            proj_p, gates_p = _mlstm_proj(xp, g_mix, l, w_mls, o, tm_big)
            proj_s, gates_s = _mlstm_proj(xs, g_mix, l, w_mls, o, tm_s)
            mix_p, c, n, m = _mlstm_call(proj_p, gates_p, gate_bias, zc, zn, zm, 0, 1, 4)
            c, n, m = _mlstm_state_from_kernel(c, n, m)
            p_c.append(c); p_n.append(n); p_m.append(m)
            mix_s, c, n, m = _mlstm_call(proj_s, gates_s, gate_bias, st_c, st_n, st_m, o, 4, 1)
            c, n, m = _mlstm_state_from_kernel(c, n, m)
            s_c.append(c); s_n.append(n); s_m.append(m)
            w_mix_out, wl = w_out_m, o

        xp = _post_call(xp, mix_p, w_mix_out, wl, g_cross, w_q, p_mk, p_mv, w_o, l, tm_big, tm_big)
        xs = _post_call(xs, mix_s, w_mix_out, wl, g_cross, w_q, s_mk, s_mv, w_o, l, post_tm_s, CHUNK)

        g_fin = norm_final if l == depth - 1 else None
        xp, cv = _ffn_call(xp, g_ffn, w_fi, conv_w, w_fo, l, None, g_fin, tm_big)
        p_conv.append(cv[None])
        xs, cv = _ffn_call(xs, g_ffn, w_fi, conv_w, w_fo, l, hist_s, g_fin, tm_s)
        s_conv.append(cv)

    p_mk, p_mv = memkv[:depth], memkv[depth:]
    mem_shape = (depth, 1, MEM_TOKENS, MEM_HEADS, MEM_HD)
    return (xp[None], xs.reshape(nb, ds, D_MODEL),
            jnp.stack(p_ak), jnp.stack(p_av), jnp.stack(p_bk), jnp.stack(p_bv),
            jnp.stack(p_c), jnp.stack(p_n), jnp.stack(p_m),
            p_mk.reshape(mem_shape), p_mv.reshape(mem_shape), jnp.stack(p_conv),
            jnp.stack(s_ak), jnp.stack(s_av), jnp.stack(s_bk), jnp.stack(s_bv),
            jnp.stack(s_c), jnp.stack(s_n), jnp.stack(s_m), jnp.stack(s_conv))
```

```python
import functools

import numpy as np
import jax
import jax.numpy as jnp
from jax import lax
from jax.experimental import pallas as pl
from jax.experimental.pallas import tpu as pltpu

F32 = jnp.float32
BF16 = jnp.bfloat16

D_MODEL = 1024
CHUNK = 64
HEAD_DIM = 64
A_HEADS = 8
A_KV_HEADS = 2
A_GROUP = A_HEADS // A_KV_HEADS
A_BAND = 3 * CHUNK
A_PREV = 2 * CHUNK
B_HEADS = 8
B_BAND = 9 * CHUNK
B_PREV = 8 * CHUNK
REL_CLIP = 128
A_QW = A_HEADS * HEAD_DIM
A_KVW = A_KV_HEADS * HEAD_DIM
B_W = B_HEADS * HEAD_DIM
ATT_PROJ = A_QW + 2 * A_KVW + 3 * B_W
KV32_W = 2 * A_KVW + 2 * B_W
M_HEADS = 8
M_DK = 64
M_DV = 128
M_QKW = M_HEADS * M_DK
M_VW = M_HEADS * M_DV
M_MAIN = 2 * M_QKW + 2 * M_VW
MEM_TOKENS = 256
MEM_HEADS = 4
MEM_HD = D_MODEL // MEM_HEADS
D_FF = 2816
FF_CHUNK = 256
EPS = 1e-6
NEG_INF = float("-inf")

VMEM_LIMIT_BYTES = 56 * 1024 * 1024
PROMPT_TILE = 512
WIDE_TILE = 1024


def _params(n_axes):
    return pltpu.CompilerParams(dimension_semantics=("arbitrary",) * n_axes,
                                vmem_limit_bytes=VMEM_LIMIT_BYTES)


def _const_spec(shape):
    zeros = (0,) * len(shape)
    return pl.BlockSpec(shape, lambda *_: zeros)


def _layer_spec(stack, layer):
    idx = (layer,) + (0,) * (stack.ndim - 1)
    return pl.BlockSpec((None,) + stack.shape[1:], lambda *_: idx, pipeline_mode=pl.Buffered(1))


def _rms(x, g):
    return x * lax.rsqrt(jnp.mean(x * x, axis=-1, keepdims=True) + EPS) * g


def _dot(a, b):
    return jnp.dot(a, b, preferred_element_type=F32)


def _dot_nt(a, b):
    return lax.dot_general(a, b, (((1,), (1,)), ((), ())), preferred_element_type=F32)


def _dot_tn(a, b):
    return lax.dot_general(a, b, (((0,), (0,)), ((), ())), preferred_element_type=F32)


def _proj_kernel(x_ref, g_ref, w_ref, *out_refs, plan, transposed, replicated):
    h = _rms(x_ref[...], g_ref[...]).astype(BF16)
    for c0, cw, dsts in plan:
        r = _dot(h, w_ref[:, c0:c0 + cw])
        for oi, oc in dsts:
            out_refs[oi][:, oc:oc + cw] = r.astype(out_refs[oi].dtype)
        if c0 in transposed:
            oi, width = transposed[c0]
            out_refs[oi][...] = r[:, :width].T.astype(out_refs[oi].dtype)
        if c0 in replicated:
            oi, n_cols = replicated[c0]
            first = lax.broadcasted_iota(jnp.int32, (r.shape[0], 128), 1) < 64
            for j in range(0, n_cols, 2):
                out_refs[oi][:, j * 64:(j + 2) * 64] = jnp.where(first, r[:, j:j + 1], r[:, j + 1:j + 2])


def _proj_call(x, g, gl, w, wl, plan, transposed, replicated, out_shapes, out_specs, tm, name):
    t = x.shape[0]
    return pl.pallas_call(
        functools.partial(_proj_kernel, plan=plan, transposed=transposed, replicated=replicated),
        grid=(t // tm,),
        in_specs=[pl.BlockSpec((tm, D_MODEL), lambda i: (i, 0)),
                  _layer_spec(g, gl),
                  _layer_spec(w, wl)],
        out_specs=out_specs,
        out_shape=out_shapes,
        compiler_params=_params(1),
        name=name,
    )(x, g, w)


_ATT_PLAN = (
    (2048, 256, ((0, 2048), (1, 0))),
    (1024, 512, ((0, 1024), (1, 256))),
    (1536, 512, ((0, 1536), (1, 768))),
    (0, 512, ((0, 0),)),
    (512, 512, ((0, 512),)),
)


def _att_proj(x, g, gl, w_perm, wl, tm, prompt):
    t = x.shape[0]
    shapes = [jax.ShapeDtypeStruct((t, ATT_PROJ), BF16)]
    specs = [pl.BlockSpec((tm, ATT_PROJ), lambda i: (i, 0))]
    transposed = {}
    if prompt:
        shapes += [jax.ShapeDtypeStruct((tm, KV32_W), F32), jax.ShapeDtypeStruct((B_W, t), BF16),
                   jax.ShapeDtypeStruct((A_KVW, t), BF16)]
        specs += [_const_spec((tm, KV32_W)), pl.BlockSpec((B_W, tm), lambda i: (0, i)),
                  pl.BlockSpec((A_KVW, tm), lambda i: (0, i))]
        transposed = {1024: (2, B_W), 2048: (3, A_KVW)}
    else:
        shapes.append(jax.ShapeDtypeStruct((t, KV32_W), F32))
        specs.append(pl.BlockSpec((tm, KV32_W), lambda i: (i, 0)))
    return _proj_call(x, g, gl, w_perm, wl, _ATT_PLAN, transposed, {}, tuple(shapes), tuple(specs), tm, "att_proj")


M_GATES_W = 2 * M_QKW
_MLSTM_PLAN = ((M_MAIN, 128, ()),) + tuple((c, 512, ((0, c),)) for c in range(0, M_MAIN, 512))


def _mlstm_proj(x, g, gl, w_pad, wl, tm):
    t = x.shape[0]
    return _proj_call(
        x, g, gl, w_pad, wl, _MLSTM_PLAN, {}, {M_MAIN: (1, 2 * M_HEADS)},
        (jax.ShapeDtypeStruct((t, M_MAIN), BF16), jax.ShapeDtypeStruct((t, M_GATES_W), F32)),
        (pl.BlockSpec((tm, M_MAIN), lambda i: (i, 0)), pl.BlockSpec((tm, M_GATES_W), lambda i: (i, 0))),
        tm, "mlstm_proj")


def _memkv_kernel(m_ref, w_ref, o_ref):
    o_ref[0] = _dot(m_ref[...].astype(BF16), w_ref[0])


def _memkv_call(mem, w):
    n = w.shape[0]
    return pl.pallas_call(
        _memkv_kernel,
        grid=(n,),
        in_specs=[_const_spec(mem.shape), pl.BlockSpec((1, D_MODEL, D_MODEL), lambda i: (i, 0, 0))],
        out_specs=pl.BlockSpec((1, MEM_TOKENS, D_MODEL), lambda i: (i, 0, 0)),
        out_shape=jax.ShapeDtypeStruct((n, MEM_TOKENS, D_MODEL), F32),
        compiler_params=_params(1),
        name="mem_kv",
    )(mem, w)


def _low_half(n):
    return lax.broadcasted_iota(jnp.int32, (n, 128), 1) < HEAD_DIM


def _with_ones(v):
    return jnp.concatenate([v, jnp.ones((v.shape[0], 128), BF16)], axis=1)


def _one_head(v_pair, first):
    zero = jnp.zeros((), BF16)
    low = _low_half(v_pair.shape[0])
    return _with_ones(jnp.where(low, v_pair, zero) if first else jnp.where(low, zero, v_pair))


def _head_on_both_halves(v_pair, first):
    swapped = jnp.concatenate([v_pair[:, HEAD_DIM:], v_pair[:, :HEAD_DIM]], axis=1)
    low = _low_half(v_pair.shape[0])
    return _with_ones(jnp.where(low, v_pair, swapped) if first else jnp.where(low, swapped, v_pair))


def _attend(units):
    scores = [scores_fn() for scores_fn, _, _ in units]
    maxes = []
    for sc, (_, _, sink) in zip(scores, units):
        if all(s.shape == sc[0].shape for s in sc):
            mx = jnp.max(functools.reduce(jnp.maximum, sc), axis=-1, keepdims=True)
        else:
            mx = functools.reduce(jnp.maximum, [jnp.max(s, axis=-1, keepdims=True) for s in sc])
        maxes.append(mx if sink is None else jnp.maximum(mx, sink))
    exps = [[jnp.exp(s - mx).astype(BF16) for s in sc] for sc, mx in zip(scores, maxes)]
    outs = []
    for es, (_, values_fn, _) in zip(exps, units):
        outs.append(functools.reduce(lambda a, b: a + b, [_dot(e, v) for e, v in zip(es, values_fn())]))
    results = []
    for out, mx, (_, _, sink) in zip(outs, maxes, units):
        den = out[:, 128:256]
        if sink is not None:
            den = den + jnp.exp(sink - mx)
        results.append(out[:, 0:128] / den)
    return results


P_TILE = 4 * CHUNK
P_SUB = 2 * CHUNK
PB_KEYS = B_PREV + P_TILE
PA_KEYS = A_PREV + P_SUB


def _attn_prompt_kernel(qa_ref, qb_ref, kt2_ref, kt1_ref, kt0_ref, vb2_ref, vb1_ref, vb0_ref, kat_ref, katp_ref,
                        kva_ref, kvap_ref, bias_a_ref, sink_ref, bias_b_ref, o_ref):

    def unit_a(sb, g):
        q_rows = slice(sb * P_SUB, (sb + 1) * P_SUB)
        hd = slice(g * HEAD_DIM, (g + 1) * HEAD_DIM)
        if sb == 0:
            blocks = [(lambda: katp_ref[hd, :], None, 0, A_PREV),
                      (lambda: kat_ref[hd, 0:P_SUB], slice(0, P_SUB), A_PREV, P_SUB)]
        else:
            k_rows = slice(sb * P_SUB - A_PREV, (sb + 1) * P_SUB)
            blocks = [(lambda: kat_ref[hd, k_rows], k_rows, 0, PA_KEYS)]

        def scores():
            qs = jnp.concatenate([qa_ref[q_rows, h * HEAD_DIM:(h + 1) * HEAD_DIM]
                                  for h in range(g * A_GROUP, (g + 1) * A_GROUP)], axis=0)
            return [_dot(qs, kt()) + bias_a_ref[sb, g, :, c0:c0 + n] for kt, _, c0, n in blocks]

        def values():
            rows = [kvap_ref[:, A_KVW:2 * A_KVW] if r is None else kva_ref[r, A_KVW:2 * A_KVW] for _, r, _, _ in blocks]
            return [_head_on_both_halves(vv, g == 0) for vv in rows]

        return scores, values, sink_ref[g]

    kv_blocks = ((kt2_ref, vb2_ref), (kt1_ref, vb1_ref), (kt0_ref, vb0_ref))

    def unit_b(h):
        cols = slice(h * HEAD_DIM, (h + 1) * HEAD_DIM)
        pair_cols = slice((h - h % 2) * HEAD_DIM, (h - h % 2 + 2) * HEAD_DIM)

        def scores():
            q = qb_ref[:, cols]
            return [_dot(q, k_ref[cols, :]) + bias_b_ref[h, :, j * P_TILE:(j + 1) * P_TILE]
                    for j, (k_ref, _) in enumerate(kv_blocks)]

        def values():
            return [_one_head(v_ref[:, pair_cols], h % 2 == 0) for _, v_ref in kv_blocks]

        return scores, values, None

    a_ids = [(sb, g) for sb in range(P_TILE // P_SUB) for g in range(A_KV_HEADS)]
    res = _attend([unit_a(sb, g) for sb, g in a_ids] + [unit_b(h) for h in range(B_HEADS)])
    for (sb, g), o in zip(a_ids, res):
        q_rows = slice(sb * P_SUB, (sb + 1) * P_SUB)
        for j in range(0, A_GROUP, 2):
            pair = jnp.where(_low_half(P_SUB), o[j * P_SUB:(j + 1) * P_SUB], o[(j + 1) * P_SUB:(j + 2) * P_SUB])
            c0 = (g * A_GROUP + j) * HEAD_DIM
            o_ref[q_rows, c0:c0 + 2 * HEAD_DIM] = pair.astype(BF16)
    res_b = res[len(a_ids):]
    for h in range(0, B_HEADS, 2):
        o_ref[:, A_QW + h * HEAD_DIM:A_QW + (h + 2) * HEAD_DIM] = (res_b[h] + res_b[h + 1]).astype(BF16)


def _attn_prompt_call(proj, kbt, kat, bias_a, sink, bias_b, e):
    t = proj.shape[0]
    back2 = lambda i: jnp.maximum(i - 2, 0)
    back1 = lambda i: jnp.maximum(i - 1, 0)
    prev_a = lambda i: jnp.maximum(i * (P_TILE // A_PREV) - 1, 0)
    in_specs = [
        pl.BlockSpec((P_TILE, 512), lambda i: (i, 0)),
        pl.BlockSpec((P_TILE, 512), lambda i: (i, 1)),
        pl.BlockSpec((B_W, P_TILE), lambda i: (0, back2(i))),
        pl.BlockSpec((B_W, P_TILE), lambda i: (0, back1(i))),
        pl.BlockSpec((B_W, P_TILE), lambda i: (0, i)),
        pl.BlockSpec((P_TILE, 512), lambda i: (back2(i), 3)),
        pl.BlockSpec((P_TILE, 512), lambda i: (back1(i), 3)),
        pl.BlockSpec((P_TILE, 512), lambda i: (i, 3)),
        pl.BlockSpec((A_KVW, P_TILE), lambda i: (0, i)),
        pl.BlockSpec((A_KVW, A_PREV), lambda i: (0, prev_a(i))),
        pl.BlockSpec((P_TILE, 256), lambda i: (i, 8)),
        pl.BlockSpec((A_PREV, 256), lambda i: (prev_a(i), 8)),
        pl.BlockSpec((None,) + bias_a.shape[1:], lambda i: (jnp.minimum(i, 1), 0, 0, 0, 0)),
        _layer_spec(sink, e),
        pl.BlockSpec((None, None) + bias_b.shape[2:], lambda i: (e, jnp.minimum(i, 2), 0, 0, 0)),
    ]
    return pl.pallas_call(
        _attn_prompt_kernel,
        grid=(t // P_TILE,),
        in_specs=in_specs,
        out_specs=pl.BlockSpec((P_TILE, D_MODEL), lambda i: (i, 0)),
        out_shape=jax.ShapeDtypeStruct((t, D_MODEL), BF16),
        compiler_params=_params(1),
        name="attn_prompt",
    )(proj, proj, kbt, kbt, kbt, proj, proj, proj, kat, kat, proj, proj, bias_a, sink, bias_b)


def _attn_sample_kernel(qa_ref, qb_ref, kb_ref, vb_ref, kva_ref, cakt_ref, cav_ref, cbkt_ref, cbv_ref,
                        bias_a_ref, sink_ref, bias_b_ref, o_ref, *, nb):
    def unit_a(b, g):
        rows = slice(b * CHUNK, (b + 1) * CHUNK)
        hd = slice(g * HEAD_DIM, (g + 1) * HEAD_DIM)

        def scores():
            qs = jnp.concatenate([qa_ref[rows, h * HEAD_DIM:(h + 1) * HEAD_DIM]
                                  for h in range(g * A_GROUP, (g + 1) * A_GROUP)], axis=0)
            return [_dot(qs, cakt_ref[b, g]) + bias_a_ref[g, :, 0:A_PREV],
                    _dot_nt(qs, kva_ref[rows, hd]) + bias_a_ref[g, :, A_PREV:A_BAND]]

        def values():
            return [_head_on_both_halves(cav_ref[b], g == 0),
                    _head_on_both_halves(kva_ref[rows, A_KVW:2 * A_KVW], g == 0)]

        return scores, values, sink_ref[g]

    def unit_b(b, h):
        rows = slice(b * CHUNK, (b + 1) * CHUNK)
        cols = slice(h * HEAD_DIM, (h + 1) * HEAD_DIM)
        pair_cols = slice((h - h % 2) * HEAD_DIM, (h - h % 2 + 2) * HEAD_DIM)

        def scores():
            q = qb_ref[rows, cols]
            return [_dot(q, cbkt_ref[b, h]) + bias_b_ref[h, :, 0:B_PREV],
                    _dot_nt(q, kb_ref[rows, cols]) + bias_b_ref[h, :, B_PREV:B_BAND]]

        def values():
            return [_one_head(cbv_ref[b, :, pair_cols], h % 2 == 0), _one_head(vb_ref[rows, pair_cols], h % 2 == 0)]

        return scores, values, None

    a_ids = [(b, g) for b in range(nb) for g in range(A_KV_HEADS)]
    b_ids = [(b, h) for b in range(nb) for h in range(B_HEADS)]
    res = _attend([unit_a(b, g) for b, g in a_ids] + [unit_b(b, h) for b, h in b_ids])
    for (b, g), o in zip(a_ids, res):
        rows = slice(b * CHUNK, (b + 1) * CHUNK)
        for j in range(0, A_GROUP, 2):
            pair = jnp.where(_low_half(CHUNK), o[j * CHUNK:(j + 1) * CHUNK], o[(j + 1) * CHUNK:(j + 2) * CHUNK])
            c0 = (g * A_GROUP + j) * HEAD_DIM
            o_ref[rows, c0:c0 + 2 * HEAD_DIM] = pair.astype(BF16)
    res_b = res[len(a_ids):]
    for i in range(0, len(b_ids), 2):
        b, h = b_ids[i]
        rows = slice(b * CHUNK, (b + 1) * CHUNK)
        o_ref[rows, A_QW + h * HEAD_DIM:A_QW + (h + 2) * HEAD_DIM] = (res_b[i] + res_b[i + 1]).astype(BF16)


def _attn_sample_call(proj, ca_kt, ca_v, cb_kt, cb_v, e, bias_a, sink, bias_b, nb):
    t = proj.shape[0]
    nbatch = t // CHUNK
    tm = nb * CHUNK
    in_specs = [
        pl.BlockSpec((tm, 512), lambda i: (i, 0)),
        pl.BlockSpec((tm, 512), lambda i: (i, 1)),
        pl.BlockSpec((tm, 512), lambda i: (i, 2)),
        pl.BlockSpec((tm, 512), lambda i: (i, 3)),
        pl.BlockSpec((tm, 256), lambda i: (i, 8)),
        pl.BlockSpec((None, nb, A_KV_HEADS, HEAD_DIM, A_PREV), lambda i: (e, i, 0, 0, 0)),
        pl.BlockSpec((None, nb, A_PREV, A_KVW), lambda i: (e, i, 0, 0)),
        pl.BlockSpec((None, nb, B_HEADS, HEAD_DIM, B_PREV), lambda i: (e, i, 0, 0, 0)),
        pl.BlockSpec((None, nb, B_PREV, B_W), lambda i: (e, i, 0, 0)),
        _const_spec(bias_a.shape), _layer_spec(sink, e), _layer_spec(bias_b, e),
    ]
    return pl.pallas_call(
        functools.partial(_attn_sample_kernel, nb=nb),
        grid=(nbatch // nb,),
        in_specs=in_specs,
        out_specs=pl.BlockSpec((tm, D_MODEL), lambda i: (i, 0)),
        out_shape=jax.ShapeDtypeStruct((t, D_MODEL), BF16),
        compiler_params=_params(1),
        name="attn_sample",
    )(proj, proj, proj, proj, proj, ca_kt, ca_v, cb_kt, cb_v, bias_a, sink, bias_b)


M_PAIRS = M_HEADS // 2
M_QK_PAIR = 2 * M_DK
M_V_PAIR = 2 * M_DV


def _split3(x):
    hi = x.astype(BF16)
    r1 = x - hi.astype(F32)
    mid = r1.astype(BF16)
    return hi, mid, (r1 - mid.astype(F32)).astype(BF16)


def _widen_heads(x):
    first = lax.broadcasted_iota(jnp.int32, (x.shape[0], M_QK_PAIR), 1) < M_DK
    outs = []
    for p in range(M_PAIRS):
        src = x[:, p * M_QK_PAIR:(p + 1) * M_QK_PAIR]
        swapped = pltpu.roll(src, M_DK, 1)
        outs += [jnp.where(first, src, swapped), jnp.where(first, swapped, src)]
    return jnp.concatenate(outs, axis=1)


def _mlstm_chunks(chunks, states, chained):
    L = CHUNK
    n_chunks = len(chunks)
    wide = (L, M_QKW)
    row = lax.broadcasted_iota(jnp.int32, wide, 0)
    pos = lax.broadcasted_iota(jnp.int32, wide, 1) % M_DK
    t_row = lax.broadcasted_iota(jnp.int32, (L, L), 0)
    t_col = lax.broadcasted_iota(jnp.int32, (L, L), 1)
    tril = jnp.where(t_col <= t_row, 1.0, 0.0).astype(BF16)
    lane_qk = lax.broadcasted_iota(jnp.int32, (L, M_QK_PAIR), 1)
    lane_v = lax.broadcasted_iota(jnp.int32, (L, M_V_PAIR), 1)
    zero_b = jnp.zeros((), BF16)
    bd_r = lax.broadcasted_iota(jnp.int32, (M_QK_PAIR, M_V_PAIR), 0) < M_DK
    bd_c = lax.broadcasted_iota(jnp.int32, (M_QK_PAIR, M_V_PAIR), 1) < M_DV
    on_diag = bd_r == bd_c
    ones_bd = jnp.where(on_diag, 1.0, 0.0).astype(BF16)
    qk_lanes = [slice(p * M_QK_PAIR, (p + 1) * M_QK_PAIR) for p in range(M_PAIRS)]
    v_lanes = [slice(p * M_V_PAIR, (p + 1) * M_V_PAIR) for p in range(M_PAIRS)]

    pre = []
    for q, k, v, o, gates in chunks:
        ig = gates[:, 0:M_QKW]
        fg = gates[:, M_QKW:2 * M_QKW]
        lf = jnp.minimum(fg, 0.0) - jnp.log(1.0 + jnp.exp(-jnp.abs(fg)))
        hi, mid, lo = _split3(lf)
        b_c = _dot(tril, hi) + _dot(tril, mid) + _dot(tril, lo)
        r_c = ig - b_c
        r_row = jnp.sum(jnp.where(row == pos, r_c, 0.0), axis=0, keepdims=True)
        cm = r_c
        for sh in (1, 2, 4, 8, 16, 32):
            cm = jnp.maximum(cm, jnp.where(row >= sh, pltpu.roll(cm, sh, 0), NEG_INF))
        s_parts, v_bd = [], []
        for p in range(M_PAIRS):
            kp, vp = k[:, qk_lanes[p]], v[:, v_lanes[p]]
            k_bd = jnp.concatenate([jnp.where(lane_qk < M_DK, kp, zero_b),
                                    jnp.where(lane_qk >= M_DK, kp, zero_b)], axis=0)
            v_bd.append(jnp.concatenate([jnp.where(lane_v < M_DV, vp, zero_b),
                                         jnp.where(lane_v >= M_DV, vp, zero_b)], axis=0))
            s_parts.append(_dot_nt(q[:, qk_lanes[p]], k_bd))
        pre.append((b_c, r_c, r_row, cm, jnp.concatenate(s_parts, axis=1), v_bd))

    m_prevs, m_state = [], states[0][2]
    for c, (b_c, _, _, cm, _, _) in enumerate(pre):
        m_prev = m_state if chained else states[c][2]
        m_prevs.append(m_prev)
        m_state = b_c[L - 1:L, :] + jnp.maximum(m_prev, cm[L - 1:L, :])
    m_finals = [m_state] if chained else [b_c[L - 1:L, :] + jnp.maximum(m_prevs[c], cm[L - 1:L, :])
                                          for c, (b_c, _, _, cm, _, _) in enumerate(pre)]

    mid_vals = []
    for (q, k, v, o, gates), (b_c, r_c, r_row, cm, s_all, v_bd), m_prev in zip(chunks, pre, m_prevs):
        mm = jnp.maximum(m_prev, cm)
        w = jnp.exp(jnp.where(pos <= row, r_row - mm, NEG_INF))
        qg = jnp.exp(m_prev - mm) * q.astype(F32)
        sc = s_all * w
        mm_last = mm[L - 1:L, :]
        decay = jnp.exp(m_prev - mm_last)
        wk = jnp.exp(r_c - mm_last) * k.astype(F32)
        floor = _widen_heads(jnp.exp(-b_c - mm))
        mid_vals.append((sc, qg, decay, wk, floor))

    n_prevs, n_state = [], states[0][1]
    n_finals = []
    for c, (_, _, decay, wk, _) in enumerate(mid_vals):
        n_prev = n_state if chained else states[c][1]
        n_prevs.append(n_prev)
        n_state = decay * n_prev + jnp.sum(wk, axis=0, keepdims=True)
        n_finals.append(n_state)
    if chained:
        n_finals = n_finals[-1:]

    dens, upds = [], []
    for (q, k, v, o, gates), (sc, qg, decay, wk, _), n_prev in zip(chunks, mid_vals, n_prevs):
        d_hi, d_mid, _ = _split3(sc + qg * n_prev)
        dens.append(jnp.concatenate([_dot(d_hi[:, l], ones_bd) + _dot(d_mid[:, l], ones_bd) for l in qk_lanes], axis=1))
        wk_b = wk.astype(BF16)
        upds.append([jnp.where(on_diag, _dot_tn(wk_b[:, qk_lanes[p]], v[:, v_lanes[p]]), 0.0) for p in range(M_PAIRS)])

    nums, c_state, c_finals = [], states[0][0], []
    for c, ((sc, qg, decay, _, _), (_, _, _, _, _, v_bd)) in enumerate(zip(mid_vals, pre)):
        c_prev = c_state if chained else states[c][0]
        sc_b, qg_b = sc.astype(BF16), qg.astype(BF16)
        decay_v = _widen_heads(jnp.broadcast_to(decay, (8, M_QKW)))[0:1]
        parts, c_state = [], []
        for p in range(M_PAIRS):
            lhs = jnp.concatenate([sc_b[:, qk_lanes[p]], qg_b[:, qk_lanes[p]]], axis=1)
            rhs = jnp.concatenate([v_bd[p], c_prev[p].astype(BF16)], axis=0)
            parts.append(_dot(lhs, rhs))
            c_state.append(decay_v[:, v_lanes[p]] * c_prev[p] + upds[c][p])
        nums.append(jnp.concatenate(parts, axis=1))
        c_finals.append(c_state)
    if chained:
        c_finals = c_finals[-1:]

    outs = []
    for (q, k, v, o, gates), num, den, (_, _, _, _, floor) in zip(chunks, nums, dens, mid_vals):
        hs = num / jnp.maximum(jnp.abs(den), floor)
        outs.append((hs * (1.0 / (1.0 + jnp.exp(-o.astype(F32))))).astype(BF16))
    return outs, list(zip(c_finals, n_finals, m_finals))


def _mlstm_kernel(q_ref, k_ref, v_ref, o_ref, gates_ref, bias_ref, c_in, n_in, m_in,
                  hg_ref, c_out, n_out, m_out, c_s, n_s, m_s, *, seqs, chunks):
    assert seqs == 1 or chunks == 1
    chained = seqs == 1
    j = pl.program_id(1)

    if chained:
        @pl.when(j == 0)
        def _():
            c_s[...] = c_in[0]
            n_s[...] = n_in[0]
            m_s[...] = m_in[0]
        states = [([c_s[p] for p in range(M_PAIRS)], n_s[...], m_s[...])]
    else:
        states = [([c_in[s, p] for p in range(M_PAIRS)], n_in[s], m_in[s]) for s in range(seqs)]

    data = []
    for c in range(seqs * chunks):
        rows = slice(c * CHUNK, (c + 1) * CHUNK)
        data.append((q_ref[rows, :], k_ref[rows, :], v_ref[rows, :], o_ref[rows, :], gates_ref[rows, :] + bias_ref[...]))
    outs, finals = _mlstm_chunks(data, states, chained)
    for c, out in enumerate(outs):
        hg_ref[c * CHUNK:(c + 1) * CHUNK, :] = out

    if chained:
        c_fin, n_fin, m_fin = finals[0]
        for p in range(M_PAIRS):
            c_s[p] = c_fin[p]
        n_s[...] = n_fin
        m_s[...] = m_fin

        @pl.when(j == pl.num_programs(1) - 1)
        def _():
            c_out[0] = c_s[...]
            n_out[0] = n_s[...]
            m_out[0] = m_s[...]
    else:
        for s, (c_fin, n_fin, m_fin) in enumerate(finals):
            for p in range(M_PAIRS):
                c_out[s, p] = c_fin[p]
            n_out[s] = n_fin
            m_out[s] = m_fin


def _mlstm_call(proj, gates, bias, bias_layer, c0, n0, m0, layer, seqs_per_step, chunks_per_step):
    t = proj.shape[0]
    nseq = c0.shape[1]
    rows = seqs_per_step * chunks_per_step * CHUNK
    nsteps = t // (nseq * chunks_per_step * CHUNK)
    assert nseq % seqs_per_step == 0 and (seqs_per_step == 1 or nsteps == 1)
    blk = lambda s, j: s * nsteps + j
    c_blk = (seqs_per_step, M_PAIRS, M_QK_PAIR, M_V_PAIR)
    r_blk = (seqs_per_step, 1, M_QKW)
    in_specs = [
        pl.BlockSpec((rows, M_QKW), lambda s, j: (blk(s, j), 0)),
        pl.BlockSpec((rows, M_QKW), lambda s, j: (blk(s, j), 1)),
        pl.BlockSpec((rows, M_VW), lambda s, j: (blk(s, j), 1)),
        pl.BlockSpec((rows, M_VW), lambda s, j: (blk(s, j), 2)),
        pl.BlockSpec((rows, 2 * M_QKW), lambda s, j: (blk(s, j), 0)),
        _layer_spec(bias, bias_layer),
        pl.BlockSpec((None,) + c_blk, lambda s, j: (layer, s, 0, 0, 0)),
        pl.BlockSpec((None,) + r_blk, lambda s, j: (layer, s, 0, 0)),
        pl.BlockSpec((None,) + r_blk, lambda s, j: (layer, s, 0, 0)),
    ]
    out_specs = (
        pl.BlockSpec((rows, M_VW), lambda s, j: (blk(s, j), 0)),
        pl.BlockSpec(c_blk, lambda s, j: (s, 0, 0, 0)),
        pl.BlockSpec(r_blk, lambda s, j: (s, 0, 0)),
        pl.BlockSpec(r_blk, lambda s, j: (s, 0, 0)),
    )
    out_shape = (
        jax.ShapeDtypeStruct((t, M_VW), BF16),
        jax.ShapeDtypeStruct((nseq,) + c_blk[1:], F32),
        jax.ShapeDtypeStruct((nseq,) + r_blk[1:], F32),
        jax.ShapeDtypeStruct((nseq,) + r_blk[1:], F32),
    )
    return pl.pallas_call(
        functools.partial(_mlstm_kernel, seqs=seqs_per_step, chunks=chunks_per_step),
        grid=(nseq // seqs_per_step, nsteps),
        in_specs=in_specs,
        out_specs=out_specs,
        out_shape=out_shape,
        scratch_shapes=[pltpu.VMEM(c_blk[1:], F32), pltpu.VMEM(r_blk[1:], F32), pltpu.VMEM(r_blk[1:], F32)],
        compiler_params=_params(2),
        name="mlstm",
    )(proj, proj, proj, proj, gates, bias, c0, n0, m0)


def _mlstm_state_to_kernel(c, n, m):
    lead = c.shape[:-3]
    cp = c.reshape(lead + (M_PAIRS, 2, M_DK, M_DV))
    z = jnp.zeros_like(cp[..., 0, :, :])
    top = jnp.concatenate([cp[..., 0, :, :], z], axis=-1)
    bot = jnp.concatenate([z, cp[..., 1, :, :]], axis=-1)
    c_bd = jnp.concatenate([top, bot], axis=-2)
    n_row = n.reshape(lead + (1, M_QKW))
    m_row = jnp.repeat(m, M_DK, axis=-1).reshape(lead + (1, M_QKW))
    return c_bd, n_row, m_row


def _mlstm_state_from_kernel(c_bd, n_row, m_row):
    lead = c_bd.shape[:-3]
    top = c_bd[..., :M_DK, :M_DV]
    bot = c_bd[..., M_DK:, M_DV:]
    c = jnp.stack([top, bot], axis=-3).reshape(lead + (M_HEADS, M_DK, M_DV))
    n = n_row.reshape(lead + (M_HEADS, M_DK))
    m = m_row.reshape(lead + (M_HEADS, M_DK))[..., 0]
    return c, n, m


def _post_kernel(x_ref, a_ref, wout_ref, g_ref, wq_ref, mk_ref, mv_ref, wo_ref, o_ref, att_s, *, nb, tb):
    x1 = x_ref[...] + _dot(a_ref[...], wout_ref[...])
    q = _dot(_rms(x1, g_ref[...]).astype(BF16), wq_ref[...]).astype(BF16)
    scale = MEM_HD ** -0.5
    units = [(b, h) for b in range(nb) for h in range(MEM_HEADS)]
    where = lambda b, h: (slice(b * tb, (b + 1) * tb), slice(h * MEM_HD, (h + 1) * MEM_HD))
    scores = [_dot_nt(q[where(b, h)], mk_ref[b, h]) * scale for b, h in units]
    probs = []
    for s in scores:
        e = jnp.exp(s - jnp.max(s, axis=-1, keepdims=True))
        probs.append((e / jnp.sum(e, axis=-1, keepdims=True)).astype(BF16))
    for (b, h), p in zip(units, probs):
        att_s[where(b, h)] = _dot(p, mv_ref[b, h]).astype(BF16)
    o_ref[...] = x1 + _dot(att_s[...], wo_ref[...])


def _post_call(x, a, w_out, wl, g, w_q, mk, mv, w_o, layer, tm, tb):
    t = x.shape[0]
    nb = tm // tb
    shared = mk.shape[1] == 1
    tail = (0,) * (mk.ndim - 2)
    mem_map = (lambda i: (layer, 0) + tail) if shared else (lambda i: (layer, i) + tail)
    mem_blk = (None, nb) + mk.shape[2:]
    row = lambda i: (i, 0)
    in_specs = [
        pl.BlockSpec((tm, D_MODEL), row),
        pl.BlockSpec((tm, a.shape[1]), row),
        _layer_spec(w_out, wl),
        _layer_spec(g, layer),
        _layer_spec(w_q, layer),
        pl.BlockSpec(mem_blk, mem_map),
        pl.BlockSpec(mem_blk, mem_map),
        _layer_spec(w_o, layer),
    ]
    return pl.pallas_call(
        functools.partial(_post_kernel, nb=nb, tb=tb),
        grid=(t // tm,),
        in_specs=in_specs,
        out_specs=pl.BlockSpec((tm, D_MODEL), row),
        out_shape=jax.ShapeDtypeStruct((t, D_MODEL), F32),
        scratch_shapes=[pltpu.VMEM((tm, D_MODEL), BF16)],
        compiler_params=_params(1),
        name="post_cross",
    )(x, a, w_out, g, w_q, mk, mv, w_o)


def _conv3(u, cw, hists, seq_rows):
    w0, w1, w2 = cw[0:1], cw[1:2], cw[2:3]
    c = pltpu.roll(u, 2, 0) * w0 + pltpu.roll(u, 1, 0) * w1 + u * w2
    row = lax.broadcasted_iota(jnp.int32, (8, u.shape[1]), 0)
    pieces = []
    for s, hist in enumerate(hists):
        base = s * seq_rows
        t8 = u[base:base + 8]
        h0, h1 = hist[0:1], hist[1:2]
        u1 = jnp.where(row == 0, h1, pltpu.roll(t8, 1, 0))
        u2 = jnp.where(row == 0, h0, jnp.where(row == 1, h1, pltpu.roll(t8, 2, 0)))
        pieces += [u2 * w0 + u1 * w1 + t8 * w2, c[base + 8:base + seq_rows]]
    return jnp.concatenate(pieces, axis=0)


def _ffn_kernel(*refs, seq_rows, carry, final_norm):
    x_ref, g_ref, win_ref, cw_ref, wout_ref = refs[:5]
    pos = 5
    hist_ref = None
    if not carry:
        hist_ref = refs[pos]
        pos += 1
    gfin_ref = None
    if final_norm:
        gfin_ref = refs[pos]
        pos += 1
    o_ref, cs_ref, act_s = refs[pos:pos + 3]
    hist_s = refs[pos + 3] if carry else None

    tm = x_ref.shape[0]
    nseq = tm // seq_rows
    x = x_ref[...]
    h = _rms(x, g_ref[...]).astype(BF16)

    if carry:
        @pl.when(pl.program_id(0) == 0)
        def _():
            hist_s[...] = jnp.zeros_like(hist_s)

    for j in range(D_FF // FF_CHUNK):
        halves = []
        for half in range(2):
            c0 = half * D_FF + j * FF_CHUNK
            cols = slice(c0, c0 + FF_CHUNK)
            u = _dot(h, win_ref[:, cols])
            if carry:
                hists = [hist_s[:, cols]]
            else:
                hists = [hist_ref[s, :, cols] for s in range(nseq)]
            halves.append(_conv3(u, cw_ref[:, cols], hists, seq_rows))
            for s in range(nseq):
                last2 = u[(s + 1) * seq_rows - 2:(s + 1) * seq_rows]
                if carry:
                    hist_s[:, cols] = last2
                    cs_ref[:, cols] = last2
                else:
                    cs_ref[s, :, cols] = last2
        ca, cg = halves
        act = ca * (1.0 / (1.0 + jnp.exp(-ca))) * cg
        act_s[:, j * FF_CHUNK:(j + 1) * FF_CHUNK] = act.astype(BF16)

    y = x + _dot(act_s[...], wout_ref[...])
    if final_norm:
        y = _rms(y, gfin_ref[...])
    o_ref[...] = y


def _ffn_call(x, g, w_in, conv_w, w_out, layer, hist, g_final, tm):
    t = x.shape[0]
    carry = hist is None
    seq_rows = tm if carry else CHUNK
    nseq = tm // seq_rows
    row = lambda i: (i, 0)
    args = [x, g, w_in, conv_w, w_out]
    in_specs = [pl.BlockSpec((tm, D_MODEL), row), _layer_spec(g, layer),
                _layer_spec(w_in, layer), _layer_spec(conv_w, layer), _layer_spec(w_out, layer)]
    if carry:
        cs_shape, cs_spec = (2, 2 * D_FF), _const_spec((2, 2 * D_FF))
    else:
        args.append(hist)
        in_specs.append(pl.BlockSpec((None, nseq, 2, 2 * D_FF), lambda i: (layer, i, 0, 0)))
        cs_shape, cs_spec = hist.shape[1:], pl.BlockSpec((nseq, 2, 2 * D_FF), lambda i: (i, 0, 0))
    if g_final is not None:
        args.append(g_final.reshape(1, D_MODEL))
        in_specs.append(_const_spec((1, D_MODEL)))
    scratch = [pltpu.VMEM((tm, D_FF), BF16)]
    if carry:
        scratch.append(pltpu.VMEM((2, 2 * D_FF), F32))
    return pl.pallas_call(
        functools.partial(_ffn_kernel, seq_rows=seq_rows, carry=carry, final_norm=g_final is not None),
        grid=(t // tm,),
        in_specs=in_specs,
        out_specs=(pl.BlockSpec((tm, D_MODEL), row), cs_spec),
        out_shape=(jax.ShapeDtypeStruct((t, D_MODEL), F32), jax.ShapeDtypeStruct(cs_shape, F32)),
        scratch_shapes=scratch,
        compiler_params=_params(1),
        name="conv_ffn",
    )(*args)


def _alibi(n_q, n_k):
    q = np.arange(n_q)[:, None]
    dist = np.abs(q + A_PREV - np.arange(n_k)[None, :]).astype(np.float32)
    slopes = (2.0 ** (-8.0 * np.arange(1, A_HEADS + 1) / A_HEADS)).astype(np.float32)
    return -slopes[:, None, None] * dist[None]


def _band_mask(n_q, n_k, n_prev_chunks, first_valid_chunk):
    qc = (np.arange(n_q) // CHUNK)[:, None]
    kc = (np.arange(n_k) // CHUNK)[None, :]
    ok = (kc >= qc) & (kc <= qc + n_prev_chunks) & (kc >= first_valid_chunk)
    return np.where(ok, 0.0, NEG_INF).astype(np.float32)


def _alibi_sample_table():
    return jnp.asarray(_alibi(CHUNK, A_BAND).reshape(A_KV_HEADS, A_GROUP * CHUNK, A_BAND))


def _alibi_prompt_tables():
    n_prev = A_PREV // CHUNK
    base = _alibi(P_SUB, PA_KEYS)
    general = (base + _band_mask(P_SUB, PA_KEYS, n_prev, 0)).reshape(A_KV_HEADS, A_GROUP * P_SUB, PA_KEYS)
    first = (base + _band_mask(P_SUB, PA_KEYS, n_prev, n_prev)).reshape(A_KV_HEADS, A_GROUP * P_SUB, PA_KEYS)
    return jnp.asarray(np.stack([np.stack([first, general]), np.stack([general, general])]))


def _relpos_toeplitz(table, n_q, n_k):
    h, n_rel = table.shape
    n_lo = n_k - B_BAND
    n_hi = n_q + n_k - 1 - n_lo - n_rel
    ext = jnp.concatenate([jnp.broadcast_to(table[:, :1], (h, n_lo + 1)), table,
                           jnp.broadcast_to(table[:, -1:], (h, n_hi))], axis=1)
    rev = ext[:, ::-1][:, None, :]
    length = n_q + n_k - 1
    tiled = jnp.broadcast_to(rev, (h, n_q, length + 1)).reshape(h, n_q * (length + 1))
    skew = tiled[:, :n_q * length].reshape(h, n_q, length)
    return skew[:, :, n_q - 1:n_q - 1 + n_k]


def _relpos_prompt_tables(tables):
    layers, heads, n_rel = tables.shape
    n_prev = B_PREV // CHUNK
    masks = np.stack([_band_mask(P_TILE, PB_KEYS, n_prev, max(n_prev - t * (P_TILE // CHUNK), 0)) for t in range(3)])
    toe = _relpos_toeplitz(tables.reshape(layers * heads, n_rel), P_TILE, PB_KEYS)
    return toe.reshape(layers, 1, heads, P_TILE, PB_KEYS) + jnp.asarray(masks)[None, :, None]


def _row_tile(t):
    return PROMPT_TILE if t % PROMPT_TILE == 0 else 256


def kernel(x_prompt, x_sample, mem_prompt, cache_a_k, cache_a_v, cache_b_k, cache_b_v, state_mlstm_c, state_mlstm_n, state_mlstm_m, cache_mem_k, cache_mem_v, state_ffn_conv, norm_mix, norm_cross, norm_ffn, norm_final, w_in_att, w_out_att, sink_a, relpos_b, w_in_mlstm, b_igate, b_fgate, w_out_mlstm, w_mem_q, w_mem_k, w_mem_v, w_mem_o, w_ffn_in, conv_ffn, w_ffn_out):
    depth = norm_mix.shape[0]
    assert x_prompt.shape[0] == 1
    tp = x_prompt.shape[1]
    nb, ds = x_sample.shape[0], x_sample.shape[1]
    assert ds == CHUNK and tp % PROMPT_TILE == 0
    ts = nb * ds
    tm_p, tm_s = PROMPT_TILE, _row_tile(ts)
    tm_big = WIDE_TILE if tp % WIDE_TILE == 0 else tm_p
    xp = x_prompt[0]
    xs = x_sample.reshape(ts, D_MODEL)

    memkv = _memkv_call(mem_prompt[0], jnp.concatenate([w_mem_k, w_mem_v], axis=0).astype(BF16))
    head_major = lambda m: jnp.transpose(m, (0, 1, 3, 2, 4)).astype(BF16)
    split_heads = lambda m: m.reshape(depth, 1, MEM_TOKENS, MEM_HEADS, MEM_HD)
    p_mk, p_mv = head_major(split_heads(memkv[:depth])), head_major(split_heads(memkv[depth:]))
    s_mk, s_mv = head_major(cache_mem_k), head_major(cache_mem_v)

    g_mix = norm_mix.astype(F32)[:, None]
    g_cross = norm_cross.astype(F32)[:, None]
    g_ffn = norm_ffn.astype(F32)[:, None]
    wa = w_in_att
    scale = HEAD_DIM ** -0.5
    w_att = jnp.concatenate([wa[:, :, 0:512] * scale, wa[:, :, 768:1280] * scale, wa[:, :, 1280:2304],
                             wa[:, :, 512:768]], axis=2).astype(BF16)
    wm = w_in_mlstm
    k_scale = M_DK ** -0.5
    w_mls = jnp.concatenate([wm[:, :, 0:M_QKW], wm[:, :, M_QKW:2 * M_QKW] * k_scale, wm[:, :, 2 * M_QKW:M_MAIN],
                             wm[:, :, M_MAIN:], jnp.zeros(wm.shape[:2] + (128 - 2 * M_HEADS,), wm.dtype)],
                            axis=2).astype(BF16)
    w_out_a, w_out_m = w_out_att.astype(BF16), w_out_mlstm.astype(BF16)
    w_q, w_o = w_mem_q.astype(BF16), w_mem_o.astype(BF16)
    w_fi, w_fo = w_ffn_in.astype(BF16), w_ffn_out.astype(BF16)
    conv_w = conv_ffn.astype(F32)
    hist_s = state_ffn_conv.astype(F32)
    n_even = cache_a_k.shape[0]
    ca_kt = jnp.transpose(cache_a_k, (0, 1, 3, 4, 2)).astype(BF16)
    cb_kt = jnp.transpose(cache_b_k, (0, 1, 3, 4, 2)).astype(BF16)
    ca_v = cache_a_v.reshape(n_even, nb, A_PREV, A_KVW).astype(BF16)
    cb_v = cache_b_v.reshape(n_even, nb, B_PREV, B_W).astype(BF16)
    st_c, st_n, st_m = _mlstm_state_to_kernel(state_mlstm_c.astype(F32), state_mlstm_n.astype(F32),
                                              state_mlstm_m.astype(F32))
    zc = jnp.zeros((1, 1, M_PAIRS, M_QK_PAIR, M_V_PAIR), F32)
    zn = jnp.zeros((1, 1, 1, M_QKW), F32)
    zm = jnp.zeros((1, 1, 1, M_QKW), F32)

    bias_a_s = _alibi_sample_table()
    bias_a_p = _alibi_prompt_tables()
    sinks = sink_a.astype(F32)
    sink_p = jnp.repeat(sinks, P_SUB, axis=1).reshape(n_even, A_KV_HEADS, A_GROUP * P_SUB, 1)
    sink_s = jnp.repeat(sinks, CHUNK, axis=1).reshape(n_even, A_KV_HEADS, A_GROUP * CHUNK, 1)
    rel = relpos_b.astype(F32)
    bias_b_p = _relpos_prompt_tables(rel)
    bias_b_s = _relpos_toeplitz(rel.reshape(n_even * B_HEADS, -1), CHUNK, B_BAND).reshape(n_even, B_HEADS, CHUNK, B_BAND)
    gate_bias = jnp.concatenate([jnp.repeat(b_igate.astype(F32), M_DK, axis=1),
                                 jnp.repeat(b_fgate.astype(F32), M_DK, axis=1)], axis=1)[:, None]
    post_tm_s = tm_s
    attn_nb = 8 if nb % 8 == 0 else 4

    kv_ps, kv_ss, st_p, st_s, p_conv, s_conv = [], [], [], [], [], []
    for l in range(depth):
        if l % 2 == 0:
            e = l // 2
            proj_p, kv_p, kbt_p, kat_p = _att_proj(xp, g_mix, l, w_att, e, tm_big, prompt=True)
            proj_s, kv_s = _att_proj(xs, g_mix, l, w_att, e, tm_s, prompt=False)
            mix_p = _attn_prompt_call(proj_p, kbt_p, kat_p, bias_a_p, sink_p, bias_b_p, e)
            mix_s = _attn_sample_call(proj_s, ca_kt, ca_v, cb_kt, cb_v, e, bias_a_s, sink_s, bias_b_s, attn_nb)
            w_mix_out, wl = w_out_a, e
            kv_ps.append(kv_p[-B_PREV:])
            kv_ss.append(kv_s)
        else:
            o = l // 2
            proj_p, gates_p = _mlstm_proj(xp, g_mix, l, w_mls, o, tm_big)
            proj_s, gates_s = _mlstm_proj(xs, g_mix, l, w_mls, o, tm_s)
            mix_p, c, n, m = _mlstm_call(proj_p, gates_p, gate_bias, o, zc, zn, zm, 0, 1, 4)
            st_p.append((c, n, m))
            mix_s, c, n, m = _mlstm_call(proj_s, gates_s, gate_bias, o, st_c, st_n, st_m, o, 4, 1)
            st_s.append((c, n, m))
            w_mix_out, wl = w_out_m, o

        xp = _post_call(xp, mix_p, w_mix_out, wl, g_cross, w_q, p_mk, p_mv, w_o, l, tm_big, tm_big)
        xs = _post_call(xs, mix_s, w_mix_out, wl, g_cross, w_q, s_mk, s_mv, w_o, l, post_tm_s, CHUNK)

        g_fin = norm_final if l == depth - 1 else None
        xp, cv = _ffn_call(xp, g_ffn, w_fi, conv_w, w_fo, l, None, g_fin, tm_big)
        p_conv.append(cv[None])
        xs, cv = _ffn_call(xs, g_ffn, w_fi, conv_w, w_fo, l, hist_s, g_fin, tm_s)
        s_conv.append(cv)

    kv_p = jnp.stack(kv_ps)
    kv_s = jnp.stack(kv_ss).reshape(n_even, nb, ds, KV32_W)
    a_shape, b_shape = (A_KV_HEADS, HEAD_DIM), (B_HEADS, HEAD_DIM)
    p_c, p_n, p_m = _mlstm_state_from_kernel(*[jnp.stack(z) for z in zip(*st_p)])
    s_c, s_n, s_m = _mlstm_state_from_kernel(*[jnp.stack(z) for z in zip(*st_s)])
    mem_shape = (depth, 1, MEM_TOKENS, MEM_HEADS, MEM_HD)
    return (xp[None], xs.reshape(nb, ds, D_MODEL),
            kv_p[:, None, -A_PREV:, 0:128].reshape((n_even, 1, A_PREV) + a_shape),
            kv_p[:, None, -A_PREV:, 128:256].reshape((n_even, 1, A_PREV) + a_shape),
            kv_p[:, None, :, 256:768].reshape((n_even, 1, B_PREV) + b_shape),
            kv_p[:, None, :, 768:1280].reshape((n_even, 1, B_PREV) + b_shape),
            p_c, p_n, p_m,
            memkv[:depth].reshape(mem_shape), memkv[depth:].reshape(mem_shape), jnp.stack(p_conv),
            kv_s[..., 0:128].reshape((n_even, nb, ds) + a_shape),
            kv_s[..., 128:256].reshape((n_even, nb, ds) + a_shape),
            kv_s[..., 256:768].reshape((n_even, nb, ds) + b_shape),
            kv_s[..., 768:1280].reshape((n_even, nb, ds) + b_shape),
            s_c, s_n, s_m, jnp.stack(s_conv))
```

```python
import functools

import numpy as np
import jax
import jax.numpy as jnp
from jax import lax
from jax.experimental import pallas as pl
from jax.experimental.pallas import tpu as pltpu

F32 = jnp.float32
BF16 = jnp.bfloat16

D_MODEL = 1024
CHUNK = 64
HEAD_DIM = 64
A_HEADS = 8
A_KV_HEADS = 2
A_GROUP = A_HEADS // A_KV_HEADS
A_BAND = 3 * CHUNK
A_PREV = 2 * CHUNK
B_HEADS = 8
B_BAND = 9 * CHUNK
B_PREV = 8 * CHUNK
REL_CLIP = 128
A_QW = A_HEADS * HEAD_DIM
A_KVW = A_KV_HEADS * HEAD_DIM
B_W = B_HEADS * HEAD_DIM
ATT_PROJ = A_QW + 2 * A_KVW + 3 * B_W
KV32_W = 2 * A_KVW + 2 * B_W
M_HEADS = 8
M_DK = 64
M_DV = 128
M_QKW = M_HEADS * M_DK
M_VW = M_HEADS * M_DV
M_MAIN = 2 * M_QKW + 2 * M_VW
MEM_TOKENS = 256
MEM_HEADS = 4
MEM_HD = D_MODEL // MEM_HEADS
D_FF = 2816
FF_CHUNK = 256
EPS = 1e-6
NEG_INF = float("-inf")

VMEM_LIMIT_BYTES = 56 * 1024 * 1024
PROMPT_TILE = 512
WIDE_TILE = 1024
POST_GROUPS = 2


def _params(n_axes):
    return pltpu.CompilerParams(dimension_semantics=("arbitrary",) * n_axes,
                                vmem_limit_bytes=VMEM_LIMIT_BYTES)


def _const_spec(shape):
    zeros = (0,) * len(shape)
    return pl.BlockSpec(shape, lambda *_: zeros)


def _layer_spec(stack, layer):
    idx = (layer,) + (0,) * (stack.ndim - 1)
    return pl.BlockSpec((None,) + stack.shape[1:], lambda *_: idx, pipeline_mode=pl.Buffered(1))


def _rms(x, g):
    return x * lax.rsqrt(jnp.mean(x * x, axis=-1, keepdims=True) + EPS) * g


def _dot(a, b):
    return jnp.dot(a, b, preferred_element_type=F32)


def _dot_nt(a, b):
    return lax.dot_general(a, b, (((1,), (1,)), ((), ())), preferred_element_type=F32)


def _dot_tn(a, b):
    return lax.dot_general(a, b, (((0,), (0,)), ((), ())), preferred_element_type=F32)


def _proj_kernel(x_ref, g_ref, w_ref, *out_refs, plan, transposed, replicated):
    h = _rms(x_ref[...], g_ref[...]).astype(BF16)
    for c0, cw, dsts in plan:
        r = _dot(h, w_ref[:, c0:c0 + cw])
        for oi, oc in dsts:
            out_refs[oi][:, oc:oc + cw] = r.astype(out_refs[oi].dtype)
        if c0 in transposed:
            oi, width = transposed[c0]
            out_refs[oi][...] = r[:, :width].T.astype(out_refs[oi].dtype)
        if c0 in replicated:
            oi, n_cols = replicated[c0]
            first = lax.broadcasted_iota(jnp.int32, (r.shape[0], 128), 1) < 64
            for j in range(0, n_cols, 2):
                out_refs[oi][:, j * 64:(j + 2) * 64] = jnp.where(first, r[:, j:j + 1], r[:, j + 1:j + 2])


def _proj_call(x, g, gl, w, wl, plan, transposed, replicated, out_shapes, out_specs, tm, name):
    t = x.shape[0]
    return pl.pallas_call(
        functools.partial(_proj_kernel, plan=plan, transposed=transposed, replicated=replicated),
        grid=(t // tm,),
        in_specs=[pl.BlockSpec((tm, D_MODEL), lambda i: (i, 0)),
                  _layer_spec(g, gl),
                  _layer_spec(w, wl)],
        out_specs=out_specs,
        out_shape=out_shapes,
        compiler_params=_params(1),
        name=name,
    )(x, g, w)


_ATT_PLAN = (
    (2048, 256, ((0, 2048), (1, 0))),
    (1024, 512, ((0, 1024), (1, 256))),
    (1536, 512, ((0, 1536), (1, 768))),
    (0, 512, ((0, 0),)),
    (512, 512, ((0, 512),)),
)


def _att_proj(x, g, gl, w_perm, wl, tm, prompt):
    t = x.shape[0]
    shapes = [jax.ShapeDtypeStruct((t, ATT_PROJ), BF16)]
    specs = [pl.BlockSpec((tm, ATT_PROJ), lambda i: (i, 0))]
    transposed = {}
    if prompt:
        shapes += [jax.ShapeDtypeStruct((tm, KV32_W), F32), jax.ShapeDtypeStruct((B_W, t), BF16),
                   jax.ShapeDtypeStruct((A_KVW, t), BF16)]
        specs += [_const_spec((tm, KV32_W)), pl.BlockSpec((B_W, tm), lambda i: (0, i)),
                  pl.BlockSpec((A_KVW, tm), lambda i: (0, i))]
        transposed = {1024: (2, B_W), 2048: (3, A_KVW)}
    else:
        shapes.append(jax.ShapeDtypeStruct((t, KV32_W), F32))
        specs.append(pl.BlockSpec((tm, KV32_W), lambda i: (i, 0)))
    return _proj_call(x, g, gl, w_perm, wl, _ATT_PLAN, transposed, {}, tuple(shapes), tuple(specs), tm, "att_proj")


M_GATES_W = 2 * M_QKW
_MLSTM_PLAN = ((M_MAIN, 128, ()),) + tuple((c, 512, ((0, c),)) for c in range(0, M_MAIN, 512))


def _mlstm_proj(x, g, gl, w_pad, wl, tm):
    t = x.shape[0]
    return _proj_call(
        x, g, gl, w_pad, wl, _MLSTM_PLAN, {}, {M_MAIN: (1, 2 * M_HEADS)},
        (jax.ShapeDtypeStruct((t, M_MAIN), BF16), jax.ShapeDtypeStruct((t, M_GATES_W), F32)),
        (pl.BlockSpec((tm, M_MAIN), lambda i: (i, 0)), pl.BlockSpec((tm, M_GATES_W), lambda i: (i, 0))),
        tm, "mlstm_proj")


def _memkv_kernel(m_ref, w_ref, o_ref):
    o_ref[0] = _dot(m_ref[...].astype(BF16), w_ref[0])


def _memkv_call(mem, w):
    n = w.shape[0]
    return pl.pallas_call(
        _memkv_kernel,
        grid=(n,),
        in_specs=[_const_spec(mem.shape), pl.BlockSpec((1, D_MODEL, D_MODEL), lambda i: (i, 0, 0))],
        out_specs=pl.BlockSpec((1, MEM_TOKENS, D_MODEL), lambda i: (i, 0, 0)),
        out_shape=jax.ShapeDtypeStruct((n, MEM_TOKENS, D_MODEL), F32),
        compiler_params=_params(1),
        name="mem_kv",
    )(mem, w)


def _low_half(n):
    return lax.broadcasted_iota(jnp.int32, (n, 128), 1) < HEAD_DIM


def _with_ones(v):
    return jnp.concatenate([v, jnp.ones((v.shape[0], 128), BF16)], axis=1)


def _one_head(v_pair, first):
    zero = jnp.zeros((), BF16)
    low = _low_half(v_pair.shape[0])
    return _with_ones(jnp.where(low, v_pair, zero) if first else jnp.where(low, zero, v_pair))


def _head_on_both_halves(v_pair, first):
    swapped = jnp.concatenate([v_pair[:, HEAD_DIM:], v_pair[:, :HEAD_DIM]], axis=1)
    low = _low_half(v_pair.shape[0])
    return _with_ones(jnp.where(low, v_pair, swapped) if first else jnp.where(low, swapped, v_pair))


def _attend(units):
    scores = [scores_fn() for scores_fn, _, _ in units]
    maxes = []
    for sc, (_, _, sink) in zip(scores, units):
        if all(s.shape == sc[0].shape for s in sc):
            mx = jnp.max(functools.reduce(jnp.maximum, sc), axis=-1, keepdims=True)
        else:
            mx = functools.reduce(jnp.maximum, [jnp.max(s, axis=-1, keepdims=True) for s in sc])
        maxes.append(mx if sink is None else jnp.maximum(mx, sink))
    exps = [[jnp.exp(s - mx).astype(BF16) for s in sc] for sc, mx in zip(scores, maxes)]
    outs = []
    for es, (_, values_fn, _) in zip(exps, units):
        outs.append(functools.reduce(lambda a, b: a + b, [_dot(e, v) for e, v in zip(es, values_fn())]))
    results = []
    for out, mx, (_, _, sink) in zip(outs, maxes, units):
        den = out[:, 128:256]
        if sink is not None:
            den = den + jnp.exp(sink - mx)
        results.append(out[:, 0:128] / den)
    return results


P_TILE = 4 * CHUNK
P_SUB = 2 * CHUNK
PB_KEYS = B_PREV + P_TILE
PA_KEYS = A_PREV + P_SUB


def _attn_prompt_kernel(qa_ref, qb_ref, kt2_ref, kt1_ref, kt0_ref, vb2_ref, vb1_ref, vb0_ref, kat_ref, katp_ref,
                        kva_ref, kvap_ref, bias_a_ref, sink_ref, bias_b_ref, o_ref):

    def unit_a(sb, g):
        q_rows = slice(sb * P_SUB, (sb + 1) * P_SUB)
        hd = slice(g * HEAD_DIM, (g + 1) * HEAD_DIM)
        if sb == 0:
            blocks = [(lambda: katp_ref[hd, :], None, 0, A_PREV),
                      (lambda: kat_ref[hd, 0:P_SUB], slice(0, P_SUB), A_PREV, P_SUB)]
        else:
            k_rows = slice(sb * P_SUB - A_PREV, (sb + 1) * P_SUB)
            blocks = [(lambda: kat_ref[hd, k_rows], k_rows, 0, PA_KEYS)]

        def scores():
            qs = jnp.concatenate([qa_ref[q_rows, h * HEAD_DIM:(h + 1) * HEAD_DIM]
                                  for h in range(g * A_GROUP, (g + 1) * A_GROUP)], axis=0)
            return [_dot(qs, kt()) + bias_a_ref[sb, g, :, c0:c0 + n] for kt, _, c0, n in blocks]

        def values():
            rows = [kvap_ref[:, A_KVW:2 * A_KVW] if r is None else kva_ref[r, A_KVW:2 * A_KVW] for _, r, _, _ in blocks]
            return [_head_on_both_halves(vv, g == 0) for vv in rows]

        return scores, values, sink_ref[g]

    kv_blocks = ((kt2_ref, vb2_ref), (kt1_ref, vb1_ref), (kt0_ref, vb0_ref))

    def unit_b(h):
        cols = slice(h * HEAD_DIM, (h + 1) * HEAD_DIM)
        pair_cols = slice((h - h % 2) * HEAD_DIM, (h - h % 2 + 2) * HEAD_DIM)

        def scores():
            q = qb_ref[:, cols]
            return [_dot(q, k_ref[cols, :]) + bias_b_ref[h, :, j * P_TILE:(j + 1) * P_TILE]
                    for j, (k_ref, _) in enumerate(kv_blocks)]

        def values():
            return [_one_head(v_ref[:, pair_cols], h % 2 == 0) for _, v_ref in kv_blocks]

        return scores, values, None

    a_ids = [(sb, g) for sb in range(P_TILE // P_SUB) for g in range(A_KV_HEADS)]
    res = _attend([unit_a(sb, g) for sb, g in a_ids] + [unit_b(h) for h in range(B_HEADS)])
    for (sb, g), o in zip(a_ids, res):
        q_rows = slice(sb * P_SUB, (sb + 1) * P_SUB)
        for j in range(0, A_GROUP, 2):
            pair = jnp.where(_low_half(P_SUB), o[j * P_SUB:(j + 1) * P_SUB], o[(j + 1) * P_SUB:(j + 2) * P_SUB])
            c0 = (g * A_GROUP + j) * HEAD_DIM
            o_ref[q_rows, c0:c0 + 2 * HEAD_DIM] = pair.astype(BF16)
    res_b = res[len(a_ids):]
    for h in range(0, B_HEADS, 2):
        o_ref[:, A_QW + h * HEAD_DIM:A_QW + (h + 2) * HEAD_DIM] = (res_b[h] + res_b[h + 1]).astype(BF16)


def _attn_prompt_call(proj, kbt, kat, bias_a, sink, bias_b, e):
    t = proj.shape[0]
    back2 = lambda i: jnp.maximum(i - 2, 0)
    back1 = lambda i: jnp.maximum(i - 1, 0)
    prev_a = lambda i: jnp.maximum(i * (P_TILE // A_PREV) - 1, 0)
    in_specs = [
        pl.BlockSpec((P_TILE, 512), lambda i: (i, 0)),
        pl.BlockSpec((P_TILE, 512), lambda i: (i, 1)),
        pl.BlockSpec((B_W, P_TILE), lambda i: (0, back2(i))),
        pl.BlockSpec((B_W, P_TILE), lambda i: (0, back1(i))),
        pl.BlockSpec((B_W, P_TILE), lambda i: (0, i)),
        pl.BlockSpec((P_TILE, 512), lambda i: (back2(i), 3)),
        pl.BlockSpec((P_TILE, 512), lambda i: (back1(i), 3)),
        pl.BlockSpec((P_TILE, 512), lambda i: (i, 3)),
        pl.BlockSpec((A_KVW, P_TILE), lambda i: (0, i)),
        pl.BlockSpec((A_KVW, A_PREV), lambda i: (0, prev_a(i))),
        pl.BlockSpec((P_TILE, 256), lambda i: (i, 8)),
        pl.BlockSpec((A_PREV, 256), lambda i: (prev_a(i), 8)),
        pl.BlockSpec((None,) + bias_a.shape[1:], lambda i: (jnp.minimum(i, 1), 0, 0, 0, 0)),
        _layer_spec(sink, e),
        pl.BlockSpec((None, None) + bias_b.shape[2:], lambda i: (e, jnp.minimum(i, 2), 0, 0, 0)),
    ]
    return pl.pallas_call(
        _attn_prompt_kernel,
        grid=(t // P_TILE,),
        in_specs=in_specs,
        out_specs=pl.BlockSpec((P_TILE, D_MODEL), lambda i: (i, 0)),
        out_shape=jax.ShapeDtypeStruct((t, D_MODEL), BF16),
        compiler_params=_params(1),
        name="attn_prompt",
    )(proj, proj, kbt, kbt, kbt, proj, proj, proj, kat, kat, proj, proj, bias_a, sink, bias_b)


def _attn_sample_kernel(qa_ref, qb_ref, kb_ref, vb_ref, kva_ref, cakt_ref, cav_ref, cbkt_ref, cbv_ref,
                        bias_a_ref, sink_ref, bias_b_ref, o_ref, *, nb):
    def unit_a(b, g):
        rows = slice(b * CHUNK, (b + 1) * CHUNK)
        hd = slice(g * HEAD_DIM, (g + 1) * HEAD_DIM)

        def scores():
            qs = jnp.concatenate([qa_ref[rows, h * HEAD_DIM:(h + 1) * HEAD_DIM]
                                  for h in range(g * A_GROUP, (g + 1) * A_GROUP)], axis=0)
            return [_dot(qs, cakt_ref[b, g]) + bias_a_ref[g, :, 0:A_PREV],
                    _dot_nt(qs, kva_ref[rows, hd]) + bias_a_ref[g, :, A_PREV:A_BAND]]

        def values():
            return [_head_on_both_halves(cav_ref[b], g == 0),
                    _head_on_both_halves(kva_ref[rows, A_KVW:2 * A_KVW], g == 0)]

        return scores, values, sink_ref[g]

    def unit_b(b, h):
        rows = slice(b * CHUNK, (b + 1) * CHUNK)
        cols = slice(h * HEAD_DIM, (h + 1) * HEAD_DIM)
        pair_cols = slice((h - h % 2) * HEAD_DIM, (h - h % 2 + 2) * HEAD_DIM)

        def scores():
            q = qb_ref[rows, cols]
            return [_dot(q, cbkt_ref[b, h]) + bias_b_ref[h, :, 0:B_PREV],
                    _dot_nt(q, kb_ref[rows, cols]) + bias_b_ref[h, :, B_PREV:B_BAND]]

        def values():
            return [_one_head(cbv_ref[b, :, pair_cols], h % 2 == 0), _one_head(vb_ref[rows, pair_cols], h % 2 == 0)]

        return scores, values, None

    a_ids = [(b, g) for b in range(nb) for g in range(A_KV_HEADS)]
    b_ids = [(b, h) for b in range(nb) for h in range(B_HEADS)]
    res = _attend([unit_a(b, g) for b, g in a_ids] + [unit_b(b, h) for b, h in b_ids])
    for (b, g), o in zip(a_ids, res):
        rows = slice(b * CHUNK, (b + 1) * CHUNK)
        for j in range(0, A_GROUP, 2):
            pair = jnp.where(_low_half(CHUNK), o[j * CHUNK:(j + 1) * CHUNK], o[(j + 1) * CHUNK:(j + 2) * CHUNK])
            c0 = (g * A_GROUP + j) * HEAD_DIM
            o_ref[rows, c0:c0 + 2 * HEAD_DIM] = pair.astype(BF16)
    res_b = res[len(a_ids):]
    for i in range(0, len(b_ids), 2):
        b, h = b_ids[i]
        rows = slice(b * CHUNK, (b + 1) * CHUNK)
        o_ref[rows, A_QW + h * HEAD_DIM:A_QW + (h + 2) * HEAD_DIM] = (res_b[i] + res_b[i + 1]).astype(BF16)


def _attn_sample_call(proj, ca_kt, ca_v, cb_kt, cb_v, e, bias_a, sink, bias_b, nb):
    t = proj.shape[0]
    nbatch = t // CHUNK
    tm = nb * CHUNK
    in_specs = [
        pl.BlockSpec((tm, 512), lambda i: (i, 0)),
        pl.BlockSpec((tm, 512), lambda i: (i, 1)),
        pl.BlockSpec((tm, 512), lambda i: (i, 2)),
        pl.BlockSpec((tm, 512), lambda i: (i, 3)),
        pl.BlockSpec((tm, 256), lambda i: (i, 8)),
        pl.BlockSpec((None, nb, A_KV_HEADS, HEAD_DIM, A_PREV), lambda i: (e, i, 0, 0, 0)),
        pl.BlockSpec((None, nb, A_PREV, A_KVW), lambda i: (e, i, 0, 0)),
        pl.BlockSpec((None, nb, B_HEADS, HEAD_DIM, B_PREV), lambda i: (e, i, 0, 0, 0)),
        pl.BlockSpec((None, nb, B_PREV, B_W), lambda i: (e, i, 0, 0)),
        _const_spec(bias_a.shape), _layer_spec(sink, e), _layer_spec(bias_b, e),
    ]
    return pl.pallas_call(
        functools.partial(_attn_sample_kernel, nb=nb),
        grid=(nbatch // nb,),
        in_specs=in_specs,
        out_specs=pl.BlockSpec((tm, D_MODEL), lambda i: (i, 0)),
        out_shape=jax.ShapeDtypeStruct((t, D_MODEL), BF16),
        compiler_params=_params(1),
        name="attn_sample",
    )(proj, proj, proj, proj, proj, ca_kt, ca_v, cb_kt, cb_v, bias_a, sink, bias_b)


M_PAIRS = M_HEADS // 2
M_QK_PAIR = 2 * M_DK
M_V_PAIR = 2 * M_DV


def _split3(x):
    hi = x.astype(BF16)
    r1 = x - hi.astype(F32)
    mid = r1.astype(BF16)
    return hi, mid, (r1 - mid.astype(F32)).astype(BF16)


def _widen_heads(x):
    first = lax.broadcasted_iota(jnp.int32, (x.shape[0], M_QK_PAIR), 1) < M_DK
    outs = []
    for p in range(M_PAIRS):
        src = x[:, p * M_QK_PAIR:(p + 1) * M_QK_PAIR]
        swapped = pltpu.roll(src, M_DK, 1)
        outs += [jnp.where(first, src, swapped), jnp.where(first, swapped, src)]
    return jnp.concatenate(outs, axis=1)


def _mlstm_chunks(chunks, states, chained):
    L = CHUNK
    n_chunks = len(chunks)
    wide = (L, M_QKW)
    row = lax.broadcasted_iota(jnp.int32, wide, 0)
    pos = lax.broadcasted_iota(jnp.int32, wide, 1) % M_DK
    t_row = lax.broadcasted_iota(jnp.int32, (L, L), 0)
    t_col = lax.broadcasted_iota(jnp.int32, (L, L), 1)
    tril = jnp.where(t_col <= t_row, 1.0, 0.0).astype(BF16)
    lane_qk = lax.broadcasted_iota(jnp.int32, (L, M_QK_PAIR), 1)
    lane_v = lax.broadcasted_iota(jnp.int32, (L, M_V_PAIR), 1)
    zero_b = jnp.zeros((), BF16)
    bd_r = lax.broadcasted_iota(jnp.int32, (M_QK_PAIR, M_V_PAIR), 0) < M_DK
    bd_c = lax.broadcasted_iota(jnp.int32, (M_QK_PAIR, M_V_PAIR), 1) < M_DV
    on_diag = bd_r == bd_c
    ones_bd = jnp.where(on_diag, 1.0, 0.0).astype(BF16)
    qk_lanes = [slice(p * M_QK_PAIR, (p + 1) * M_QK_PAIR) for p in range(M_PAIRS)]
    v_lanes = [slice(p * M_V_PAIR, (p + 1) * M_V_PAIR) for p in range(M_PAIRS)]

    pre = []
    for q, k, v, o, gates in chunks:
        ig = gates[:, 0:M_QKW]
        fg = gates[:, M_QKW:2 * M_QKW]
        lf = jnp.minimum(fg, 0.0) - jnp.log(1.0 + jnp.exp(-jnp.abs(fg)))
        hi, mid, lo = _split3(lf)
        b_c = _dot(tril, hi) + _dot(tril, mid) + _dot(tril, lo)
        r_c = ig - b_c
        r_row = jnp.sum(jnp.where(row == pos, r_c, 0.0), axis=0, keepdims=True)
        cm = r_c
        for sh in (1, 2, 4, 8, 16, 32):
            cm = jnp.maximum(cm, jnp.where(row >= sh, pltpu.roll(cm, sh, 0), NEG_INF))
        s_parts, v_bd = [], []
        for p in range(M_PAIRS):
            kp, vp = k[:, qk_lanes[p]], v[:, v_lanes[p]]
            k_bd = jnp.concatenate([jnp.where(lane_qk < M_DK, kp, zero_b),
                                    jnp.where(lane_qk >= M_DK, kp, zero_b)], axis=0)
            v_bd.append(jnp.concatenate([jnp.where(lane_v < M_DV, vp, zero_b),
                                         jnp.where(lane_v >= M_DV, vp, zero_b)], axis=0))
            s_parts.append(_dot_nt(q[:, qk_lanes[p]], k_bd))
        pre.append((b_c, r_c, r_row, cm, jnp.concatenate(s_parts, axis=1), v_bd))

    m_prevs, m_state = [], states[0][2]
    for c, (b_c, _, _, cm, _, _) in enumerate(pre):
        m_prev = m_state if chained else states[c][2]
        m_prevs.append(m_prev)
        m_state = b_c[L - 1:L, :] + jnp.maximum(m_prev, cm[L - 1:L, :])
    m_finals = [m_state] if chained else [b_c[L - 1:L, :] + jnp.maximum(m_prevs[c], cm[L - 1:L, :])
                                          for c, (b_c, _, _, cm, _, _) in enumerate(pre)]

    mid_vals = []
    for (q, k, v, o, gates), (b_c, r_c, r_row, cm, s_all, v_bd), m_prev in zip(chunks, pre, m_prevs):
        mm = jnp.maximum(m_prev, cm)
        w = jnp.exp(jnp.where(pos <= row, r_row - mm, NEG_INF))
        qg = jnp.exp(m_prev - mm) * q.astype(F32)
        sc = s_all * w
        mm_last = mm[L - 1:L, :]
        decay = jnp.exp(m_prev - mm_last)
        wk = jnp.exp(r_c - mm_last) * k.astype(F32)
        floor = _widen_heads(jnp.exp(-b_c - mm))
        mid_vals.append((sc, qg, decay, wk, floor))

    n_prevs, n_state = [], states[0][1]
    n_finals = []
    for c, (_, _, decay, wk, _) in enumerate(mid_vals):
        n_prev = n_state if chained else states[c][1]
        n_prevs.append(n_prev)
        n_state = decay * n_prev + jnp.sum(wk, axis=0, keepdims=True)
        n_finals.append(n_state)
    if chained:
        n_finals = n_finals[-1:]

    dens, upds = [], []
    for (q, k, v, o, gates), (sc, qg, decay, wk, _), n_prev in zip(chunks, mid_vals, n_prevs):
        d_hi, d_mid, _ = _split3(sc + qg * n_prev)
        dens.append(jnp.concatenate([_dot(d_hi[:, l], ones_bd) + _dot(d_mid[:, l], ones_bd) for l in qk_lanes], axis=1))
        wk_b = wk.astype(BF16)
        upds.append([jnp.where(on_diag, _dot_tn(wk_b[:, qk_lanes[p]], v[:, v_lanes[p]]), 0.0) for p in range(M_PAIRS)])

    nums, c_state, c_finals = [], states[0][0], []
    for c, ((sc, qg, decay, _, _), (_, _, _, _, _, v_bd)) in enumerate(zip(mid_vals, pre)):
        c_prev = c_state if chained else states[c][0]
        sc_b, qg_b = sc.astype(BF16), qg.astype(BF16)
        decay_v = _widen_heads(jnp.broadcast_to(decay, (8, M_QKW)))[0:1]
        parts, c_state = [], []
        for p in range(M_PAIRS):
            lhs = jnp.concatenate([sc_b[:, qk_lanes[p]], qg_b[:, qk_lanes[p]]], axis=1)
            rhs = jnp.concatenate([v_bd[p], c_prev[p].astype(BF16)], axis=0)
            parts.append(_dot(lhs, rhs))
            c_state.append(decay_v[:, v_lanes[p]] * c_prev[p] + upds[c][p])
        nums.append(jnp.concatenate(parts, axis=1))
        c_finals.append(c_state)
    if chained:
        c_finals = c_finals[-1:]

    outs = []
    for (q, k, v, o, gates), num, den, (_, _, _, _, floor) in zip(chunks, nums, dens, mid_vals):
        hs = num / jnp.maximum(jnp.abs(den), floor)
        outs.append((hs * (1.0 / (1.0 + jnp.exp(-o.astype(F32))))).astype(BF16))
    return outs, list(zip(c_finals, n_finals, m_finals))


def _mlstm_kernel(q_ref, k_ref, v_ref, o_ref, gates_ref, bias_ref, c_in, n_in, m_in,
                  hg_ref, c_out, n_out, m_out, c_s, n_s, m_s, *, seqs, chunks):
    assert seqs == 1 or chunks == 1
    chained = seqs == 1
    j = pl.program_id(1)

    if chained:
        @pl.when(j == 0)
        def _():
            c_s[...] = c_in[0]
            n_s[...] = n_in[0]
            m_s[...] = m_in[0]
        states = [([c_s[p] for p in range(M_PAIRS)], n_s[...], m_s[...])]
    else:
        states = [([c_in[s, p] for p in range(M_PAIRS)], n_in[s], m_in[s]) for s in range(seqs)]

    data = []
    for c in range(seqs * chunks):
        rows = slice(c * CHUNK, (c + 1) * CHUNK)
        data.append((q_ref[rows, :], k_ref[rows, :], v_ref[rows, :], o_ref[rows, :], gates_ref[rows, :] + bias_ref[...]))
    outs, finals = _mlstm_chunks(data, states, chained)
    for c, out in enumerate(outs):
        hg_ref[c * CHUNK:(c + 1) * CHUNK, :] = out

    if chained:
        c_fin, n_fin, m_fin = finals[0]
        for p in range(M_PAIRS):
            c_s[p] = c_fin[p]
        n_s[...] = n_fin
        m_s[...] = m_fin

        @pl.when(j == pl.num_programs(1) - 1)
        def _():
            c_out[0] = c_s[...]
            n_out[0] = n_s[...]
            m_out[0] = m_s[...]
    else:
        for s, (c_fin, n_fin, m_fin) in enumerate(finals):
            for p in range(M_PAIRS):
                c_out[s, p] = c_fin[p]
            n_out[s] = n_fin
            m_out[s] = m_fin


def _mlstm_call(proj, gates, bias, bias_layer, c0, n0, m0, layer, seqs_per_step, chunks_per_step):
    t = proj.shape[0]
    nseq = c0.shape[1]
    rows = seqs_per_step * chunks_per_step * CHUNK
    nsteps = t // (nseq * chunks_per_step * CHUNK)
    assert nseq % seqs_per_step == 0 and (seqs_per_step == 1 or nsteps == 1)
    blk = lambda s, j: s * nsteps + j
    c_blk = (seqs_per_step, M_PAIRS, M_QK_PAIR, M_V_PAIR)
    r_blk = (seqs_per_step, 1, M_QKW)
    in_specs = [
        pl.BlockSpec((rows, M_QKW), lambda s, j: (blk(s, j), 0)),
        pl.BlockSpec((rows, M_QKW), lambda s, j: (blk(s, j), 1)),
        pl.BlockSpec((rows, M_VW), lambda s, j: (blk(s, j), 1)),
        pl.BlockSpec((rows, M_VW), lambda s, j: (blk(s, j), 2)),
        pl.BlockSpec((rows, 2 * M_QKW), lambda s, j: (blk(s, j), 0)),
        _layer_spec(bias, bias_layer),
        pl.BlockSpec((None,) + c_blk, lambda s, j: (layer, s, 0, 0, 0)),
        pl.BlockSpec((None,) + r_blk, lambda s, j: (layer, s, 0, 0)),
        pl.BlockSpec((None,) + r_blk, lambda s, j: (layer, s, 0, 0)),
    ]
    out_specs = (
        pl.BlockSpec((rows, M_VW), lambda s, j: (blk(s, j), 0)),
        pl.BlockSpec(c_blk, lambda s, j: (s, 0, 0, 0)),
        pl.BlockSpec(r_blk, lambda s, j: (s, 0, 0)),
        pl.BlockSpec(r_blk, lambda s, j: (s, 0, 0)),
    )
    out_shape = (
        jax.ShapeDtypeStruct((t, M_VW), BF16),
        jax.ShapeDtypeStruct((nseq,) + c_blk[1:], F32),
        jax.ShapeDtypeStruct((nseq,) + r_blk[1:], F32),
        jax.ShapeDtypeStruct((nseq,) + r_blk[1:], F32),
    )
    return pl.pallas_call(
        functools.partial(_mlstm_kernel, seqs=seqs_per_step, chunks=chunks_per_step),
        grid=(nseq // seqs_per_step, nsteps),
        in_specs=in_specs,
        out_specs=out_specs,
        out_shape=out_shape,
        scratch_shapes=[pltpu.VMEM(c_blk[1:], F32), pltpu.VMEM(r_blk[1:], F32), pltpu.VMEM(r_blk[1:], F32)],
        compiler_params=_params(2),
        name="mlstm",
    )(proj, proj, proj, proj, gates, bias, c0, n0, m0)


def _mlstm_state_to_kernel(c, n, m):
    lead = c.shape[:-3]
    cp = c.reshape(lead + (M_PAIRS, 2, M_DK, M_DV))
    z = jnp.zeros_like(cp[..., 0, :, :])
    top = jnp.concatenate([cp[..., 0, :, :], z], axis=-1)
    bot = jnp.concatenate([z, cp[..., 1, :, :]], axis=-1)
    c_bd = jnp.concatenate([top, bot], axis=-2)
    n_row = n.reshape(lead + (1, M_QKW))
    m_row = jnp.repeat(m, M_DK, axis=-1).reshape(lead + (1, M_QKW))
    return c_bd, n_row, m_row


def _mlstm_state_from_kernel(c_bd, n_row, m_row):
    lead = c_bd.shape[:-3]
    top = c_bd[..., :M_DK, :M_DV]
    bot = c_bd[..., M_DK:, M_DV:]
    c = jnp.stack([top, bot], axis=-3).reshape(lead + (M_HEADS, M_DK, M_DV))
    n = n_row.reshape(lead + (M_HEADS, M_DK))
    m = m_row.reshape(lead + (M_HEADS, M_DK))[..., 0]
    return c, n, m


def _post_kernel(x_ref, a_ref, wout_ref, g_ref, wq_ref, mk_ref, mv_ref, wo_ref, o_ref, att_s, *, nb, tb):
    tm = x_ref.shape[0]
    rg = tm // POST_GROUPS
    group_rows = [slice(i * rg, (i + 1) * rg) for i in range(POST_GROUPS)]
    x1 = [x_ref[r, :] + _dot(a_ref[r, :], wout_ref[...]) for r in group_rows]
    hq = [_rms(v, g_ref[...]).astype(BF16) for v in x1]
    q = [_dot(v, wq_ref[...]).astype(BF16) for v in hq]
    scale = MEM_HD ** -0.5
    if nb == 1:
        units = [(gi, 0, rg, gi * rg, 0, h) for gi in range(POST_GROUPS) for h in range(MEM_HEADS)]
    else:
        units = [((b * tb) // rg, (b * tb) % rg, tb, b * tb, b, h) for b in range(nb) for h in range(MEM_HEADS)]
    hcols = lambda h: slice(h * MEM_HD, (h + 1) * MEM_HD)
    scores = [_dot_nt(q[gi][lo:lo + n, hcols(h)], mk_ref[b, h]) * scale for gi, lo, n, _, b, h in units]
    probs = []
    for s in scores:
        e = jnp.exp(s - jnp.max(s, axis=-1, keepdims=True))
        probs.append((e / jnp.sum(e, axis=-1, keepdims=True)).astype(BF16))
    for (_, _, n, r0, b, h), p in zip(units, probs):
        att_s[r0:r0 + n, hcols(h)] = _dot(p, mv_ref[b, h]).astype(BF16)
    for r, v in zip(group_rows, x1):
        o_ref[r, :] = v + _dot(att_s[r, :], wo_ref[...])


def _post_call(x, a, w_out, wl, g, w_q, mk, mv, w_o, layer, tm, tb):
    t = x.shape[0]
    nb = tm // tb
    shared = mk.shape[1] == 1
    tail = (0,) * (mk.ndim - 2)
    mem_map = (lambda i: (layer, 0) + tail) if shared else (lambda i: (layer, i) + tail)
    mem_blk = (None, nb) + mk.shape[2:]
    row = lambda i: (i, 0)
    in_specs = [
        pl.BlockSpec((tm, D_MODEL), row),
        pl.BlockSpec((tm, a.shape[1]), row),
        _layer_spec(w_out, wl),
        _layer_spec(g, layer),
        _layer_spec(w_q, layer),
        pl.BlockSpec(mem_blk, mem_map),
        pl.BlockSpec(mem_blk, mem_map),
        _layer_spec(w_o, layer),
    ]
    return pl.pallas_call(
        functools.partial(_post_kernel, nb=nb, tb=tb),
        grid=(t // tm,),
        in_specs=in_specs,
        out_specs=pl.BlockSpec((tm, D_MODEL), row),
        out_shape=jax.ShapeDtypeStruct((t, D_MODEL), F32),
        scratch_shapes=[pltpu.VMEM((tm, D_MODEL), BF16)],
        compiler_params=_params(1),
        name="post_cross",
    )(x, a, w_out, g, w_q, mk, mv, w_o)


def _conv3(u, cw, hists, seq_rows):
    w0, w1, w2 = cw[0:1], cw[1:2], cw[2:3]
    c = pltpu.roll(u, 2, 0) * w0 + pltpu.roll(u, 1, 0) * w1 + u * w2
    row = lax.broadcasted_iota(jnp.int32, (8, u.shape[1]), 0)
    pieces = []
    for s, hist in enumerate(hists):
        base = s * seq_rows
        t8 = u[base:base + 8]
        h0, h1 = hist[0:1], hist[1:2]
        u1 = jnp.where(row == 0, h1, pltpu.roll(t8, 1, 0))
        u2 = jnp.where(row == 0, h0, jnp.where(row == 1, h1, pltpu.roll(t8, 2, 0)))
        pieces += [u2 * w0 + u1 * w1 + t8 * w2, c[base + 8:base + seq_rows]]
    return jnp.concatenate(pieces, axis=0)


def _ffn_kernel(*refs, seq_rows, carry, final_norm):
    x_ref, g_ref, win_ref, cw_ref, wout_ref = refs[:5]
    pos = 5
    hist_ref = None
    if not carry:
        hist_ref = refs[pos]
        pos += 1
    gfin_ref = None
    if final_norm:
        gfin_ref = refs[pos]
        pos += 1
    o_ref, cs_ref, act_s = refs[pos:pos + 3]
    hist_s = refs[pos + 3] if carry else None

    tm = x_ref.shape[0]
    nseq = tm // seq_rows
    x = x_ref[...]
    h = _rms(x, g_ref[...]).astype(BF16)

    if carry:
        @pl.when(pl.program_id(0) == 0)
        def _():
            hist_s[...] = jnp.zeros_like(hist_s)

    for j in range(D_FF // FF_CHUNK):
        halves = []
        for half in range(2):
            c0 = half * D_FF + j * FF_CHUNK
            cols = slice(c0, c0 + FF_CHUNK)
            u = _dot(h, win_ref[:, cols])
            if carry:
                hists = [hist_s[:, cols]]
            else:
                hists = [hist_ref[s, :, cols] for s in range(nseq)]
            halves.append(_conv3(u, cw_ref[:, cols], hists, seq_rows))
            for s in range(nseq):
                last2 = u[(s + 1) * seq_rows - 2:(s + 1) * seq_rows]
                if carry:
                    hist_s[:, cols] = last2
                    cs_ref[:, cols] = last2
                else:
                    cs_ref[s, :, cols] = last2
        ca, cg = halves
        act = ca * (1.0 / (1.0 + jnp.exp(-ca))) * cg
        act_s[:, j * FF_CHUNK:(j + 1) * FF_CHUNK] = act.astype(BF16)

    y = x + _dot(act_s[...], wout_ref[...])
    if final_norm:
        y = _rms(y, gfin_ref[...])
    o_ref[...] = y


def _ffn_call(x, g, w_in, conv_w, w_out, layer, hist, g_final, tm):
    t = x.shape[0]
    carry = hist is None
    seq_rows = tm if carry else CHUNK
    nseq = tm // seq_rows
    row = lambda i: (i, 0)
    args = [x, g, w_in, conv_w, w_out]
    in_specs = [pl.BlockSpec((tm, D_MODEL), row), _layer_spec(g, layer),
                _layer_spec(w_in, layer), _layer_spec(conv_w, layer), _layer_spec(w_out, layer)]
    if carry:
        cs_shape, cs_spec = (2, 2 * D_FF), _const_spec((2, 2 * D_FF))
    else:
        args.append(hist)
        in_specs.append(pl.BlockSpec((None, nseq, 2, 2 * D_FF), lambda i: (layer, i, 0, 0)))
        cs_shape, cs_spec = hist.shape[1:], pl.BlockSpec((nseq, 2, 2 * D_FF), lambda i: (i, 0, 0))
    if g_final is not None:
        args.append(g_final.reshape(1, D_MODEL))
        in_specs.append(_const_spec((1, D_MODEL)))
    scratch = [pltpu.VMEM((tm, D_FF), BF16)]
    if carry:
        scratch.append(pltpu.VMEM((2, 2 * D_FF), F32))
    return pl.pallas_call(
        functools.partial(_ffn_kernel, seq_rows=seq_rows, carry=carry, final_norm=g_final is not None),
        grid=(t // tm,),
        in_specs=in_specs,
        out_specs=(pl.BlockSpec((tm, D_MODEL), row), cs_spec),
        out_shape=(jax.ShapeDtypeStruct((t, D_MODEL), F32), jax.ShapeDtypeStruct(cs_shape, F32)),
        scratch_shapes=scratch,
        compiler_params=_params(1),
        name="conv_ffn",
    )(*args)


def _alibi(n_q, n_k):
    q = np.arange(n_q)[:, None]
    dist = np.abs(q + A_PREV - np.arange(n_k)[None, :]).astype(np.float32)
    slopes = (2.0 ** (-8.0 * np.arange(1, A_HEADS + 1) / A_HEADS)).astype(np.float32)
    return -slopes[:, None, None] * dist[None]


def _band_mask(n_q, n_k, n_prev_chunks, first_valid_chunk):
    qc = (np.arange(n_q) // CHUNK)[:, None]
    kc = (np.arange(n_k) // CHUNK)[None, :]
    ok = (kc >= qc) & (kc <= qc + n_prev_chunks) & (kc >= first_valid_chunk)
    return np.where(ok, 0.0, NEG_INF).astype(np.float32)


def _alibi_sample_table():
    return jnp.asarray(_alibi(CHUNK, A_BAND).reshape(A_KV_HEADS, A_GROUP * CHUNK, A_BAND))


def _alibi_prompt_tables():
    n_prev = A_PREV // CHUNK
    base = _alibi(P_SUB, PA_KEYS)
    general = (base + _band_mask(P_SUB, PA_KEYS, n_prev, 0)).reshape(A_KV_HEADS, A_GROUP * P_SUB, PA_KEYS)
    first = (base + _band_mask(P_SUB, PA_KEYS, n_prev, n_prev)).reshape(A_KV_HEADS, A_GROUP * P_SUB, PA_KEYS)
    return jnp.asarray(np.stack([np.stack([first, general]), np.stack([general, general])]))


def _relpos_toeplitz(table, n_q, n_k):
    h, n_rel = table.shape
    n_lo = n_k - B_BAND
    n_hi = n_q + n_k - 1 - n_lo - n_rel
    ext = jnp.concatenate([jnp.broadcast_to(table[:, :1], (h, n_lo + 1)), table,
                           jnp.broadcast_to(table[:, -1:], (h, n_hi))], axis=1)
    rev = ext[:, ::-1][:, None, :]
    length = n_q + n_k - 1
    tiled = jnp.broadcast_to(rev, (h, n_q, length + 1)).reshape(h, n_q * (length + 1))
    skew = tiled[:, :n_q * length].reshape(h, n_q, length)
    return skew[:, :, n_q - 1:n_q - 1 + n_k]


def _relpos_prompt_tables(tables):
    layers, heads, n_rel = tables.shape
    n_prev = B_PREV // CHUNK
    masks = np.stack([_band_mask(P_TILE, PB_KEYS, n_prev, max(n_prev - t * (P_TILE // CHUNK), 0)) for t in range(3)])
    toe = _relpos_toeplitz(tables.reshape(layers * heads, n_rel), P_TILE, PB_KEYS)
    return toe.reshape(layers, 1, heads, P_TILE, PB_KEYS) + jnp.asarray(masks)[None, :, None]


def _row_tile(t):
    return PROMPT_TILE if t % PROMPT_TILE == 0 else 256


def kernel(x_prompt, x_sample, mem_prompt, cache_a_k, cache_a_v, cache_b_k, cache_b_v, state_mlstm_c, state_mlstm_n, state_mlstm_m, cache_mem_k, cache_mem_v, state_ffn_conv, norm_mix, norm_cross, norm_ffn, norm_final, w_in_att, w_out_att, sink_a, relpos_b, w_in_mlstm, b_igate, b_fgate, w_out_mlstm, w_mem_q, w_mem_k, w_mem_v, w_mem_o, w_ffn_in, conv_ffn, w_ffn_out):
    depth = norm_mix.shape[0]
    assert x_prompt.shape[0] == 1
    tp = x_prompt.shape[1]
    nb, ds = x_sample.shape[0], x_sample.shape[1]
    assert ds == CHUNK and tp % PROMPT_TILE == 0
    ts = nb * ds
    tm_p, tm_s = PROMPT_TILE, _row_tile(ts)
    tm_big = WIDE_TILE if tp % WIDE_TILE == 0 else tm_p
    xp = x_prompt[0]
    xs = x_sample.reshape(ts, D_MODEL)

    memkv = _memkv_call(mem_prompt[0], jnp.concatenate([w_mem_k, w_mem_v], axis=0).astype(BF16))
    head_major = lambda m: jnp.transpose(m, (0, 1, 3, 2, 4)).astype(BF16)
    split_heads = lambda m: m.reshape(depth, 1, MEM_TOKENS, MEM_HEADS, MEM_HD)
    p_mk, p_mv = head_major(split_heads(memkv[:depth])), head_major(split_heads(memkv[depth:]))
    s_mk, s_mv = head_major(cache_mem_k), head_major(cache_mem_v)

    g_mix = norm_mix.astype(F32)[:, None]
    g_cross = norm_cross.astype(F32)[:, None]
    g_ffn = norm_ffn.astype(F32)[:, None]
    wa = w_in_att
    scale = HEAD_DIM ** -0.5
    w_att = jnp.concatenate([wa[:, :, 0:512] * scale, wa[:, :, 768:1280] * scale, wa[:, :, 1280:2304],
                             wa[:, :, 512:768]], axis=2).astype(BF16)
    wm = w_in_mlstm
    k_scale = M_DK ** -0.5
    w_mls = jnp.concatenate([wm[:, :, 0:M_QKW], wm[:, :, M_QKW:2 * M_QKW] * k_scale, wm[:, :, 2 * M_QKW:M_MAIN],
                             wm[:, :, M_MAIN:], jnp.zeros(wm.shape[:2] + (128 - 2 * M_HEADS,), wm.dtype)],
                            axis=2).astype(BF16)
    w_out_a, w_out_m = w_out_att.astype(BF16), w_out_mlstm.astype(BF16)
    w_q, w_o = w_mem_q.astype(BF16), w_mem_o.astype(BF16)
    w_fi, w_fo = w_ffn_in.astype(BF16), w_ffn_out.astype(BF16)
    conv_w = conv_ffn.astype(F32)
    hist_s = state_ffn_conv.astype(F32)
    n_even = cache_a_k.shape[0]
    ca_kt = jnp.transpose(cache_a_k, (0, 1, 3, 4, 2)).astype(BF16)
    cb_kt = jnp.transpose(cache_b_k, (0, 1, 3, 4, 2)).astype(BF16)
    ca_v = cache_a_v.reshape(n_even, nb, A_PREV, A_KVW).astype(BF16)
    cb_v = cache_b_v.reshape(n_even, nb, B_PREV, B_W).astype(BF16)
    st_c, st_n, st_m = _mlstm_state_to_kernel(state_mlstm_c.astype(F32), state_mlstm_n.astype(F32),
                                              state_mlstm_m.astype(F32))
    zc = jnp.zeros((1, 1, M_PAIRS, M_QK_PAIR, M_V_PAIR), F32)
    zn = jnp.zeros((1, 1, 1, M_QKW), F32)
    zm = jnp.zeros((1, 1, 1, M_QKW), F32)

    bias_a_s = _alibi_sample_table()
    bias_a_p = _alibi_prompt_tables()
    sinks = sink_a.astype(F32)
    sink_p = jnp.repeat(sinks, P_SUB, axis=1).reshape(n_even, A_KV_HEADS, A_GROUP * P_SUB, 1)
    sink_s = jnp.repeat(sinks, CHUNK, axis=1).reshape(n_even, A_KV_HEADS, A_GROUP * CHUNK, 1)
    rel = relpos_b.astype(F32)
    bias_b_p = _relpos_prompt_tables(rel)
    bias_b_s = _relpos_toeplitz(rel.reshape(n_even * B_HEADS, -1), CHUNK, B_BAND).reshape(n_even, B_HEADS, CHUNK, B_BAND)
    gate_bias = jnp.concatenate([jnp.repeat(b_igate.astype(F32), M_DK, axis=1),
                                 jnp.repeat(b_fgate.astype(F32), M_DK, axis=1)], axis=1)[:, None]
    post_tm_s = tm_s
    attn_nb = 8 if nb % 8 == 0 else 4

    kv_ps, kv_ss, st_p, st_s, p_conv, s_conv = [], [], [], [], [], []
    for l in range(depth):
        if l % 2 == 0:
            e = l // 2
            proj_p, kv_p, kbt_p, kat_p = _att_proj(xp, g_mix, l, w_att, e, tm_big, prompt=True)
            proj_s, kv_s = _att_proj(xs, g_mix, l, w_att, e, tm_s, prompt=False)
            mix_p = _attn_prompt_call(proj_p, kbt_p, kat_p, bias_a_p, sink_p, bias_b_p, e)
            mix_s = _attn_sample_call(proj_s, ca_kt, ca_v, cb_kt, cb_v, e, bias_a_s, sink_s, bias_b_s, attn_nb)
            w_mix_out, wl = w_out_a, e
            kv_ps.append(kv_p[-B_PREV:])
            kv_ss.append(kv_s)
        else:
            o = l // 2
            proj_p, gates_p = _mlstm_proj(xp, g_mix, l, w_mls, o, tm_big)
            proj_s, gates_s = _mlstm_proj(xs, g_mix, l, w_mls, o, tm_s)
            mix_p, c, n, m = _mlstm_call(proj_p, gates_p, gate_bias, o, zc, zn, zm, 0, 1, 4)
            st_p.append((c, n, m))
            mix_s, c, n, m = _mlstm_call(proj_s, gates_s, gate_bias, o, st_c, st_n, st_m, o, 4, 1)
            st_s.append((c, n, m))
            w_mix_out, wl = w_out_m, o

        xp = _post_call(xp, mix_p, w_mix_out, wl, g_cross, w_q, p_mk, p_mv, w_o, l, tm_big, tm_big)
        xs = _post_call(xs, mix_s, w_mix_out, wl, g_cross, w_q, s_mk, s_mv, w_o, l, post_tm_s, CHUNK)

        g_fin = norm_final if l == depth - 1 else None
        xp, cv = _ffn_call(xp, g_ffn, w_fi, conv_w, w_fo, l, None, g_fin, tm_big)
        p_conv.append(cv[None])
        xs, cv = _ffn_call(xs, g_ffn, w_fi, conv_w, w_fo, l, hist_s, g_fin, tm_s)
        s_conv.append(cv)

    kv_p = jnp.stack(kv_ps)
    kv_s = jnp.stack(kv_ss).reshape(n_even, nb, ds, KV32_W)
    a_shape, b_shape = (A_KV_HEADS, HEAD_DIM), (B_HEADS, HEAD_DIM)
    p_c, p_n, p_m = _mlstm_state_from_kernel(*[jnp.stack(z) for z in zip(*st_p)])
    s_c, s_n, s_m = _mlstm_state_from_kernel(*[jnp.stack(z) for z in zip(*st_s)])
    mem_shape = (depth, 1, MEM_TOKENS, MEM_HEADS, MEM_HD)
    return (xp[None], xs.reshape(nb, ds, D_MODEL),
            kv_p[:, None, -A_PREV:, 0:128].reshape((n_even, 1, A_PREV) + a_shape),
            kv_p[:, None, -A_PREV:, 128:256].reshape((n_even, 1, A_PREV) + a_shape),
            kv_p[:, None, :, 256:768].reshape((n_even, 1, B_PREV) + b_shape),
            kv_p[:, None, :, 768:1280].reshape((n_even, 1, B_PREV) + b_shape),
            p_c, p_n, p_m,
            memkv[:depth].reshape(mem_shape), memkv[depth:].reshape(mem_shape), jnp.stack(p_conv),
            kv_s[..., 0:128].reshape((n_even, nb, ds) + a_shape),
            kv_s[..., 128:256].reshape((n_even, nb, ds) + a_shape),
            kv_s[..., 256:768].reshape((n_even, nb, ds) + b_shape),
            kv_s[..., 768:1280].reshape((n_even, nb, ds) + b_shape),
            s_c, s_n, s_m, jnp.stack(s_conv))
```
